```python
import functools
import math
import jax, jax.numpy as jnp
from jax import lax
import numpy as np

D_MODEL = 1024
BATCH = 4
SEQ = 8192
DEPTH = 1
DEC_BATCH = 128
DEC_SEQ = 8
PAST_LEN = 16384
PAGE_SIZE = 128

N_META = 16
SSM_WIDTH = D_MODEL // 2
SSM_GROUP = 16
SSM_GROUPS = SSM_WIDTH // SSM_GROUP
SSM_STATE = 64
MLA_WIDTH = D_MODEL - SSM_WIDTH
MLA_HEADS = 8
MLA_V_DIM = MLA_WIDTH // MLA_HEADS
MLA_NOPE_DIM = 64
MLA_ROPE_DIM = 32
Q_LORA = 384
KV_LORA = 256
ROPE_THETA = 10000.0
D_FF = 2816
RMS_EPS = 1e-6
Q_BLOCK = 128
ATTN_SCALE = (MLA_NOPE_DIM + MLA_ROPE_DIM) ** -0.5
OFF_Q = SSM_WIDTH
OFF_KV = OFF_Q + Q_LORA
OFF_KR = OFF_KV + KV_LORA
IN_WIDTH = OFF_KR + MLA_ROPE_DIM

kernel_name = 'hymba_s5_mla_macaron_step'


def rms_norm(x, g):
    x32 = x.astype(jnp.float32)
    y = x32 * lax.rsqrt(jnp.mean(x32 * x32, axis=-1, keepdims=True) + RMS_EPS)
    return (y * g.astype(jnp.float32)).astype(x.dtype)


def swiglu(x, wg, wu, wd):
    return (jax.nn.silu(x @ wg) * (x @ wu)) @ wd


def rope(x, pos):
    half = MLA_ROPE_DIM // 2
    inv = ROPE_THETA ** (-jnp.arange(half, dtype=jnp.float32) / half)
    ang = pos.astype(jnp.float32)[:, None] * inv[None, :]
    cos, sin = jnp.cos(ang), jnp.sin(ang)
    if x.ndim == 4:
        cos, sin = cos[:, None, :], sin[:, None, :]
    x32 = x.astype(jnp.float32)
    x1, x2 = x32[..., :half], x32[..., half:]
    return jnp.concatenate([x1 * cos - x2 * sin, x2 * cos + x1 * sin], axis=-1).astype(x.dtype)


def _combine(e1, e2):
    a1, b1 = e1
    a2, b2 = e2
    return a1 * a2, a2 * b1 + b2


def s5_ssm(u, h0, a_re, a_im, log_dt, b_re, b_im, c_re, c_im, d_skip):
    f32 = jnp.float32
    A = lax.complex(a_re.astype(f32), a_im.astype(f32))
    dt = jnp.exp(log_dt.astype(f32))[:, None]
    A_bar = jnp.exp(A * dt)
    B_bar = ((A_bar - 1.0) / A)[..., None] * lax.complex(b_re.astype(f32), b_im.astype(f32))
    C = lax.complex(c_re.astype(f32), c_im.astype(f32))
    nb, L, _ = u.shape
    u32 = u.astype(f32)
    ug = u32.reshape(nb, L, SSM_GROUPS, SSM_GROUP).astype(jnp.complex64)
    bu = jnp.einsum('gpc,blgc->blgp', B_bar, ug)
    bu = bu.at[:, 0].add(A_bar * h0)
    a = jnp.broadcast_to(A_bar, bu.shape)
    _, h = lax.associative_scan(_combine, (a, bu), axis=1)
    y = jnp.einsum('gcp,blgp->blgc', C, h).real.reshape(nb, L, SSM_WIDTH)
    y = y + d_skip.astype(f32) * u32
    return y.astype(u.dtype), h[:, -1]


def attend_prompt(q_lat, q_rope, c, kr):
    nb, L, H, R = q_lat.shape
    n_blk = -(-L // Q_BLOCK)
    pad = n_blk * Q_BLOCK - L
    ql = jnp.pad(q_lat, ((0, 0), (0, pad), (0, 0), (0, 0))).reshape(nb, n_blk, Q_BLOCK, H, R).transpose(1, 0, 2, 3, 4)
    qr = jnp.pad(q_rope, ((0, 0), (0, pad), (0, 0), (0, 0))).reshape(nb, n_blk, Q_BLOCK, H, MLA_ROPE_DIM).transpose(1, 0, 2, 3, 4)
    key_pos = jnp.arange(L)

    def block(args):
        qlb, qrb, start = args
        s = (jnp.einsum('bqhr,bkr->bhqk', qlb, c) + jnp.einsum('bqhd,bkd->bhqk', qrb, kr)).astype(jnp.float32) * ATTN_SCALE
        q_pos = start + jnp.arange(Q_BLOCK)
        s = jnp.where(key_pos[None, :] <= q_pos[:, None], s, -jnp.inf)
        p = jax.nn.softmax(s, axis=-1).astype(c.dtype)
        return jnp.einsum('bhqk,bkr->bqhr', p, c)

    out = lax.map(block, (ql, qr, jnp.arange(n_blk) * Q_BLOCK))
    return out.transpose(1, 0, 2, 3, 4).reshape(nb, n_blk * Q_BLOCK, H, R)[:, :L]


def attend_sample(q_lat, q_rope, c_new, kr_new, c_past, kr_past):
    n_past = c_past.shape[1]
    s_len = q_lat.shape[1]
    s_past = jnp.einsum('bqhr,bkr->bhqk', q_lat, c_past) + jnp.einsum('bqhd,bkd->bhqk', q_rope, kr_past)
    s_new = jnp.einsum('bqhr,bkr->bhqk', q_lat, c_new) + jnp.einsum('bqhd,bkd->bhqk', q_rope, kr_new)
    causal = jnp.arange(s_len)[None, :] <= jnp.arange(s_len)[:, None]
    s_new = jnp.where(causal, s_new.astype(jnp.float32), -jnp.inf)
    s = jnp.concatenate([s_past.astype(jnp.float32), s_new], axis=-1) * ATTN_SCALE
    p = jax.nn.softmax(s, axis=-1).astype(c_new.dtype)
    return (jnp.einsum('bhqk,bkr->bqhr', p[..., :n_past], c_past)
            + jnp.einsum('bhqk,bkr->bqhr', p[..., n_past:], c_new))


def mixer(h, pos, ssm_h0, attend, lw):
    nb, L, _ = h.shape
    proj = h @ lw['w_in']
    u = proj[..., :OFF_Q]
    c_q = proj[..., OFF_Q:OFF_KV]
    c_kv = proj[..., OFF_KV:OFF_KR]
    k_r = proj[..., OFF_KR:IN_WIDTH]
    y_s, h_last = s5_ssm(u, ssm_h0, lw['ssm_a_re'], lw['ssm_a_im'], lw['ssm_log_dt'], lw['ssm_b_re'],
                         lw['ssm_b_im'], lw['ssm_c_re'], lw['ssm_c_im'], lw['ssm_d'])
    z = jax.nn.gelu(y_s)
    y_s = z * jax.nn.sigmoid(z @ lw['w_glu'] + lw['b_glu'])
    q = (rms_norm(c_q, lw['g_q_norm']) @ lw['w_uq']).reshape(nb, L, MLA_HEADS, MLA_NOPE_DIM + MLA_ROPE_DIM)
    q_nope = q[..., :MLA_NOPE_DIM]
    q_rope = rope(q[..., MLA_NOPE_DIM:], pos)
    c_kv = rms_norm(c_kv, lw['g_kv_norm'])
    k_r = rope(k_r, pos)
    q_lat = jnp.einsum('bshd,rhd->bshr', q_nope, lw['w_uk'])
    o_lat = attend(q_lat, q_rope, c_kv, k_r)
    y_a = jnp.einsum('bshr,rhd->bshd', o_lat, lw['w_uv']).reshape(nb, L, MLA_WIDTH)
    y = jnp.concatenate([rms_norm(y_s, lw['g_ssm_out']), rms_norm(y_a, lw['g_mla_out'])], axis=-1) @ lw['w_o']
    return y, (c_kv, k_r, h_last)


def trunk_layer(x, pos, ssm_h0, attend, lw):
    h = rms_norm(x, lw['g_ff1_pre'])
    x = x + 0.5 * rms_norm(swiglu(h, lw['w_ff1_gate'], lw['w_ff1_up'], lw['w_ff1_down']), lw['g_ff1_post'])
    h = rms_norm(x, lw['g_mix_pre'])
    y, st = mixer(h, pos, ssm_h0, attend, lw)
    x = x + rms_norm(y, lw['g_mix_post'])
    h = rms_norm(x, lw['g_ff2_pre'])
    x = x + 0.5 * rms_norm(swiglu(h, lw['w_ff2_gate'], lw['w_ff2_up'], lw['w_ff2_down']), lw['g_ff2_post'])
    return x, st


def setup_inputs(seed: int = 0) -> dict:
    key = jax.random.key(seed)
    ks = iter(jax.random.split(key, 48))
    f32 = jnp.float32
    n_pages = PAST_LEN // PAGE_SIZE
    n_pool = (DEC_BATCH * n_pages * 5) // 4

    def nrm(shape, scale):
        return jax.random.normal(next(ks), shape, f32) * scale

    def gain(n):
        return 1.0 + nrm((DEPTH, n), 0.02)

    perm = jax.random.permutation(next(ks), n_pool)
    page_table = perm[:DEC_BATCH * n_pages].reshape(DEC_BATCH, n_pages).astype(jnp.int32)
    a_im = math.pi * jnp.broadcast_to(jnp.arange(SSM_STATE, dtype=f32), (DEPTH, SSM_GROUPS, SSM_STATE))
    return {
        'x_prompt': nrm((BATCH, SEQ, D_MODEL), 1.0),
        'x_sample': nrm((DEC_BATCH, DEC_SEQ, D_MODEL), 1.0),
        'cache_kv_latent': nrm((DEPTH, n_pool, PAGE_SIZE, KV_LORA), 1.0),
        'cache_k_rope': nrm((DEPTH, n_pool, PAGE_SIZE, MLA_ROPE_DIM), 1.0),
        'state_ssm_re': nrm((DEPTH, DEC_BATCH, SSM_GROUPS, SSM_STATE), 0.3),
        'state_ssm_im': nrm((DEPTH, DEC_BATCH, SSM_GROUPS, SSM_STATE), 0.3),
        'page_table': page_table,
        'meta_tokens': nrm((N_META, D_MODEL), 1.0),
        'g_ff1_pre': gain(D_MODEL),
        'w_ff1_gate': nrm((DEPTH, D_MODEL, D_FF), D_MODEL ** -0.5),
        'w_ff1_up': nrm((DEPTH, D_MODEL, D_FF), D_MODEL ** -0.5),
        'w_ff1_down': nrm((DEPTH, D_FF, D_MODEL), D_FF ** -0.5),
        'g_ff1_post': gain(D_MODEL),
        'g_mix_pre': gain(D_MODEL),
        'w_in': nrm((DEPTH, D_MODEL, IN_WIDTH), D_MODEL ** -0.5),
        'ssm_a_re': -0.5 + nrm((DEPTH, SSM_GROUPS, SSM_STATE), 0.01),
        'ssm_a_im': a_im + nrm((DEPTH, SSM_GROUPS, SSM_STATE), 0.01),
        'ssm_log_dt': jax.random.uniform(next(ks), (DEPTH, SSM_GROUPS), f32, math.log(1e-3), math.log(1e-1)),
        'ssm_b_re': nrm((DEPTH, SSM_GROUPS, SSM_STATE, SSM_GROUP), (2 * SSM_GROUP) ** -0.5),
        'ssm_b_im': nrm((DEPTH, SSM_GROUPS, SSM_STATE, SSM_GROUP), (2 * SSM_GROUP) ** -0.5),
        'ssm_c_re': nrm((DEPTH, SSM_GROUPS, SSM_GROUP, SSM_STATE), SSM_STATE ** -0.5),
        'ssm_c_im': nrm((DEPTH, SSM_GROUPS, SSM_GROUP, SSM_STATE), SSM_STATE ** -0.5),
        'ssm_d': nrm((DEPTH, SSM_WIDTH), 1.0),
        'w_glu': nrm((DEPTH, SSM_WIDTH, SSM_WIDTH), SSM_WIDTH ** -0.5),
        'b_glu': nrm((DEPTH, SSM_WIDTH), 0.01),
        'g_q_norm': gain(Q_LORA),
        'w_uq': nrm((DEPTH, Q_LORA, MLA_HEADS * (MLA_NOPE_DIM + MLA_ROPE_DIM)), Q_LORA ** -0.5),
        'g_kv_norm': gain(KV_LORA),
        'w_uk': nrm((DEPTH, KV_LORA, MLA_HEADS, MLA_NOPE_DIM), KV_LORA ** -0.5),
        'w_uv': nrm((DEPTH, KV_LORA, MLA_HEADS, MLA_V_DIM), KV_LORA ** -0.5),
        'g_ssm_out': gain(SSM_WIDTH),
        'g_mla_out': gain(MLA_WIDTH),
        'w_o': nrm((DEPTH, D_MODEL, D_MODEL), D_MODEL ** -0.5),
        'g_mix_post': gain(D_MODEL),
        'g_ff2_pre': gain(D_MODEL),
        'w_ff2_gate': nrm((DEPTH, D_MODEL, D_FF), D_MODEL ** -0.5),
        'w_ff2_up': nrm((DEPTH, D_MODEL, D_FF), D_MODEL ** -0.5),
        'w_ff2_down': nrm((DEPTH, D_FF, D_MODEL), D_FF ** -0.5),
        'g_ff2_post': gain(D_MODEL),
    }


def reference(x_prompt, x_sample, cache_kv_latent, cache_k_rope, state_ssm_re, state_ssm_im, page_table,
              meta_tokens, g_ff1_pre, w_ff1_gate, w_ff1_up, w_ff1_down, g_ff1_post, g_mix_pre, w_in,
              ssm_a_re, ssm_a_im, ssm_log_dt, ssm_b_re, ssm_b_im, ssm_c_re, ssm_c_im, ssm_d, w_glu, b_glu,
              g_q_norm, w_uq, g_kv_norm, w_uk, w_uv, g_ssm_out, g_mla_out, w_o, g_mix_post,
              g_ff2_pre, w_ff2_gate, w_ff2_up, w_ff2_down, g_ff2_post):
    f32 = jnp.float32
    bp, seq, _ = x_prompt.shape
    db, ds, _ = x_sample.shape
    n_pages = page_table.shape[1]
    past_len = n_pages * cache_kv_latent.shape[2]
    xp = jnp.concatenate([jnp.broadcast_to(meta_tokens.astype(x_prompt.dtype)[None], (bp, N_META, D_MODEL)), x_prompt], axis=1)
    xs = x_sample
    pos_p = jnp.arange(seq + N_META)
    pos_s = past_len + jnp.arange(ds)
    ckv_p_l, kr_p_l, hre_p_l, him_p_l = [], [], [], []
    ckv_s_l, kr_s_l, hre_s_l, him_s_l = [], [], [], []
    for l in range(DEPTH):
        lw = dict(g_ff1_pre=g_ff1_pre[l], w_ff1_gate=w_ff1_gate[l], w_ff1_up=w_ff1_up[l], w_ff1_down=w_ff1_down[l],
                  g_ff1_post=g_ff1_post[l], g_mix_pre=g_mix_pre[l], w_in=w_in[l], ssm_a_re=ssm_a_re[l],
                  ssm_a_im=ssm_a_im[l], ssm_log_dt=ssm_log_dt[l], ssm_b_re=ssm_b_re[l], ssm_b_im=ssm_b_im[l],
                  ssm_c_re=ssm_c_re[l], ssm_c_im=ssm_c_im[l], ssm_d=ssm_d[l], w_glu=w_glu[l], b_glu=b_glu[l],
                  g_q_norm=g_q_norm[l], w_uq=w_uq[l], g_kv_norm=g_kv_norm[l], w_uk=w_uk[l], w_uv=w_uv[l],
                  g_ssm_out=g_ssm_out[l], g_mla_out=g_mla_out[l], w_o=w_o[l], g_mix_post=g_mix_post[l],
                  g_ff2_pre=g_ff2_pre[l], w_ff2_gate=w_ff2_gate[l], w_ff2_up=w_ff2_up[l], w_ff2_down=w_ff2_down[l],
                  g_ff2_post=g_ff2_post[l])
        h0_p = jnp.zeros((bp, SSM_GROUPS, SSM_STATE), jnp.complex64)
        xp, (ckv_p, kr_p, hl_p) = trunk_layer(xp, pos_p, h0_p, attend_prompt, lw)
        c_past = cache_kv_latent[l, page_table].reshape(db, past_len, KV_LORA)
        kr_past = cache_k_rope[l, page_table].reshape(db, past_len, MLA_ROPE_DIM)
        h0_s = lax.complex(state_ssm_re[l].astype(f32), state_ssm_im[l].astype(f32))
        attend_s = functools.partial(attend_sample, c_past=c_past, kr_past=kr_past)
        xs, (ckv_s, kr_s, hl_s) = trunk_layer(xs, pos_s, h0_s, attend_s, lw)
        ckv_p_l.append(ckv_p); kr_p_l.append(kr_p); hre_p_l.append(hl_p.real); him_p_l.append(hl_p.imag)
        ckv_s_l.append(ckv_s); kr_s_l.append(kr_s); hre_s_l.append(hl_s.real); him_s_l.append(hl_s.imag)
    y_prompt = xp[:, N_META:]
    return (y_prompt, xs,
            jnp.stack(ckv_p_l), jnp.stack(kr_p_l), jnp.stack(hre_p_l), jnp.stack(him_p_l),
            jnp.stack(ckv_s_l), jnp.stack(kr_s_l), jnp.stack(hre_s_l), jnp.stack(him_s_l))
```

```python
import functools
import math

import numpy as np
import jax
import jax.numpy as jnp
from jax import lax
from jax.experimental import pallas as pl
from jax.experimental.pallas import tpu as pltpu

F32 = jnp.float32
BF16 = jnp.bfloat16

N_META = 16
SSM_GROUP = 16
MLA_HEADS = 8
MLA_NOPE_DIM = 64
MLA_ROPE_DIM = 32
ROPE_HALF = MLA_ROPE_DIM // 2
ROPE_THETA = 10000.0
RMS_EPS = 1e-6
ATTN_SCALE = (MLA_NOPE_DIM + MLA_ROPE_DIM) ** -0.5

LANES = 128
SUBLANES = 8
VMEM_LIMIT = 56 * 1024 * 1024

ROW_TILE = 512
FF_TILE = 256
ATTN_BQ = 128
ATTN_BK = 512
PAGES_PER_STEP = 16
SSM_CHUNK = 16


def _rms(x, g):
    return (x * lax.rsqrt(jnp.mean(x * x, axis=-1, keepdims=True) + RMS_EPS)) * g


def _row_tile(n):
    return ROW_TILE if n % ROW_TILE == 0 else n


def _cparams(sem):
    return pltpu.CompilerParams(dimension_semantics=sem, vmem_limit_bytes=VMEM_LIMIT)


def _ffn_body(emit_norm, x_ref, gpre_ref, wg_ref, wu_ref, wd_ref, gpost_ref, gnext_ref, *refs):
    if emit_norm:
        y_ref, hn_ref, h_scr, acc_scr = refs
    else:
        y_ref, h_scr, acc_scr = refs
        hn_ref = None
    j = pl.program_id(1)

    @pl.when(j == 0)
    def _():
        h_scr[...] = _rms(x_ref[...], gpre_ref[...]).astype(BF16)
        acc_scr[...] = jnp.zeros_like(acc_scr)

    h = h_scr[...]
    g = jnp.dot(h, wg_ref[...], preferred_element_type=F32)
    u = jnp.dot(h, wu_ref[...], preferred_element_type=F32)
    a = (g * jax.nn.sigmoid(g)) * u
    acc_scr[...] += jnp.dot(a.astype(BF16), wd_ref[...], preferred_element_type=F32)

    @pl.when(j == pl.num_programs(1) - 1)
    def _():
        y = x_ref[...] + 0.5 * _rms(acc_scr[...], gpost_ref[...])
        y_ref[...] = y
        if emit_norm:
            hn_ref[...] = _rms(y, gnext_ref[...]).astype(BF16)


def _ffn(x, g_pre, wg, wu, wd, g_post, g_next, emit_norm, name):
    n, d = x.shape
    d_ff = wg.shape[1]
    tm = _row_tile(n)
    tf = FF_TILE
    assert d_ff % tf == 0
    row = lambda i, j: (i, 0)
    vec = lambda i, j: (0, 0)
    out_shape = [jax.ShapeDtypeStruct((n, d), F32)]
    out_specs = [pl.BlockSpec((tm, d), row)]
    if emit_norm:
        out_shape.append(jax.ShapeDtypeStruct((n, d), BF16))
        out_specs.append(pl.BlockSpec((tm, d), row))
    res = pl.pallas_call(
        functools.partial(_ffn_body, emit_norm),
        grid=(n // tm, d_ff // tf),
        in_specs=[
            pl.BlockSpec((tm, d), row),
            pl.BlockSpec((1, d), vec),
            pl.BlockSpec((d, tf), lambda i, j: (0, j)),
            pl.BlockSpec((d, tf), lambda i, j: (0, j)),
            pl.BlockSpec((tf, d), lambda i, j: (j, 0)),
            pl.BlockSpec((1, d), vec),
            pl.BlockSpec((1, d), vec),
        ],
        out_specs=out_specs,
        out_shape=out_shape,
        scratch_shapes=[pltpu.VMEM((tm, d), BF16), pltpu.VMEM((tm, d), F32)],
        compiler_params=_cparams(("parallel", "arbitrary")),
        name=name,
    )(x, g_pre, wg, wu, wd, g_post, g_next)
    return res if emit_norm else (res[0], None)


def _proj_body(d_u, d_q, d_kv, h_ref, win_ref, gq_ref, wuq_ref, wukt_ref, gkv_ref, cos_ref, sin_ref,
               u_ref, ckv_ref, ckvb_ref, kr_ref, krb_ref, qc_ref, qr_ref):
    proj = jnp.dot(h_ref[...], win_ref[...], preferred_element_type=F32)
    off_q, off_kv, off_kr = d_u, d_u + d_q, d_u + d_q + d_kv
    u_ref[...] = proj[:, :off_q]
    cq = _rms(proj[:, off_q:off_kv], gq_ref[...]).astype(BF16)
    q = jnp.dot(cq, wuq_ref[...], preferred_element_type=F32) * ATTN_SCALE
    n_nope = MLA_HEADS * MLA_NOPE_DIM
    for hd in range(MLA_HEADS):
        qn = q[:, hd * MLA_NOPE_DIM:(hd + 1) * MLA_NOPE_DIM].astype(BF16)
        qc_ref[hd] = jnp.dot(qn, wukt_ref[hd], preferred_element_type=F32).astype(BF16)
    cos = cos_ref[...]
    sin = sin_ref[...]
    r1 = q[:, n_nope:n_nope + LANES]
    r2 = q[:, n_nope + LANES:n_nope + 2 * LANES]
    o1 = r1 * cos - r2 * sin
    o2 = r2 * cos + r1 * sin
    for hd in range(MLA_HEADS):
        sl = slice(hd * ROPE_HALF, (hd + 1) * ROPE_HALF)
        qr_ref[hd] = jnp.concatenate([o1[:, sl], o2[:, sl]], axis=-1).astype(BF16)
    ckv = _rms(proj[:, off_kv:off_kr], gkv_ref[...])
    ckv_ref[...] = ckv
    ckvb_ref[...] = ckv.astype(BF16)
    x1 = proj[:, off_kr:off_kr + ROPE_HALF]
    x2 = proj[:, off_kr + ROPE_HALF:off_kr + MLA_ROPE_DIM]
    c16 = cos[:, :ROPE_HALF]
    s16 = sin[:, :ROPE_HALF]
    kr = jnp.concatenate([x1 * c16 - x2 * s16, x2 * c16 + x1 * s16], axis=-1)
    kr_ref[...] = kr
    krb_ref[...] = kr.astype(BF16)


def _proj(hn, w_in, g_q, w_uq, w_ukt, g_kv, cos, sin, dims, name):
    n, d = hn.shape
    d_u, d_q, d_kv = dims
    tm = _row_tile(n)
    row = lambda i: (i, 0)
    full2 = lambda i: (0, 0)
    full3 = lambda i: (0, 0, 0)
    hrow = lambda i: (0, i, 0)
    return pl.pallas_call(
        functools.partial(_proj_body, d_u, d_q, d_kv),
        grid=(n // tm,),
        in_specs=[
            pl.BlockSpec((tm, d), row),
            pl.BlockSpec(w_in.shape, full2),
            pl.BlockSpec(g_q.shape, full2),
            pl.BlockSpec(w_uq.shape, full2),
            pl.BlockSpec(w_ukt.shape, full3),
            pl.BlockSpec(g_kv.shape, full2),
            pl.BlockSpec((tm, LANES), row),
            pl.BlockSpec((tm, LANES), row),
        ],
        out_specs=[
            pl.BlockSpec((tm, d_u), row),
            pl.BlockSpec((tm, d_kv), row),
            pl.BlockSpec((tm, d_kv), row),
            pl.BlockSpec((tm, MLA_ROPE_DIM), row),
            pl.BlockSpec((tm, MLA_ROPE_DIM), row),
            pl.BlockSpec((MLA_HEADS, tm, d_kv), hrow),
            pl.BlockSpec((MLA_HEADS, tm, MLA_ROPE_DIM), hrow),
        ],
        out_shape=[
            jax.ShapeDtypeStruct((n, d_u), F32),
            jax.ShapeDtypeStruct((n, d_kv), F32),
            jax.ShapeDtypeStruct((n, d_kv), BF16),
            jax.ShapeDtypeStruct((n, MLA_ROPE_DIM), F32),
            jax.ShapeDtypeStruct((n, MLA_ROPE_DIM), BF16),
            jax.ShapeDtypeStruct((MLA_HEADS, n, d_kv), BF16),
            jax.ShapeDtypeStruct((MLA_HEADS, n, MLA_ROPE_DIM), BF16),
        ],
        compiler_params=_cparams(("parallel",)),
        name=name,
    )(hn, w_in, g_q, w_uq, w_ukt, g_kv, cos, sin)


def _scores(qc, qr, kc, kr):
    nt = (((1,), (1,)), ((), ()))
    return (lax.dot_general(qc, kc, nt, preferred_element_type=F32)
            + lax.dot_general(qr, kr, nt, preferred_element_type=F32))


def _softmax_step(s, kc, m_scr, l_scr, acc_scr):
    m_old = m_scr[...]
    m_new = jnp.maximum(m_old, jnp.max(s, axis=-1, keepdims=True))
    alpha = jnp.exp(m_old - m_new)
    p = jnp.exp(s - m_new)
    l_scr[...] = alpha * l_scr[...] + jnp.sum(p, axis=-1, keepdims=True)
    acc_scr[...] = alpha * acc_scr[...] + jnp.dot(p.astype(BF16), kc, preferred_element_type=F32)
    m_scr[...] = m_new


def _attn_finish(rows_per_head, wuv_ref, g_ref, l_scr, acc_scr):
    o = (acc_scr[...] / l_scr[...]).astype(BF16)
    outs = []
    for hd in range(MLA_HEADS):
        oh = o[hd * rows_per_head:(hd + 1) * rows_per_head]
        outs.append(jnp.dot(oh, wuv_ref[hd], preferred_element_type=F32))
    return _rms(jnp.concatenate(outs, axis=-1), g_ref[...])


def _attn_prompt_body(bq, bk, qc_ref, qr_ref, kc_ref, kr_ref, kmc_ref, kmr_ref, wuv_ref, g_ref,
                      o_ref, m_scr, l_scr, acc_scr):
    i = pl.program_id(1)
    rows = MLA_HEADS * bq
    qc = qc_ref[...].reshape(rows, qc_ref.shape[-1])
    qr = qr_ref[...].reshape(rows, qr_ref.shape[-1])

    kmc = kmc_ref[...]
    s = _scores(qc, qr, kmc, kmr_ref[...])
    col = lax.broadcasted_iota(jnp.int32, s.shape, 1)
    s = jnp.where(col < N_META, s, -jnp.inf)
    m0 = jnp.max(s, axis=-1, keepdims=True)
    p = jnp.exp(s - m0)
    m_scr[...] = m0
    l_scr[...] = jnp.sum(p, axis=-1, keepdims=True)
    acc_scr[...] = jnp.dot(p.astype(BF16), kmc, preferred_element_type=F32)

    n_full = (i * bq) // bk

    def full_block(j, carry):
        start = pl.multiple_of(j * bk, bk)
        kc = kc_ref[0, pl.ds(start, bk), :]
        kr = kr_ref[0, pl.ds(start, bk), :]
        _softmax_step(_scores(qc, qr, kc, kr), kc, m_scr, l_scr, acc_scr)
        return carry

    lax.fori_loop(0, n_full, full_block, 0)

    start = pl.multiple_of(n_full * bk, bk)
    kc = kc_ref[0, pl.ds(start, bk), :]
    kr = kr_ref[0, pl.ds(start, bk), :]
    s = _scores(qc, qr, kc, kr)
    q_pos = i * bq + (lax.broadcasted_iota(jnp.int32, s.shape, 0) & (bq - 1))
    k_pos = start + lax.broadcasted_iota(jnp.int32, s.shape, 1)
    s = jnp.where(k_pos <= q_pos, s, -jnp.inf)
    _softmax_step(s, kc, m_scr, l_scr, acc_scr)

    o_ref[...] = _attn_finish(bq, wuv_ref, g_ref, l_scr, acc_scr).astype(o_ref.dtype)


def _attn_prompt(qc, qr, kc, kr, kmc, kmr, w_uv, g_mla):
    nb, seq, d_kv = kc.shape
    bq, bk = ATTN_BQ, ATTN_BK
    assert seq % bk == 0 and bk % bq == 0 and bq & (bq - 1) == 0 and N_META >= 1
    nq = seq // bq
    rows = MLA_HEADS * bq
    d_out = w_uv.shape[0] * w_uv.shape[2]
    qmap = lambda b, i: (0, b * nq + i, 0)
    kmap = lambda b, i: (b, 0, 0)
    c2 = lambda b, i: (0, 0)
    c3 = lambda b, i: (0, 0, 0)
    return pl.pallas_call(
        functools.partial(_attn_prompt_body, bq, bk),
        grid=(nb, nq),
        in_specs=[
            pl.BlockSpec((MLA_HEADS, bq, d_kv), qmap),
            pl.BlockSpec((MLA_HEADS, bq, MLA_ROPE_DIM), qmap),
            pl.BlockSpec((1, seq, d_kv), kmap),
            pl.BlockSpec((1, seq, MLA_ROPE_DIM), kmap),
            pl.BlockSpec(kmc.shape, c2),
            pl.BlockSpec(kmr.shape, c2),
            pl.BlockSpec(w_uv.shape, c3),
            pl.BlockSpec(g_mla.shape, c2),
        ],
        out_specs=pl.BlockSpec((bq, d_out), lambda b, i: (b * nq + i, 0)),
        out_shape=jax.ShapeDtypeStruct((nb * seq, d_out), BF16),
        scratch_shapes=[pltpu.VMEM((rows, 1), F32), pltpu.VMEM((rows, 1), F32),
                        pltpu.VMEM((rows, d_kv), F32)],
        compiler_params=_cparams(("parallel", "arbitrary")),
        name="attn_prompt",
    )(qc, qr, kc, kr, kmc, kmr, w_uv, g_mla)


def _attn_sample_body(npg, page, ds, pt_ref, qc_ref, qr_ref, cn_ref, rn_ref, wuv_ref, g_ref, *refs):
    pc_refs = refs[:npg]
    pr_refs = refs[npg:2 * npg]
    o_ref, kc_scr, kr_scr, m_scr, l_scr, acc_scr = refs[2 * npg:]
    j = pl.program_id(1)
    qc = qc_ref[0]
    qr = qr_ref[0]

    @pl.when(j == 0)
    def _():
        m_scr[...] = jnp.full_like(m_scr, -jnp.inf)
        l_scr[...] = jnp.zeros_like(l_scr)
        acc_scr[...] = jnp.zeros_like(acc_scr)

    for pg in range(npg):
        kc_scr[pg * page:(pg + 1) * page, :] = pc_refs[pg][0].astype(BF16)
        kr_scr[pg * page:(pg + 1) * page, :] = pr_refs[pg][0].astype(BF16)
    kc = kc_scr[...]
    _softmax_step(_scores(qc, qr, kc, kr_scr[...]), kc, m_scr, l_scr, acc_scr)

    @pl.when(j == pl.num_programs(1) - 1)
    def _():
        pad = LANES - ds
        kn = jnp.concatenate([cn_ref[0], jnp.zeros((pad, cn_ref.shape[-1]), F32)], axis=0).astype(BF16)
        rn = jnp.concatenate([rn_ref[0], jnp.zeros((pad, rn_ref.shape[-1]), F32)], axis=0).astype(BF16)
        s = _scores(qc, qr, kn, rn)
        t_q = lax.broadcasted_iota(jnp.int32, s.shape, 0) & (ds - 1)
        t_k = lax.broadcasted_iota(jnp.int32, s.shape, 1)
        s = jnp.where(t_k <= t_q, s, -jnp.inf)
        _softmax_step(s, kn, m_scr, l_scr, acc_scr)
        o_ref[0] = _attn_finish(ds, wuv_ref, g_ref, l_scr, acc_scr)


def _attn_sample(page_table, qc, qr, c_new, r_new, cache_c, cache_r, w_uv, g_mla):
    db, rows, d_kv = qc.shape
    ds = c_new.shape[1]
    n_pages = page_table.shape[1]
    page = cache_c.shape[1]
    npg = math.gcd(PAGES_PER_STEP, n_pages)
    assert ds & (ds - 1) == 0 and ds <= LANES
    d_out = w_uv.shape[0] * w_uv.shape[2]
    bmap = lambda b, j, pt: (b, 0, 0)
    c2 = lambda b, j, pt: (0, 0)
    c3 = lambda b, j, pt: (0, 0, 0)

    def page_map(pg):
        return lambda b, j, pt: (pt[b, j * npg + pg], 0, 0)

    in_specs = [
        pl.BlockSpec((1, rows, d_kv), bmap),
        pl.BlockSpec((1, rows, MLA_ROPE_DIM), bmap),
        pl.BlockSpec((1, ds, d_kv), bmap),
        pl.BlockSpec((1, ds, MLA_ROPE_DIM), bmap),
        pl.BlockSpec(w_uv.shape, c3),
        pl.BlockSpec(g_mla.shape, c2),
    ]
    in_specs += [pl.BlockSpec((1, page, d_kv), page_map(pg)) for pg in range(npg)]
    in_specs += [pl.BlockSpec((1, page, MLA_ROPE_DIM), page_map(pg)) for pg in range(npg)]
    grid_spec = pltpu.PrefetchScalarGridSpec(
        num_scalar_prefetch=1,
        grid=(db, n_pages // npg),
        in_specs=in_specs,
        out_specs=pl.BlockSpec((1, ds, d_out), bmap),
        scratch_shapes=[pltpu.VMEM((npg * page, d_kv), BF16), pltpu.VMEM((npg * page, MLA_ROPE_DIM), BF16),
                        pltpu.VMEM((rows, 1), F32), pltpu.VMEM((rows, 1), F32), pltpu.VMEM((rows, d_kv), F32)],
    )
    return pl.pallas_call(
        functools.partial(_attn_sample_body, npg, page, ds),
        grid_spec=grid_spec,
        out_shape=jax.ShapeDtypeStruct((db, ds, d_out), F32),
        compiler_params=_cparams(("parallel", "arbitrary")),
        name="attn_sample",
    )(page_table, qc, qr, c_new, r_new, w_uv, g_mla, *([cache_c] * npg), *([cache_r] * npg))


def _cmul_add(cur, sh, a_r, a_i, half):
    return cur + a_r * sh + a_i * pltpu.roll(sh, half, axis=1)


def _ssm_prompt_body(nb, n_levels, pre, u_ref, wy_ref, wd_ref, wc_ref, dv_ref, ar_ref, ai_ref,
                     y_ref, hl_ref, scr):
    u = u_ref[0]
    ub = u.astype(BF16)
    rp, half = u.shape[0], wd_ref.shape[-1] // 2
    d = jnp.dot(ub, wd_ref[0], preferred_element_type=F32)
    scr[0:pre, :] = jnp.zeros((pre, scr.shape[1]), F32)
    scr[pre:pre + rp, :] = d
    scr[pre:pre + rp, :] = scr[pre - nb:pre - nb + rp, :]
    for k in range(n_levels):
        s = nb << k
        cur = scr[pre:pre + rp, :]
        sh = scr[pre - s:pre - s + rp, :]
        scr[pre:pre + rp, :] = _cmul_add(cur, sh, ar_ref[0, k:k + 1, :], ai_ref[0, k:k + 1, :], half)
    e = scr[pre:pre + rp, :]
    y_ref[0] = (jnp.dot(ub, wy_ref[0], preferred_element_type=F32)
                + jnp.dot(e.astype(BF16), wc_ref[0], preferred_element_type=F32)
                + u * dv_ref[0])
    hl_ref[0] = _cmul_add(d, e, ar_ref[0, 0:1, :], ai_ref[0, 0:1, :], half)


def _ssm_sample_body(u_ref, h_ref, wy_ref, wd_ref, wc_ref, dv_ref, ar_ref, ai_ref, y_ref, hl_ref):
    u = u_ref[0]
    ub = u.astype(BF16)
    e = h_ref[0]
    half = e.shape[-1] // 2
    d = jnp.dot(ub, wd_ref[0], preferred_element_type=F32)
    y_ref[0] = (jnp.dot(ub, wy_ref[0], preferred_element_type=F32)
                + jnp.dot(e.astype(BF16), wc_ref[0], preferred_element_type=F32)
                + u * dv_ref[0])
    hl_ref[0] = _cmul_add(d, e, ar_ref[0, 0:1, :], ai_ref[0, 0:1, :], half)


def _ssm_weights(a_re, a_im, log_dt, b_re, b_im, c_re, c_im, d_skip, t_chunk, n_levels):
    hi = lax.Precision.HIGHEST
    a = lax.complex(a_re.astype(F32), a_im.astype(F32))
    dt = jnp.exp(log_dt.astype(F32))[:, None]
    a_dt = a * dt
    a_bar = jnp.exp(a_dt)
    b_bar = ((a_bar - 1.0) / a)[..., None] * lax.complex(b_re.astype(F32), b_im.astype(F32))
    c = lax.complex(c_re.astype(F32), c_im.astype(F32))
    g, p_state, ch = b_bar.shape
    k = jnp.arange(t_chunk + 1, dtype=F32)
    a_pow = jnp.exp(a_dt[None] * k[:, None, None])
    kern = jnp.einsum('gcp,kgp,gpd->kgcd', c, a_pow[:t_chunk], b_bar, precision=hi).real
    s_idx = jnp.arange(t_chunk)[:, None]
    t_idx = jnp.arange(t_chunk)[None, :]
    lag = t_idx - s_idx
    wy = jnp.where((lag >= 0)[:, :, None, None, None], kern[jnp.clip(lag, 0, t_chunk - 1)], 0.0)
    wy = wy.transpose(2, 0, 4, 1, 3).reshape(g, t_chunk * ch, t_chunk * ch)
    wd = a_pow[:t_chunk][::-1][:, :, :, None] * b_bar[None]
    wd = wd.transpose(1, 0, 3, 2).reshape(g, t_chunk * ch, p_state)
    wd = jnp.concatenate([wd.real, wd.imag], axis=-1)
    gm = c[None] * a_pow[1:][:, :, None, :]
    gm = gm.transpose(1, 3, 0, 2).reshape(g, p_state, t_chunk * ch)
    wc = jnp.concatenate([gm.real, -gm.imag], axis=1)
    dv = jnp.tile(d_skip.astype(F32).reshape(g, 1, ch), (1, 1, t_chunk))
    lev = (t_chunk * (2.0 ** jnp.arange(n_levels, dtype=F32)))
    a_lev = jnp.exp(a_dt[:, None, :] * lev[None, :, None])
    a_r = jnp.concatenate([a_lev.real, a_lev.real], axis=-1)
    a_i = jnp.concatenate([-a_lev.imag, a_lev.imag], axis=-1)
    return wy.astype(BF16), wd.astype(BF16), wc.astype(BF16), dv, a_r, a_i


def _ssm_prompt(u_rows, nb, n_chunks, ops):
    wy, wd, wc, dv, a_r, a_i = ops
    g, rp, tc = u_rows.shape
    n_levels = a_r.shape[1]
    st = wd.shape[-1]
    pre = -(-(nb << (n_levels - 1)) // SUBLANES) * SUBLANES
    gmap = lambda i: (i, 0, 0)
    return pl.pallas_call(
        functools.partial(_ssm_prompt_body, nb, n_levels, pre),
        grid=(g,),
        in_specs=[pl.BlockSpec((1, rp, tc), gmap), pl.BlockSpec((1,) + wy.shape[1:], gmap),
                  pl.BlockSpec((1,) + wd.shape[1:], gmap), pl.BlockSpec((1,) + wc.shape[1:], gmap),
                  pl.BlockSpec((1,) + dv.shape[1:], gmap), pl.BlockSpec((1,) + a_r.shape[1:], gmap),
                  pl.BlockSpec((1,) + a_i.shape[1:], gmap)],
        out_specs=[pl.BlockSpec((1, rp, tc), gmap), pl.BlockSpec((1, rp, st), gmap)],
        out_shape=[jax.ShapeDtypeStruct((g, rp, tc), F32), jax.ShapeDtypeStruct((g, rp, st), F32)],
        scratch_shapes=[pltpu.VMEM((pre + rp, st), F32)],
        compiler_params=_cparams(("parallel",)),
        name="ssm_prompt",
    )(u_rows, wy, wd, wc, dv, a_r, a_i)


def _ssm_sample(u_rows, h_rows, ops):
    wy, wd, wc, dv, a_r, a_i = ops
    g, r, tc = u_rows.shape
    st = wd.shape[-1]
    gmap = lambda i: (i, 0, 0)
    return pl.pallas_call(
        _ssm_sample_body,
        grid=(g,),
        in_specs=[pl.BlockSpec((1, r, tc), gmap), pl.BlockSpec((1, r, st), gmap),
                  pl.BlockSpec((1,) + wy.shape[1:], gmap), pl.BlockSpec((1,) + wd.shape[1:], gmap),
                  pl.BlockSpec((1,) + wc.shape[1:], gmap), pl.BlockSpec((1,) + dv.shape[1:], gmap),
                  pl.BlockSpec((1,) + a_r.shape[1:], gmap), pl.BlockSpec((1,) + a_i.shape[1:], gmap)],
        out_specs=[pl.BlockSpec((1, r, tc), gmap), pl.BlockSpec((1, r, st), gmap)],
        out_shape=[jax.ShapeDtypeStruct((g, r, tc), F32), jax.ShapeDtypeStruct((g, r, st), F32)],
        compiler_params=_cparams(("parallel",)),
        name="ssm_sample",
    )(u_rows, h_rows, wy, wd, wc, dv, a_r, a_i)


def _mix_out_body(ys_ref, ya_ref, x_ref, wglu_ref, bglu_ref, gs_ref, wo_ref, gpost_ref, o_ref):
    z = jax.nn.gelu(ys_ref[...])
    gate = jax.nn.sigmoid(jnp.dot(z.astype(BF16), wglu_ref[...], preferred_element_type=F32) + bglu_ref[...])
    ns = _rms(z * gate, gs_ref[...]).astype(BF16)
    w = ns.shape[-1]
    y = (jnp.dot(ns, wo_ref[:w, :], preferred_element_type=F32)
         + jnp.dot(ya_ref[...].astype(BF16), wo_ref[w:, :], preferred_element_type=F32))
    o_ref[...] = x_ref[...] + _rms(y, gpost_ref[...])


def _mix_out(ys, ya, x, w_glu, b_glu, g_ssm, w_o, g_post, name):
    n, d = x.shape
    w = ys.shape[1]
    tm = _row_tile(n)
    row = lambda i: (i, 0)
    c2 = lambda i: (0, 0)
    return pl.pallas_call(
        _mix_out_body,
        grid=(n // tm,),
        in_specs=[pl.BlockSpec((tm, w), row), pl.BlockSpec((tm, ya.shape[1]), row), pl.BlockSpec((tm, d), row),
                  pl.BlockSpec(w_glu.shape, c2), pl.BlockSpec(b_glu.shape, c2), pl.BlockSpec(g_ssm.shape, c2),
                  pl.BlockSpec(w_o.shape, c2), pl.BlockSpec(g_post.shape, c2)],
        out_specs=pl.BlockSpec((tm, d), row),
        out_shape=jax.ShapeDtypeStruct((n, d), F32),
        compiler_params=_cparams(("parallel",)),
        name=name,
    )(ys, ya, x, w_glu, b_glu, g_ssm, w_o, g_post)


def _rope_tables(pos):
    inv = ROPE_THETA ** (-jnp.arange(ROPE_HALF, dtype=F32) / ROPE_HALF)
    ang = pos.astype(F32)[:, None] * inv[None, :]
    reps = LANES // ROPE_HALF
    return jnp.tile(jnp.cos(ang), (1, reps)), jnp.tile(jnp.sin(ang), (1, reps))


def _uq_column_order():
    per = MLA_NOPE_DIM + MLA_ROPE_DIM
    heads = np.arange(MLA_HEADS)[:, None]
    nope = (heads * per + np.arange(MLA_NOPE_DIM)[None, :]).reshape(-1)
    rope1 = (heads * per + MLA_NOPE_DIM + np.arange(ROPE_HALF)[None, :]).reshape(-1)
    rope2 = (heads * per + MLA_NOPE_DIM + ROPE_HALF + np.arange(ROPE_HALF)[None, :]).reshape(-1)
    return np.concatenate([nope, rope1, rope2])


def kernel(x_prompt, x_sample, cache_kv_latent, cache_k_rope, state_ssm_re, state_ssm_im, page_table, meta_tokens, g_ff1_pre, w_ff1_gate, w_ff1_up, w_ff1_down, g_ff1_post, g_mix_pre, w_in, ssm_a_re, ssm_a_im, ssm_log_dt, ssm_b_re, ssm_b_im, ssm_c_re, ssm_c_im, ssm_d, w_glu, b_glu, g_q_norm, w_uq, g_kv_norm, w_uk, w_uv, g_ssm_out, g_mla_out, w_o, g_mix_post, g_ff2_pre, w_ff2_gate, w_ff2_up, w_ff2_down, g_ff2_post):
    depth = w_in.shape[0]
    assert depth == 1, "single-layer step"
    bp, seq, d_model = x_prompt.shape
    db, ds, _ = x_sample.shape
    n_pages = page_table.shape[1]
    page = cache_kv_latent.shape[2]
    past_len = n_pages * page
    d_kv = cache_kv_latent.shape[3]
    d_q = w_uq.shape[1]
    n_groups, n_state = ssm_a_re.shape[1], ssm_a_re.shape[2]
    d_u = n_groups * SSM_GROUP
    dims = (d_u, d_q, d_kv)
    l = 0
    row = lambda v: v[l].reshape(1, -1).astype(F32)

    wg1, wu1, wd1 = (w[l].astype(BF16) for w in (w_ff1_gate, w_ff1_up, w_ff1_down))
    wg2, wu2, wd2 = (w[l].astype(BF16) for w in (w_ff2_gate, w_ff2_up, w_ff2_down))
    w_in_b = w_in[l].astype(BF16)
    w_uq_b = w_uq[l][:, _uq_column_order()].astype(BF16)
    w_ukt = jnp.transpose(w_uk[l], (1, 2, 0)).astype(BF16)
    w_uv_b = jnp.transpose(w_uv[l], (1, 0, 2)).astype(BF16)
    w_glu_b = w_glu[l].astype(BF16)
    w_o_b = w_o[l].astype(BF16)

    xs = [x_prompt.reshape(bp * seq, d_model), x_sample.reshape(db * ds, d_model), meta_tokens.astype(F32)]
    names = ["prompt", "sample", "meta"]
    pos = [N_META + jnp.tile(jnp.arange(seq), bp), past_len + jnp.tile(jnp.arange(ds), db), jnp.arange(N_META)]

    x1, pr = [], []
    for x, nm, ps in zip(xs, names, pos):
        y, hn = _ffn(x, row(g_ff1_pre), wg1, wu1, wd1, row(g_ff1_post), row(g_mix_pre), True, "ffn1_" + nm)
        cos, sin = _rope_tables(ps)
        x1.append(y)
        pr.append(_proj(hn, w_in_b, row(g_q_norm), w_uq_b, w_ukt, row(g_kv_norm), cos, sin, dims, "proj_" + nm))
    (u_p, ckv_p, ckvb_p, kr_p, krb_p, qc_p, qr_p) = pr[0]
    (u_s, ckv_s, _, kr_s, _, qc_s, qr_s) = pr[1]
    (u_m, ckv_m, ckvb_m, kr_m, krb_m, _, _) = pr[2]

    t_p = SSM_CHUNK
    assert N_META == t_p and seq % t_p == 0
    n_chunks = seq // t_p + 1
    n_levels = max(1, (n_chunks - 1).bit_length())
    ssm_w = (ssm_a_re[l], ssm_a_im[l], ssm_log_dt[l], ssm_b_re[l], ssm_b_im[l], ssm_c_re[l], ssm_c_im[l], ssm_d[l])
    ops_p = _ssm_weights(*ssm_w, t_p, n_levels)
    ops_s = _ssm_weights(*ssm_w, ds, 1)
    tc = t_p * SSM_GROUP
    up = u_p.reshape(bp, seq // t_p, t_p, n_groups, SSM_GROUP).transpose(3, 1, 0, 2, 4).reshape(n_groups, seq // t_p, bp, tc)
    um = u_m.reshape(1, t_p, n_groups, SSM_GROUP).transpose(2, 0, 1, 3).reshape(n_groups, 1, 1, tc)
    u_rows = jnp.concatenate([jnp.broadcast_to(um, (n_groups, 1, bp, tc)), up], axis=1).reshape(n_groups, n_chunks * bp, tc)
    r_p = n_chunks * bp
    rp = -(-r_p // SUBLANES) * SUBLANES
    u_rows = jnp.pad(u_rows, ((0, 0), (0, rp - r_p), (0, 0)))
    y_rows, hl_rows = _ssm_prompt(u_rows, bp, n_chunks, ops_p)
    ys_p = (y_rows[:, bp:r_p].reshape(n_groups, seq // t_p, bp, t_p, SSM_GROUP)
            .transpose(2, 1, 3, 0, 4).reshape(bp * seq, d_u))
    hl_p = hl_rows[:, r_p - bp:r_p].transpose(1, 0, 2)

    us_rows = u_s.reshape(db, ds, n_groups, SSM_GROUP).transpose(2, 0, 1, 3).reshape(n_groups, db, ds * SSM_GROUP)
    h0_rows = jnp.concatenate([state_ssm_re[l], state_ssm_im[l]], axis=-1).astype(F32).transpose(1, 0, 2)
    ysr, hl_s = _ssm_sample(us_rows, h0_rows, ops_s)
    ys_s = ysr.reshape(n_groups, db, ds, SSM_GROUP).transpose(1, 2, 0, 3).reshape(db * ds, d_u)
    hl_s = hl_s.transpose(1, 0, 2)

    g_mla = row(g_mla_out)
    pad_m = LANES - N_META
    kmc = jnp.pad(ckvb_m, ((0, pad_m), (0, 0)))
    kmr = jnp.pad(krb_m, ((0, pad_m), (0, 0)))
    ya_p = _attn_prompt(qc_p, qr_p, ckvb_p.reshape(bp, seq, d_kv), krb_p.reshape(bp, seq, MLA_ROPE_DIM),
                        kmc, kmr, w_uv_b, g_mla)
    to_seq = lambda q: q.reshape(MLA_HEADS, db, ds, q.shape[-1]).transpose(1, 0, 2, 3).reshape(db, MLA_HEADS * ds, q.shape[-1])
    ya_s = _attn_sample(page_table, to_seq(qc_s), to_seq(qr_s), ckv_s.reshape(db, ds, d_kv),
                        kr_s.reshape(db, ds, MLA_ROPE_DIM), cache_kv_latent[l], cache_k_rope[l], w_uv_b, g_mla)
    ya_s = ya_s.reshape(db * ds, -1)

    outs = []
    for x, ys, ya, nm in ((x1[0], ys_p, ya_p, "prompt"), (x1[1], ys_s, ya_s, "sample")):
        x2 = _mix_out(ys, ya, x, w_glu_b, row(b_glu), row(g_ssm_out), w_o_b, row(g_mix_post), "mix_out_" + nm)
        y, _ = _ffn(x2, row(g_ff2_pre), wg2, wu2, wd2, row(g_ff2_post), row(g_ff2_post), False, "ffn2_" + nm)
        outs.append(y)

    y_prompt = outs[0].reshape(bp, seq, d_model)
    y_sample = outs[1].reshape(db, ds, d_model)
    meta_b = lambda v: jnp.broadcast_to(v[None], (bp,) + v.shape)
    new_ckv_p = jnp.concatenate([meta_b(ckv_m), ckv_p.reshape(bp, seq, d_kv)], axis=1)[None]
    new_kr_p = jnp.concatenate([meta_b(kr_m), kr_p.reshape(bp, seq, MLA_ROPE_DIM)], axis=1)[None]
    return (y_prompt, y_sample, new_ckv_p, new_kr_p,
            hl_p[None, :, :, :n_state], hl_p[None, :, :, n_state:],
            ckv_s.reshape(1, db, ds, d_kv), kr_s.reshape(1, db, ds, MLA_ROPE_DIM),
            hl_s[None, :, :, :n_state], hl_s[None, :, :, n_state:])
```

```python
import functools
import math

import numpy as np
import jax
import jax.numpy as jnp
from jax import lax
from jax.experimental import pallas as pl
from jax.experimental.pallas import tpu as pltpu

F32 = jnp.float32
BF16 = jnp.bfloat16

N_META = 16
SSM_GROUP = 16
MLA_HEADS = 8
MLA_NOPE_DIM = 64
MLA_ROPE_DIM = 32
ROPE_HALF = MLA_ROPE_DIM // 2
ROPE_THETA = 10000.0
RMS_EPS = 1e-6
ATTN_SCALE = (MLA_NOPE_DIM + MLA_ROPE_DIM) ** -0.5

LANES = 128
SUBLANES = 8
VMEM_LIMIT = 56 * 1024 * 1024

ROW_TILE = 512
FFN_ROW_TILE = 1024
FF_TILE = 256
ATTN_BQ = 128
ATTN_BK = 512
ATTN_COLS = 512
PAGES_PER_STEP = 16
SAMPLE_PARTS = 4
SSM_CHUNK = 16


def _rms(x, g):
    return (x * lax.rsqrt(jnp.mean(x * x, axis=-1, keepdims=True) + RMS_EPS)) * g


def _row_tile(n):
    return ROW_TILE if n % ROW_TILE == 0 else n


def _cparams(sem):
    return pltpu.CompilerParams(dimension_semantics=sem, vmem_limit_bytes=VMEM_LIMIT)


def _ffn_body(emit_norm, x_ref, gpre_ref, wg_ref, wu_ref, wd_ref, gpost_ref, gnext_ref, *refs):
    if emit_norm:
        y_ref, hn_ref, h_scr, acc_scr = refs
    else:
        y_ref, h_scr, acc_scr = refs
        hn_ref = None
    j = pl.program_id(1)

    @pl.when(j == 0)
    def _():
        h_scr[...] = _rms(x_ref[...], gpre_ref[...]).astype(BF16)
        acc_scr[...] = jnp.zeros_like(acc_scr)

    h = h_scr[...]
    g = jnp.dot(h, wg_ref[...], preferred_element_type=F32)
    u = jnp.dot(h, wu_ref[...], preferred_element_type=F32)
    a = (g * jax.nn.sigmoid(g)) * u
    acc_scr[...] += jnp.dot(a.astype(BF16), wd_ref[...], preferred_element_type=F32)

    @pl.when(j == pl.num_programs(1) - 1)
    def _():
        y = x_ref[...] + 0.5 * _rms(acc_scr[...], gpost_ref[...])
        y_ref[...] = y
        if emit_norm:
            hn_ref[...] = _rms(y, gnext_ref[...]).astype(BF16)


def _ffn_weights(wg, wu, wd):
    d, d_ff = wg.shape
    tf = FF_TILE
    assert d_ff % tf == 0
    chunk = lambda w: w.astype(BF16).reshape(d, d_ff // tf, tf).transpose(1, 0, 2)
    return chunk(wg), chunk(wu), wd.astype(BF16).reshape(d_ff // tf, tf, d)


def _ffn(x, g_pre, weights, g_post, g_next, emit_norm, name):
    wg, wu, wd = weights
    n, d = x.shape
    nj, _, tf = wg.shape
    tm = FFN_ROW_TILE if n % FFN_ROW_TILE == 0 else n
    row = lambda i, j: (i, 0)
    vec = lambda i, j: (0, 0)
    chunk = lambda i, j: (j, 0, 0)
    out_shape = [jax.ShapeDtypeStruct((n, d), F32)]
    out_specs = [pl.BlockSpec((tm, d), row)]
    if emit_norm:
        out_shape.append(jax.ShapeDtypeStruct((n, d), BF16))
        out_specs.append(pl.BlockSpec((tm, d), row))
    res = pl.pallas_call(
        functools.partial(_ffn_body, emit_norm),
        grid=(n // tm, nj),
        in_specs=[
            pl.BlockSpec((tm, d), row),
            pl.BlockSpec((1, d), vec),
            pl.BlockSpec((None, d, tf), chunk),
            pl.BlockSpec((None, d, tf), chunk),
            pl.BlockSpec((None, tf, d), chunk),
            pl.BlockSpec((1, d), vec),
            pl.BlockSpec((1, d), vec),
        ],
        out_specs=out_specs,
        out_shape=out_shape,
        scratch_shapes=[pltpu.VMEM((tm, d), BF16), pltpu.VMEM((tm, d), F32)],
        compiler_params=_cparams(("parallel", "arbitrary")),
        name=name,
    )(x, g_pre, wg, wu, wd, g_post, g_next)
    return res if emit_norm else (res[0], None)


def _proj_body(d_u, d_q, d_kv, h_ref, win_ref, gq_ref, wuq_ref, wukt_ref, gkv_ref, cos_ref, sin_ref,
               u_ref, ckv_ref, ckvb_ref, ckvt_ref, kr_ref, krb_ref, qc_ref, qr_ref):
    proj = jnp.dot(h_ref[...], win_ref[...], preferred_element_type=F32)
    off_q, off_kv, off_kr = d_u, d_u + d_q, d_u + d_q + d_kv
    u_ref[...] = proj[:, :off_q]
    cq = _rms(proj[:, off_q:off_kv], gq_ref[...]).astype(BF16)
    q = jnp.dot(cq, wuq_ref[...], preferred_element_type=F32) * ATTN_SCALE
    n_nope = MLA_HEADS * MLA_NOPE_DIM
    for hd in range(MLA_HEADS):
        qn = q[:, hd * MLA_NOPE_DIM:(hd + 1) * MLA_NOPE_DIM].astype(BF16)
        qc_ref[hd] = jnp.dot(qn, wukt_ref[hd], preferred_element_type=F32).astype(BF16)
    cos = cos_ref[...]
    sin = sin_ref[...]
    r1 = q[:, n_nope:n_nope + LANES]
    r2 = q[:, n_nope + LANES:n_nope + 2 * LANES]
    o1 = r1 * cos - r2 * sin
    o2 = r2 * cos + r1 * sin
    for hd in range(MLA_HEADS):
        sl = slice(hd * ROPE_HALF, (hd + 1) * ROPE_HALF)
        qr_ref[hd] = jnp.concatenate([o1[:, sl], o2[:, sl]], axis=-1).astype(BF16)
    ckv = _rms(proj[:, off_kv:off_kr], gkv_ref[...])
    ckv_ref[...] = ckv
    ckvb_ref[...] = ckv.astype(BF16)
    ckvt_ref[...] = jnp.transpose(ckv).astype(BF16)
    x1 = proj[:, off_kr:off_kr + ROPE_HALF]
    x2 = proj[:, off_kr + ROPE_HALF:off_kr + MLA_ROPE_DIM]
    c16 = cos[:, :ROPE_HALF]
    s16 = sin[:, :ROPE_HALF]
    kr = jnp.concatenate([x1 * c16 - x2 * s16, x2 * c16 + x1 * s16], axis=-1)
    kr_ref[...] = kr
    krb_ref[...] = kr.astype(BF16)


def _proj(hn, w_in, g_q, w_uq, w_ukt, g_kv, cos, sin, dims, name):
    n, d = hn.shape
    d_u, d_q, d_kv = dims
    tm = _row_tile(n)
    row = lambda i: (i, 0)
    full2 = lambda i: (0, 0)
    full3 = lambda i: (0, 0, 0)
    hrow = lambda i: (0, i, 0)
    return pl.pallas_call(
        functools.partial(_proj_body, d_u, d_q, d_kv),
        grid=(n // tm,),
        in_specs=[
            pl.BlockSpec((tm, d), row),
            pl.BlockSpec(w_in.shape, full2),
            pl.BlockSpec(g_q.shape, full2),
            pl.BlockSpec(w_uq.shape, full2),
            pl.BlockSpec(w_ukt.shape, full3),
            pl.BlockSpec(g_kv.shape, full2),
            pl.BlockSpec((tm, LANES), row),
            pl.BlockSpec((tm, LANES), row),
        ],
        out_specs=[
            pl.BlockSpec((tm, d_u), row),
            pl.BlockSpec((tm, d_kv), row),
            pl.BlockSpec((tm, d_kv), row),
            pl.BlockSpec((d_kv, tm), lambda i: (0, i)),
            pl.BlockSpec((tm, MLA_ROPE_DIM), row),
            pl.BlockSpec((tm, MLA_ROPE_DIM), row),
            pl.BlockSpec((MLA_HEADS, tm, d_kv), hrow),
            pl.BlockSpec((MLA_HEADS, tm, MLA_ROPE_DIM), hrow),
        ],
        out_shape=[
            jax.ShapeDtypeStruct((n, d_u), F32),
            jax.ShapeDtypeStruct((n, d_kv), F32),
            jax.ShapeDtypeStruct((n, d_kv), BF16),
            jax.ShapeDtypeStruct((d_kv, n), BF16),
            jax.ShapeDtypeStruct((n, MLA_ROPE_DIM), F32),
            jax.ShapeDtypeStruct((n, MLA_ROPE_DIM), BF16),
            jax.ShapeDtypeStruct((MLA_HEADS, n, d_kv), BF16),
            jax.ShapeDtypeStruct((MLA_HEADS, n, MLA_ROPE_DIM), BF16),
        ],
        compiler_params=_cparams(("parallel",)),
        name=name,
    )(hn, w_in, g_q, w_uq, w_ukt, g_kv, cos, sin)


def _scores(qc, qr, kc, kr):
    nt = (((1,), (1,)), ((), ()))
    return (lax.dot_general(qc, kc, nt, preferred_element_type=F32)
            + lax.dot_general(qr, kr, nt, preferred_element_type=F32))


def _softmax_step(s, kc, m_scr, l_scr, acc_scr):
    m_old = m_scr[...]
    m_new = jnp.maximum(m_old, jnp.max(s, axis=-1, keepdims=True))
    alpha = jnp.exp(m_old - m_new)
    p = jnp.exp(s - m_new)
    l_scr[...] = alpha * l_scr[...] + jnp.sum(p, axis=-1, keepdims=True)
    acc_scr[...] = alpha * acc_scr[...] + jnp.dot(p.astype(BF16), kc, preferred_element_type=F32)
    m_scr[...] = m_new


def _attn_finish(rows_per_head, wuv_ref, g_ref, l_scr, acc_scr):
    o = (acc_scr[...] / l_scr[...]).astype(BF16)
    outs = []
    for hd in range(MLA_HEADS):
        oh = o[hd * rows_per_head:(hd + 1) * rows_per_head]
        outs.append(jnp.dot(oh, wuv_ref[hd], preferred_element_type=F32))
    return _rms(jnp.concatenate(outs, axis=-1), g_ref[...])


def _attn_cols_step(s, m_blk, kct, cols, first, m_scr, l_scr, acc_scr):
    if first:
        m_new = m_blk
    else:
        m_old = m_scr[:, cols]
        m_new = jnp.maximum(m_old, m_blk)
        alpha = jnp.exp(m_old - m_new)
    p = jnp.exp(s - m_new)
    l_blk = jnp.sum(p, axis=0, keepdims=True)
    pv = jnp.dot(kct, p.astype(BF16), preferred_element_type=F32)
    if first:
        l_scr[:, cols] = l_blk
        acc_scr[:, cols] = pv
    else:
        l_scr[:, cols] = alpha * l_scr[:, cols] + l_blk
        acc_scr[:, cols] = alpha * acc_scr[:, cols] + pv
    m_scr[:, cols] = m_new


def _attn_prompt_body(bq, bk, ncol, qc_ref, qr_ref, kc_ref, kct_ref, kr_ref, kmc_ref, kmct_ref, kmr_ref,
                      wuv_ref, g_ref, o_ref, m_scr, l_scr, acc_scr, s_scr, mb_scr):
    i = pl.program_id(1)
    nt = (((1,), (1,)), ((), ()))
    heads_per = ncol // bq
    n_groups = MLA_HEADS // heads_per
    groups = [slice(gi * ncol, (gi + 1) * ncol) for gi in range(n_groups)]

    def scores(gi, kc, kr):
        hs = slice(gi * heads_per, (gi + 1) * heads_per)
        qc = qc_ref[hs].reshape(ncol, qc_ref.shape[-1])
        qr = qr_ref[hs].reshape(ncol, qr_ref.shape[-1])
        return (lax.dot_general(kc, qc, nt, preferred_element_type=F32)
                + lax.dot_general(kr, qr, nt, preferred_element_type=F32))

    def produce(j, slot):
        start = pl.multiple_of(j * bk, bk)
        kc = kc_ref[0, pl.ds(start, bk), :]
        kr = kr_ref[0, pl.ds(start, bk), :]
        for gi, cols in enumerate(groups):
            s = scores(gi, kc, kr)
            s_scr[slot, :, cols] = s
            mb_scr[slot, :, cols] = jnp.max(s, axis=0, keepdims=True)

    def consume(j, slot, diagonal):
        start = pl.multiple_of(j * bk, bk)
        kct = kct_ref[:, pl.ds(start, bk)]
        for cols in groups:
            s = s_scr[slot, :, cols]
            if diagonal:
                k_pos = start + lax.broadcasted_iota(jnp.int32, s.shape, 0)
                q_pos = i * bq + (lax.broadcasted_iota(jnp.int32, s.shape, 1) & (bq - 1))
                s = jnp.where(k_pos <= q_pos, s, -jnp.inf)
                m_blk = jnp.max(s, axis=0, keepdims=True)
            else:
                m_blk = mb_scr[slot, :, cols]
            _attn_cols_step(s, m_blk, kct, cols, False, m_scr, l_scr, acc_scr)

    n_full = (i * bq) // bk
    produce(0, 0)

    kmc, kmct, kmr = kmc_ref[...], kmct_ref[...], kmr_ref[...]
    for gi, cols in enumerate(groups):
        s = scores(gi, kmc, kmr)
        s = jnp.where(lax.broadcasted_iota(jnp.int32, s.shape, 0) < N_META, s, -jnp.inf)
        _attn_cols_step(s, jnp.max(s, axis=0, keepdims=True), kmct, cols, True, m_scr, l_scr, acc_scr)

    def pair(k, carry):
        j = 2 * k
        produce(j + 1, 1)
        consume(j, 0, False)
        produce(j + 2, 0)
        consume(j + 1, 1, False)
        return carry

    lax.fori_loop(0, n_full // 2, pair, 0)
    odd = (n_full & 1) == 1

    @pl.when(odd)
    def _():
        produce(n_full, 1)
        consume(n_full - 1, 0, False)
        consume(n_full, 1, True)

    @pl.when(jnp.logical_not(odd))
    def _():
        consume(n_full, 0, True)

    outs = []
    for hd in range(MLA_HEADS):
        cols = slice(hd * bq, (hd + 1) * bq)
        o_t = (acc_scr[:, cols] / l_scr[:, cols]).astype(BF16)
        outs.append(jnp.dot(wuv_ref[hd], o_t, preferred_element_type=F32))
    y = jnp.transpose(jnp.concatenate(outs, axis=0))
    o_ref[...] = _rms(y, g_ref[...]).astype(o_ref.dtype)


def _attn_prompt(qc, qr, kc, kct, kr, kmc, kmct, kmr, w_uvt, g_mla):
    nb, seq, d_kv = kc.shape
    bq, bk, ncol = ATTN_BQ, ATTN_BK, ATTN_COLS
    assert seq % bk == 0 and bk % bq == 0 and bq & (bq - 1) == 0 and N_META >= 1
    assert ncol % bq == 0 and (MLA_HEADS * bq) % ncol == 0
    nq = seq // bq
    rows = MLA_HEADS * bq
    d_out = w_uvt.shape[0] * w_uvt.shape[1]
    qmap = lambda b, i: (0, b * nq + i, 0)
    kmap = lambda b, i: (b, 0, 0)
    c2 = lambda b, i: (0, 0)
    c3 = lambda b, i: (0, 0, 0)
    return pl.pallas_call(
        functools.partial(_attn_prompt_body, bq, bk, ncol),
        grid=(nb, nq),
        in_specs=[
            pl.BlockSpec((MLA_HEADS, bq, d_kv), qmap),
            pl.BlockSpec((MLA_HEADS, bq, MLA_ROPE_DIM), qmap),
            pl.BlockSpec((1, seq, d_kv), kmap),
            pl.BlockSpec((d_kv, seq), lambda b, i: (0, b)),
            pl.BlockSpec((1, seq, MLA_ROPE_DIM), kmap),
            pl.BlockSpec(kmc.shape, c2),
            pl.BlockSpec(kmct.shape, c2),
            pl.BlockSpec(kmr.shape, c2),
            pl.BlockSpec(w_uvt.shape, c3),
            pl.BlockSpec(g_mla.shape, c2),
        ],
        out_specs=pl.BlockSpec((bq, d_out), lambda b, i: (b * nq + i, 0)),
        out_shape=jax.ShapeDtypeStruct((nb * seq, d_out), BF16),
        scratch_shapes=[pltpu.VMEM((1, rows), F32), pltpu.VMEM((1, rows), F32),
                        pltpu.VMEM((d_kv, rows), F32),
                        pltpu.VMEM((2, bk, rows), F32), pltpu.VMEM((2, 1, rows), F32)],
        compiler_params=_cparams(("parallel", "arbitrary")),
        name="attn_prompt",
    )(qc, qr, kc, kct, kr, kmc, kmct, kmr, w_uvt, g_mla)


def _attn_sample_body(npg, page, ds, n_part, pt_ref, qc_ref, qr_ref, cn_ref, rn_ref, wuv_ref, g_ref, *refs):
    pc_refs = refs[:npg]
    pr_refs = refs[npg:2 * npg]
    o_ref, kc_scr, krt_scr, s_scr, mb_scr, m_scr, l_scr, acc_scr = refs[2 * npg:]
    j = pl.program_id(1)
    qc = qc_ref[0]
    qr = qr_ref[0]
    nt = (((1,), (1,)), ((), ()))
    per = npg // n_part

    @pl.when(j == 0)
    def _():
        m_scr[...] = jnp.full_like(m_scr, -jnp.inf)
        l_scr[...] = jnp.zeros_like(l_scr)
        acc_scr[...] = jnp.zeros_like(acc_scr)

    def keys(part):
        return slice(part * per * page, (part + 1) * per * page)

    def produce(part):
        for pg in range(part * per, (part + 1) * per):
            kc_scr[pg * page:(pg + 1) * page, :] = pc_refs[pg][0].astype(BF16)
            krt_scr[:, pg * page:(pg + 1) * page] = pr_refs[pg][0].astype(BF16)
        s = (lax.dot_general(qc, kc_scr[keys(part), :], nt, preferred_element_type=F32)
             + jnp.dot(qr, krt_scr[:, keys(part)], preferred_element_type=F32))
        s_scr[part] = s
        mb_scr[part] = jnp.max(s, axis=-1, keepdims=True)

    def consume(part):
        s = s_scr[part]
        m_old = m_scr[...]
        m_new = jnp.maximum(m_old, mb_scr[part])
        alpha = jnp.exp(m_old - m_new)
        p = jnp.exp(s - m_new)
        l_scr[...] = alpha * l_scr[...] + jnp.sum(p, axis=-1, keepdims=True)
        acc_scr[...] = alpha * acc_scr[...] + jnp.dot(p.astype(BF16), kc_scr[keys(part), :],
                                                      preferred_element_type=F32)
        m_scr[...] = m_new

    produce(0)
    for part in range(n_part):
        if part + 1 < n_part:
            produce(part + 1)
        consume(part)

    @pl.when(j == pl.num_programs(1) - 1)
    def _():
        pad = LANES - ds
        kn = jnp.concatenate([cn_ref[0], jnp.zeros((pad, cn_ref.shape[-1]), F32)], axis=0).astype(BF16)
        rn = jnp.concatenate([rn_ref[0], jnp.zeros((pad, rn_ref.shape[-1]), F32)], axis=0).astype(BF16)
        s = _scores(qc, qr, kn, rn)
        t_q = lax.broadcasted_iota(jnp.int32, s.shape, 0) & (ds - 1)
        t_k = lax.broadcasted_iota(jnp.int32, s.shape, 1)
        s = jnp.where(t_k <= t_q, s, -jnp.inf)
        _softmax_step(s, kn, m_scr, l_scr, acc_scr)
        o_ref[0] = _attn_finish(ds, wuv_ref, g_ref, l_scr, acc_scr)


def _attn_sample(page_table, qc, qr, c_new, r_new, cache_c, cache_r, w_uv, g_mla):
    db, rows, d_kv = qc.shape
    ds = c_new.shape[1]
    n_pages = page_table.shape[1]
    page = cache_c.shape[1]
    npg = math.gcd(PAGES_PER_STEP, n_pages)
    n_part = math.gcd(SAMPLE_PARTS, npg)
    assert ds & (ds - 1) == 0 and ds <= LANES
    assert cache_r.shape[1:] == (MLA_ROPE_DIM, page)
    d_out = w_uv.shape[0] * w_uv.shape[2]
    bmap = lambda b, j, pt: (b, 0, 0)
    c2 = lambda b, j, pt: (0, 0)
    c3 = lambda b, j, pt: (0, 0, 0)

    def page_map(pg):
        return lambda b, j, pt: (pt[b, j * npg + pg], 0, 0)

    in_specs = [
        pl.BlockSpec((1, rows, d_kv), bmap),
        pl.BlockSpec((1, rows, MLA_ROPE_DIM), bmap),
        pl.BlockSpec((1, ds, d_kv), bmap),
        pl.BlockSpec((1, ds, MLA_ROPE_DIM), bmap),
        pl.BlockSpec(w_uv.shape, c3),
        pl.BlockSpec(g_mla.shape, c2),
    ]
    in_specs += [pl.BlockSpec((1, page, d_kv), page_map(pg)) for pg in range(npg)]
    in_specs += [pl.BlockSpec((1, MLA_ROPE_DIM, page), page_map(pg)) for pg in range(npg)]
    part_keys = (npg // n_part) * page
    grid_spec = pltpu.PrefetchScalarGridSpec(
        num_scalar_prefetch=1,
        grid=(db, n_pages // npg),
        in_specs=in_specs,
        out_specs=pl.BlockSpec((1, ds, d_out), bmap),
        scratch_shapes=[pltpu.VMEM((npg * page, d_kv), BF16), pltpu.VMEM((MLA_ROPE_DIM, npg * page), BF16),
                        pltpu.VMEM((n_part, rows, part_keys), F32), pltpu.VMEM((n_part, rows, 1), F32),
                        pltpu.VMEM((rows, 1), F32), pltpu.VMEM((rows, 1), F32), pltpu.VMEM((rows, d_kv), F32)],
    )
    return pl.pallas_call(
        functools.partial(_attn_sample_body, npg, page, ds, n_part),
        grid_spec=grid_spec,
        out_shape=jax.ShapeDtypeStruct((db, ds, d_out), F32),
        compiler_params=_cparams(("parallel", "arbitrary")),
        name="attn_sample",
    )(page_table, qc, qr, c_new, r_new, w_uv, g_mla, *([cache_c] * npg), *([cache_r] * npg))


def _cmul_add(cur, sh, a_r, a_i, half):
    return cur + a_r * sh + a_i * pltpu.roll(sh, half, axis=1)


def _ssm_prompt_body(nb, n_levels, pre, u_ref, wy_ref, wd_ref, wc_ref, dv_ref, ar_ref, ai_ref,
                     y_ref, hl_ref, scr):
    u = u_ref[0]
    ub = u.astype(BF16)
    rp, half = u.shape[0], wd_ref.shape[-1] // 2
    d = jnp.dot(ub, wd_ref[0], preferred_element_type=F32)
    scr[0:pre, :] = jnp.zeros((pre, scr.shape[1]), F32)
    scr[pre:pre + rp, :] = d
    scr[pre:pre + rp, :] = scr[pre - nb:pre - nb + rp, :]
    for k in range(n_levels):
        s = nb << k
        cur = scr[pre:pre + rp, :]
        sh = scr[pre - s:pre - s + rp, :]
        scr[pre:pre + rp, :] = _cmul_add(cur, sh, ar_ref[0, k:k + 1, :], ai_ref[0, k:k + 1, :], half)
    e = scr[pre:pre + rp, :]
    y_ref[0] = (jnp.dot(ub, wy_ref[0], preferred_element_type=F32)
                + jnp.dot(e.astype(BF16), wc_ref[0], preferred_element_type=F32)
                + u * dv_ref[0])
    hl_ref[0] = _cmul_add(d, e, ar_ref[0, 0:1, :], ai_ref[0, 0:1, :], half)


def _ssm_sample_body(u_ref, h_ref, wy_ref, wd_ref, wc_ref, dv_ref, ar_ref, ai_ref, y_ref, hl_ref):
    u = u_ref[0]
    ub = u.astype(BF16)
    e = h_ref[0]
    half = e.shape[-1] // 2
    d = jnp.dot(ub, wd_ref[0], preferred_element_type=F32)
    y_ref[0] = (jnp.dot(ub, wy_ref[0], preferred_element_type=F32)
                + jnp.dot(e.astype(BF16), wc_ref[0], preferred_element_type=F32)
                + u * dv_ref[0])
    hl_ref[0] = _cmul_add(d, e, ar_ref[0, 0:1, :], ai_ref[0, 0:1, :], half)


def _ssm_weights(a_re, a_im, log_dt, b_re, b_im, c_re, c_im, d_skip, t_chunk, n_levels):
    hi = lax.Precision.HIGHEST
    a = lax.complex(a_re.astype(F32), a_im.astype(F32))
    dt = jnp.exp(log_dt.astype(F32))[:, None]
    a_dt = a * dt
    a_bar = jnp.exp(a_dt)
    b_bar = ((a_bar - 1.0) / a)[..., None] * lax.complex(b_re.astype(F32), b_im.astype(F32))
    c = lax.complex(c_re.astype(F32), c_im.astype(F32))
    g, p_state, ch = b_bar.shape
    k = jnp.arange(t_chunk + 1, dtype=F32)
    a_pow = jnp.exp(a_dt[None] * k[:, None, None])
    kern = jnp.einsum('gcp,kgp,gpd->kgcd', c, a_pow[:t_chunk], b_bar, precision=hi).real
    s_idx = jnp.arange(t_chunk)[:, None]
    t_idx = jnp.arange(t_chunk)[None, :]
    lag = t_idx - s_idx
    wy = jnp.where((lag >= 0)[:, :, None, None, None], kern[jnp.clip(lag, 0, t_chunk - 1)], 0.0)
    wy = wy.transpose(2, 0, 4, 1, 3).reshape(g, t_chunk * ch, t_chunk * ch)
    wd = a_pow[:t_chunk][::-1][:, :, :, None] * b_bar[None]
    wd = wd.transpose(1, 0, 3, 2).reshape(g, t_chunk * ch, p_state)
    wd = jnp.concatenate([wd.real, wd.imag], axis=-1)
    gm = c[None] * a_pow[1:][:, :, None, :]
    gm = gm.transpose(1, 3, 0, 2).reshape(g, p_state, t_chunk * ch)
    wc = jnp.concatenate([gm.real, -gm.imag], axis=1)
    dv = jnp.tile(d_skip.astype(F32).reshape(g, 1, ch), (1, 1, t_chunk))
    lev = (t_chunk * (2.0 ** jnp.arange(n_levels, dtype=F32)))
    a_lev = jnp.exp(a_dt[:, None, :] * lev[None, :, None])
    a_r = jnp.concatenate([a_lev.real, a_lev.real], axis=-1)
    a_i = jnp.concatenate([-a_lev.imag, a_lev.imag], axis=-1)
    return wy.astype(BF16), wd.astype(BF16), wc.astype(BF16), dv, a_r, a_i


def _ssm_prompt(u_rows, nb, n_chunks, ops):
    wy, wd, wc, dv, a_r, a_i = ops
    g, rp, tc = u_rows.shape
    n_levels = a_r.shape[1]
    st = wd.shape[-1]
    pre = -(-(nb << (n_levels - 1)) // SUBLANES) * SUBLANES
    gmap = lambda i: (i, 0, 0)
    return pl.pallas_call(
        functools.partial(_ssm_prompt_body, nb, n_levels, pre),
        grid=(g,),
        in_specs=[pl.BlockSpec((1, rp, tc), gmap), pl.BlockSpec((1,) + wy.shape[1:], gmap),
                  pl.BlockSpec((1,) + wd.shape[1:], gmap), pl.BlockSpec((1,) + wc.shape[1:], gmap),
                  pl.BlockSpec((1,) + dv.shape[1:], gmap), pl.BlockSpec((1,) + a_r.shape[1:], gmap),
                  pl.BlockSpec((1,) + a_i.shape[1:], gmap)],
        out_specs=[pl.BlockSpec((1, rp, tc), gmap), pl.BlockSpec((1, rp, st), gmap)],
        out_shape=[jax.ShapeDtypeStruct((g, rp, tc), F32), jax.ShapeDtypeStruct((g, rp, st), F32)],
        scratch_shapes=[pltpu.VMEM((pre + rp, st), F32)],
        compiler_params=_cparams(("parallel",)),
        name="ssm_prompt",
    )(u_rows, wy, wd, wc, dv, a_r, a_i)


def _ssm_sample(u_rows, h_rows, ops):
    wy, wd, wc, dv, a_r, a_i = ops
    g, r, tc = u_rows.shape
    st = wd.shape[-1]
    gmap = lambda i: (i, 0, 0)
    return pl.pallas_call(
        _ssm_sample_body,
        grid=(g,),
        in_specs=[pl.BlockSpec((1, r, tc), gmap), pl.BlockSpec((1, r, st), gmap),
                  pl.BlockSpec((1,) + wy.shape[1:], gmap), pl.BlockSpec((1,) + wd.shape[1:], gmap),
                  pl.BlockSpec((1,) + wc.shape[1:], gmap), pl.BlockSpec((1,) + dv.shape[1:], gmap),
                  pl.BlockSpec((1,) + a_r.shape[1:], gmap), pl.BlockSpec((1,) + a_i.shape[1:], gmap)],
        out_specs=[pl.BlockSpec((1, r, tc), gmap), pl.BlockSpec((1, r, st), gmap)],
        out_shape=[jax.ShapeDtypeStruct((g, r, tc), F32), jax.ShapeDtypeStruct((g, r, st), F32)],
        compiler_params=_cparams(("parallel",)),
        name="ssm_sample",
    )(u_rows, h_rows, wy, wd, wc, dv, a_r, a_i)


def _mix_out_body(ys_ref, ya_ref, x_ref, wglu_ref, bglu_ref, gs_ref, wo_ref, gpost_ref, o_ref):
    z = jax.nn.gelu(ys_ref[...])
    gate = jax.nn.sigmoid(jnp.dot(z.astype(BF16), wglu_ref[...], preferred_element_type=F32) + bglu_ref[...])
    ns = _rms(z * gate, gs_ref[...]).astype(BF16)
    w = ns.shape[-1]
    y = (jnp.dot(ns, wo_ref[:w, :], preferred_element_type=F32)
         + jnp.dot(ya_ref[...].astype(BF16), wo_ref[w:, :], preferred_element_type=F32))
    o_ref[...] = x_ref[...] + _rms(y, gpost_ref[...])


def _mix_out(ys, ya, x, w_glu, b_glu, g_ssm, w_o, g_post, name):
    n, d = x.shape
    w = ys.shape[1]
    tm = _row_tile(n)
    row = lambda i: (i, 0)
    c2 = lambda i: (0, 0)
    return pl.pallas_call(
        _mix_out_body,
        grid=(n // tm,),
        in_specs=[pl.BlockSpec((tm, w), row), pl.BlockSpec((tm, ya.shape[1]), row), pl.BlockSpec((tm, d), row),
                  pl.BlockSpec(w_glu.shape, c2), pl.BlockSpec(b_glu.shape, c2), pl.BlockSpec(g_ssm.shape, c2),
                  pl.BlockSpec(w_o.shape, c2), pl.BlockSpec(g_post.shape, c2)],
        out_specs=pl.BlockSpec((tm, d), row),
        out_shape=jax.ShapeDtypeStruct((n, d), F32),
        compiler_params=_cparams(("parallel",)),
        name=name,
    )(ys, ya, x, w_glu, b_glu, g_ssm, w_o, g_post)


def _rope_tables(pos):
    inv = ROPE_THETA ** (-jnp.arange(ROPE_HALF, dtype=F32) / ROPE_HALF)
    ang = pos.astype(F32)[:, None] * inv[None, :]
    reps = LANES // ROPE_HALF
    return jnp.tile(jnp.cos(ang), (1, reps)), jnp.tile(jnp.sin(ang), (1, reps))


def _uq_column_order():
    per = MLA_NOPE_DIM + MLA_ROPE_DIM
    heads = np.arange(MLA_HEADS)[:, None]
    nope = (heads * per + np.arange(MLA_NOPE_DIM)[None, :]).reshape(-1)
    rope1 = (heads * per + MLA_NOPE_DIM + np.arange(ROPE_HALF)[None, :]).reshape(-1)
    rope2 = (heads * per + MLA_NOPE_DIM + ROPE_HALF + np.arange(ROPE_HALF)[None, :]).reshape(-1)
    return np.concatenate([nope, rope1, rope2])


def kernel(x_prompt, x_sample, cache_kv_latent, cache_k_rope, state_ssm_re, state_ssm_im, page_table, meta_tokens, g_ff1_pre, w_ff1_gate, w_ff1_up, w_ff1_down, g_ff1_post, g_mix_pre, w_in, ssm_a_re, ssm_a_im, ssm_log_dt, ssm_b_re, ssm_b_im, ssm_c_re, ssm_c_im, ssm_d, w_glu, b_glu, g_q_norm, w_uq, g_kv_norm, w_uk, w_uv, g_ssm_out, g_mla_out, w_o, g_mix_post, g_ff2_pre, w_ff2_gate, w_ff2_up, w_ff2_down, g_ff2_post):
    depth = w_in.shape[0]
    assert depth == 1, "single-layer step"
    bp, seq, d_model = x_prompt.shape
    db, ds, _ = x_sample.shape
    n_pages = page_table.shape[1]
    page = cache_kv_latent.shape[2]
    past_len = n_pages * page
    d_kv = cache_kv_latent.shape[3]
    d_q = w_uq.shape[1]
    n_groups, n_state = ssm_a_re.shape[1], ssm_a_re.shape[2]
    d_u = n_groups * SSM_GROUP
    dims = (d_u, d_q, d_kv)
    l = 0
    row = lambda v: v[l].reshape(1, -1).astype(F32)

    ff1_w = _ffn_weights(w_ff1_gate[l], w_ff1_up[l], w_ff1_down[l])
    ff2_w = _ffn_weights(w_ff2_gate[l], w_ff2_up[l], w_ff2_down[l])
    w_in_b = w_in[l].astype(BF16)
    w_uq_b = w_uq[l][:, _uq_column_order()].astype(BF16)
    w_ukt = jnp.transpose(w_uk[l], (1, 2, 0)).astype(BF16)
    w_uv_b = jnp.transpose(w_uv[l], (1, 0, 2)).astype(BF16)
    w_uvt_b = jnp.transpose(w_uv[l], (1, 2, 0)).astype(BF16)
    w_glu_b = w_glu[l].astype(BF16)
    w_o_b = w_o[l].astype(BF16)

    xs = [x_prompt.reshape(bp * seq, d_model), x_sample.reshape(db * ds, d_model), meta_tokens.astype(F32)]
    names = ["prompt", "sample", "meta"]
    pos = [N_META + jnp.tile(jnp.arange(seq), bp), past_len + jnp.tile(jnp.arange(ds), db), jnp.arange(N_META)]

    x1, pr = [], []
    for x, nm, ps in zip(xs, names, pos):
        y, hn = _ffn(x, row(g_ff1_pre), ff1_w, row(g_ff1_post), row(g_mix_pre), True, "ffn1_" + nm)
        cos, sin = _rope_tables(ps)
        x1.append(y)
        pr.append(_proj(hn, w_in_b, row(g_q_norm), w_uq_b, w_ukt, row(g_kv_norm), cos, sin, dims, "proj_" + nm))
    (u_p, ckv_p, ckvb_p, ckvt_p, kr_p, krb_p, qc_p, qr_p) = pr[0]
    (u_s, ckv_s, _, _, kr_s, _, qc_s, qr_s) = pr[1]
    (u_m, ckv_m, ckvb_m, ckvt_m, kr_m, krb_m, _, _) = pr[2]

    t_p = SSM_CHUNK
    assert N_META == t_p and seq % t_p == 0
    n_chunks = seq // t_p + 1
    n_levels = max(1, (n_chunks - 1).bit_length())
    ssm_w = (ssm_a_re[l], ssm_a_im[l], ssm_log_dt[l], ssm_b_re[l], ssm_b_im[l], ssm_c_re[l], ssm_c_im[l], ssm_d[l])
    ops_p = _ssm_weights(*ssm_w, t_p, n_levels)
    ops_s = _ssm_weights(*ssm_w, ds, 1)
    tc = t_p * SSM_GROUP
    up = u_p.reshape(bp, seq // t_p, t_p, n_groups, SSM_GROUP).transpose(3, 1, 0, 2, 4).reshape(n_groups, seq // t_p, bp, tc)
    um = u_m.reshape(1, t_p, n_groups, SSM_GROUP).transpose(2, 0, 1, 3).reshape(n_groups, 1, 1, tc)
    u_rows = jnp.concatenate([jnp.broadcast_to(um, (n_groups, 1, bp, tc)), up], axis=1).reshape(n_groups, n_chunks * bp, tc)
    r_p = n_chunks * bp
    rp = -(-r_p // SUBLANES) * SUBLANES
    u_rows = jnp.pad(u_rows, ((0, 0), (0, rp - r_p), (0, 0)))
    y_rows, hl_rows = _ssm_prompt(u_rows, bp, n_chunks, ops_p)
    ys_p = (y_rows[:, bp:r_p].reshape(n_groups, seq // t_p, bp, t_p, SSM_GROUP)
            .transpose(2, 1, 3, 0, 4).reshape(bp * seq, d_u))
    hl_p = hl_rows[:, r_p - bp:r_p].transpose(1, 0, 2)

    us_rows = u_s.reshape(db, ds, n_groups, SSM_GROUP).transpose(2, 0, 1, 3).reshape(n_groups, db, ds * SSM_GROUP)
    h0_rows = jnp.concatenate([state_ssm_re[l], state_ssm_im[l]], axis=-1).astype(F32).transpose(1, 0, 2)
    ysr, hl_s = _ssm_sample(us_rows, h0_rows, ops_s)
    ys_s = ysr.reshape(n_groups, db, ds, SSM_GROUP).transpose(1, 2, 0, 3).reshape(db * ds, d_u)
    hl_s = hl_s.transpose(1, 0, 2)

    g_mla = row(g_mla_out)
    pad_m = LANES - N_META
    kmc = jnp.pad(ckvb_m, ((0, pad_m), (0, 0)))
    kmr = jnp.pad(krb_m, ((0, pad_m), (0, 0)))
    kmct = jnp.pad(ckvt_m, ((0, 0), (0, pad_m)))
    ya_p = _attn_prompt(qc_p, qr_p, ckvb_p.reshape(bp, seq, d_kv), ckvt_p,
                        krb_p.reshape(bp, seq, MLA_ROPE_DIM), kmc, kmct, kmr, w_uvt_b, g_mla)
    to_seq = lambda q: q.reshape(MLA_HEADS, db, ds, q.shape[-1]).transpose(1, 0, 2, 3).reshape(db, MLA_HEADS * ds, q.shape[-1])
    ya_s = _attn_sample(page_table, to_seq(qc_s), to_seq(qr_s), ckv_s.reshape(db, ds, d_kv),
                        kr_s.reshape(db, ds, MLA_ROPE_DIM), cache_kv_latent[l],
                        jnp.swapaxes(cache_k_rope[l], 1, 2), w_uv_b, g_mla)
    ya_s = ya_s.reshape(db * ds, -1)

    outs = []
    for x, ys, ya, nm in ((x1[0], ys_p, ya_p, "prompt"), (x1[1], ys_s, ya_s, "sample")):
        x2 = _mix_out(ys, ya, x, w_glu_b, row(b_glu), row(g_ssm_out), w_o_b, row(g_mix_post), "mix_out_" + nm)
        y, _ = _ffn(x2, row(g_ff2_pre), ff2_w, row(g_ff2_post), row(g_ff2_post), False, "ffn2_" + nm)
        outs.append(y)

    y_prompt = outs[0].reshape(bp, seq, d_model)
    y_sample = outs[1].reshape(db, ds, d_model)
    meta_b = lambda v: jnp.broadcast_to(v[None], (bp,) + v.shape)
    new_ckv_p = jnp.concatenate([meta_b(ckv_m), ckv_p.reshape(bp, seq, d_kv)], axis=1)[None]
    new_kr_p = jnp.concatenate([meta_b(kr_m), kr_p.reshape(bp, seq, MLA_ROPE_DIM)], axis=1)[None]
    return (y_prompt, y_sample, new_ckv_p, new_kr_p,
            hl_p[None, :, :, :n_state], hl_p[None, :, :, n_state:],
            ckv_s.reshape(1, db, ds, d_kv), kr_s.reshape(1, db, ds, MLA_ROPE_DIM),
            hl_s[None, :, :, :n_state], hl_s[None, :, :, n_state:])
```

```python
import functools
import math

import numpy as np
import jax
import jax.numpy as jnp
from jax import lax
from jax.experimental import pallas as pl
from jax.experimental.pallas import tpu as pltpu

F32 = jnp.float32
BF16 = jnp.bfloat16

N_META = 16
SSM_GROUP = 16
MLA_HEADS = 8
MLA_NOPE_DIM = 64
MLA_ROPE_DIM = 32
ROPE_HALF = MLA_ROPE_DIM // 2
ROPE_THETA = 10000.0
RMS_EPS = 1e-6
ATTN_SCALE = (MLA_NOPE_DIM + MLA_ROPE_DIM) ** -0.5

LANES = 128
SUBLANES = 8
VMEM_LIMIT = 56 * 1024 * 1024

ROW_TILE = 512
FFN_ROW_TILE = 1024
FF_TILE = 256
ATTN_BQ = 256
ATTN_BK = 512
ATTN_COLS = 512
PAGES_PER_STEP = 64
SAMPLE_PARTS = 2
SSM_CHUNK = 16


def _rms(x, g):
    return (x * lax.rsqrt(jnp.mean(x * x, axis=-1, keepdims=True) + RMS_EPS)) * g


def _row_tile(n):
    return ROW_TILE if n % ROW_TILE == 0 else n


def _cparams(sem):
    return pltpu.CompilerParams(dimension_semantics=sem, vmem_limit_bytes=VMEM_LIMIT)


def _ffn_body(emit_norm, x_ref, gpre_ref, wg_ref, wu_ref, wd_ref, gpost_ref, gnext_ref, *refs):
    if emit_norm:
        y_ref, hn_ref, h_scr, acc_scr = refs
    else:
        y_ref, h_scr, acc_scr = refs
        hn_ref = None
    j = pl.program_id(1)

    @pl.when(j == 0)
    def _():
        h_scr[...] = _rms(x_ref[...], gpre_ref[...]).astype(BF16)
        acc_scr[...] = jnp.zeros_like(acc_scr)

    h = h_scr[...]
    g = jnp.dot(h, wg_ref[...], preferred_element_type=F32)
    u = jnp.dot(h, wu_ref[...], preferred_element_type=F32)
    a = (g * jax.nn.sigmoid(g)) * u
    acc_scr[...] += jnp.dot(a.astype(BF16), wd_ref[...], preferred_element_type=F32)

    @pl.when(j == pl.num_programs(1) - 1)
    def _():
        y = x_ref[...] + 0.5 * _rms(acc_scr[...], gpost_ref[...])
        y_ref[...] = y
        if emit_norm:
            hn_ref[...] = _rms(y, gnext_ref[...]).astype(BF16)


def _ffn_weights(wg, wu, wd):
    d, d_ff = wg.shape
    tf = FF_TILE
    assert d_ff % tf == 0
    chunk = lambda w: w.astype(BF16).reshape(d, d_ff // tf, tf).transpose(1, 0, 2)
    return chunk(wg), chunk(wu), wd.astype(BF16).reshape(d_ff // tf, tf, d)


def _ffn(x, g_pre, weights, g_post, g_next, emit_norm, name):
    wg, wu, wd = weights
    n, d = x.shape
    nj, _, tf = wg.shape
    tm = FFN_ROW_TILE if n % FFN_ROW_TILE == 0 else n
    row = lambda i, j: (i, 0)
    vec = lambda i, j: (0, 0)
    chunk = lambda i, j: (j, 0, 0)
    out_shape = [jax.ShapeDtypeStruct((n, d), F32)]
    out_specs = [pl.BlockSpec((tm, d), row)]
    if emit_norm:
        out_shape.append(jax.ShapeDtypeStruct((n, d), BF16))
        out_specs.append(pl.BlockSpec((tm, d), row))
    res = pl.pallas_call(
        functools.partial(_ffn_body, emit_norm),
        grid=(n // tm, nj),
        in_specs=[
            pl.BlockSpec((tm, d), row),
            pl.BlockSpec((1, d), vec),
            pl.BlockSpec((None, d, tf), chunk),
            pl.BlockSpec((None, d, tf), chunk),
            pl.BlockSpec((None, tf, d), chunk),
            pl.BlockSpec((1, d), vec),
            pl.BlockSpec((1, d), vec),
        ],
        out_specs=out_specs,
        out_shape=out_shape,
        scratch_shapes=[pltpu.VMEM((tm, d), BF16), pltpu.VMEM((tm, d), F32)],
        compiler_params=_cparams(("parallel", "arbitrary")),
        name=name,
    )(x, g_pre, wg, wu, wd, g_post, g_next)
    return res if emit_norm else (res[0], None)


def _proj_body(d_u, d_q, d_kv, t_chunk, h_ref, win_ref, gq_ref, wuq_ref, wukt_ref, gkv_ref, cos_ref, sin_ref,
               u_ref, ckv_ref, ckvb_ref, ckvt_ref, kr_ref, krb_ref, qc_ref, qr_ref, *maybe_urows):
    proj = jnp.dot(h_ref[...], win_ref[...], preferred_element_type=F32)
    off_q, off_kv, off_kr = d_u, d_u + d_q, d_u + d_q + d_kv
    u_ref[...] = proj[:, :off_q]
    if t_chunk:
        urows_ref, slab_scr = maybe_urows
        n_rows = u_ref.shape[0] // t_chunk
        per_slab = LANES // SSM_GROUP
        for k in range(d_u // LANES):
            slab_scr[k] = proj[:, k * LANES:(k + 1) * LANES]
        for k in range(d_u // LANES):
            per_step = [slab_scr[k, pl.ds(s, n_rows, stride=t_chunk), :] for s in range(t_chunk)]
            for gg in range(per_slab):
                sl = slice(gg * SSM_GROUP, (gg + 1) * SSM_GROUP)
                urows_ref[k * per_slab + gg] = jnp.concatenate([x[:, sl] for x in per_step], axis=-1)
    cq = _rms(proj[:, off_q:off_kv], gq_ref[...]).astype(BF16)
    q = jnp.dot(cq, wuq_ref[...], preferred_element_type=F32) * ATTN_SCALE
    n_nope = MLA_HEADS * MLA_NOPE_DIM
    for hd in range(MLA_HEADS):
        qn = q[:, hd * MLA_NOPE_DIM:(hd + 1) * MLA_NOPE_DIM].astype(BF16)
        qc_ref[hd] = jnp.dot(qn, wukt_ref[hd], preferred_element_type=F32).astype(BF16)
    cos = cos_ref[...]
    sin = sin_ref[...]
    r1 = q[:, n_nope:n_nope + LANES]
    r2 = q[:, n_nope + LANES:n_nope + 2 * LANES]
    o1 = r1 * cos - r2 * sin
    o2 = r2 * cos + r1 * sin
    for hd in range(MLA_HEADS):
        sl = slice(hd * ROPE_HALF, (hd + 1) * ROPE_HALF)
        qr_ref[hd] = jnp.concatenate([o1[:, sl], o2[:, sl]], axis=-1).astype(BF16)
    ckv = _rms(proj[:, off_kv:off_kr], gkv_ref[...])
    ckv_ref[...] = ckv
    ckvb_ref[...] = ckv.astype(BF16)
    ckvt_ref[...] = jnp.transpose(ckv).astype(BF16)
    x1 = proj[:, off_kr:off_kr + ROPE_HALF]
    x2 = proj[:, off_kr + ROPE_HALF:off_kr + MLA_ROPE_DIM]
    c16 = cos[:, :ROPE_HALF]
    s16 = sin[:, :ROPE_HALF]
    kr = jnp.concatenate([x1 * c16 - x2 * s16, x2 * c16 + x1 * s16], axis=-1)
    kr_ref[...] = kr
    krb_ref[...] = kr.astype(BF16)


def _proj(hn, w_in, g_q, w_uq, w_ukt, g_kv, cos, sin, dims, t_chunk, name):
    n, d = hn.shape
    d_u, d_q, d_kv = dims
    tm = _row_tile(n)
    row = lambda i: (i, 0)
    full2 = lambda i: (0, 0)
    full3 = lambda i: (0, 0, 0)
    hrow = lambda i: (0, i, 0)
    extra_specs, extra_shapes, scratch = [], [], []
    if t_chunk:
        assert tm % (t_chunk * SUBLANES) == 0 and d_u % LANES == 0
        n_groups = d_u // SSM_GROUP
        extra_specs = [pl.BlockSpec((n_groups, tm // t_chunk, t_chunk * SSM_GROUP), hrow)]
        extra_shapes = [jax.ShapeDtypeStruct((n_groups, n // t_chunk, t_chunk * SSM_GROUP), F32)]
        scratch = [pltpu.VMEM((d_u // LANES, tm, LANES), F32)]
    return pl.pallas_call(
        functools.partial(_proj_body, d_u, d_q, d_kv, t_chunk),
        grid=(n // tm,),
        in_specs=[
            pl.BlockSpec((tm, d), row),
            pl.BlockSpec(w_in.shape, full2),
            pl.BlockSpec(g_q.shape, full2),
            pl.BlockSpec(w_uq.shape, full2),
            pl.BlockSpec(w_ukt.shape, full3),
            pl.BlockSpec(g_kv.shape, full2),
            pl.BlockSpec((tm, LANES), row),
            pl.BlockSpec((tm, LANES), row),
        ],
        out_specs=[
            pl.BlockSpec((tm, d_u), row),
            pl.BlockSpec((tm, d_kv), row),
            pl.BlockSpec((tm, d_kv), row),
            pl.BlockSpec((d_kv, tm), lambda i: (0, i)),
            pl.BlockSpec((tm, MLA_ROPE_DIM), row),
            pl.BlockSpec((tm, MLA_ROPE_DIM), row),
            pl.BlockSpec((MLA_HEADS, tm, d_kv), hrow),
            pl.BlockSpec((MLA_HEADS, tm, MLA_ROPE_DIM), hrow),
        ] + extra_specs,
        out_shape=[
            jax.ShapeDtypeStruct((n, d_u), F32),
            jax.ShapeDtypeStruct((n, d_kv), F32),
            jax.ShapeDtypeStruct((n, d_kv), BF16),
            jax.ShapeDtypeStruct((d_kv, n), BF16),
            jax.ShapeDtypeStruct((n, MLA_ROPE_DIM), F32),
            jax.ShapeDtypeStruct((n, MLA_ROPE_DIM), BF16),
            jax.ShapeDtypeStruct((MLA_HEADS, n, d_kv), BF16),
            jax.ShapeDtypeStruct((MLA_HEADS, n, MLA_ROPE_DIM), BF16),
        ] + extra_shapes,
        scratch_shapes=scratch,
        compiler_params=_cparams(("parallel",)),
        name=name,
    )(hn, w_in, g_q, w_uq, w_ukt, g_kv, cos, sin)


def _scores(qc, qr, kc, kr):
    nt = (((1,), (1,)), ((), ()))
    return (lax.dot_general(qc, kc, nt, preferred_element_type=F32)
            + lax.dot_general(qr, kr, nt, preferred_element_type=F32))


def _softmax_step(s, kc, m_scr, l_scr, acc_scr):
    m_old = m_scr[...]
    m_new = jnp.maximum(m_old, jnp.max(s, axis=-1, keepdims=True))
    alpha = jnp.exp(m_old - m_new)
    p = jnp.exp(s - m_new)
    l_scr[...] = alpha * l_scr[...] + jnp.sum(p, axis=-1, keepdims=True)
    acc_scr[...] = alpha * acc_scr[...] + jnp.dot(p.astype(BF16), kc, preferred_element_type=F32)
    m_scr[...] = m_new


def _attn_finish(rows_per_head, wuv_ref, g_ref, l_scr, acc_scr):
    o = (acc_scr[...] / l_scr[...]).astype(BF16)
    outs = []
    for hd in range(MLA_HEADS):
        oh = o[hd * rows_per_head:(hd + 1) * rows_per_head]
        outs.append(jnp.dot(oh, wuv_ref[hd], preferred_element_type=F32))
    return _rms(jnp.concatenate(outs, axis=-1), g_ref[...])


def _attn_cols_step(parts, m_blk, cols, m_scr, l_scr, acc_scr):
    m_old = m_scr[:, cols]
    m_new = jnp.maximum(m_old, m_blk)
    alpha = jnp.exp(m_old - m_new)
    l_new = alpha * l_scr[:, cols]
    acc_new = alpha * acc_scr[:, cols]
    for s, vt in parts:
        p = jnp.exp(s - m_new)
        l_new = l_new + jnp.sum(p, axis=0, keepdims=True)
        acc_new = acc_new + jnp.dot(vt, p.astype(BF16), preferred_element_type=F32)
    l_scr[:, cols] = l_new
    acc_scr[:, cols] = acc_new
    m_scr[:, cols] = m_new


def _attn_prompt_body(bq, bk, ncol, qc_ref, qr_ref, kc_ref, kct_ref, kr_ref, kmc_ref, kmct_ref, kmr_ref,
                      wuv_ref, g_ref, o_ref, m_scr, l_scr, acc_scr, s_scr, mb_scr):
    i = pl.program_id(1)
    nt = (((1,), (1,)), ((), ()))
    heads_per = ncol // bq
    n_groups = MLA_HEADS // heads_per
    groups = [slice(gi * ncol, (gi + 1) * ncol) for gi in range(n_groups)]

    def scores(gi, kc, kr):
        hs = slice(gi * heads_per, (gi + 1) * heads_per)
        qc = qc_ref[hs].reshape(ncol, qc_ref.shape[-1])
        qr = qr_ref[hs].reshape(ncol, qr_ref.shape[-1])
        return (lax.dot_general(kc, qc, nt, preferred_element_type=F32)
                + lax.dot_general(kr, qr, nt, preferred_element_type=F32))

    def produce(j, slot):
        start = pl.multiple_of(j * bk, bk)
        kc = kc_ref[0, pl.ds(start, bk), :]
        kr = kr_ref[0, pl.ds(start, bk), :]
        for gi, cols in enumerate(groups):
            s = scores(gi, kc, kr)
            s_scr[slot, :, cols] = s
            mb_scr[slot, :, cols] = jnp.max(s, axis=0, keepdims=True)

    def consume(j, slot, diagonal):
        start = pl.multiple_of(j * bk, bk)
        kct = kct_ref[:, pl.ds(start, bk)]
        for gi, cols in enumerate(groups):
            s = s_scr[slot, :, cols]
            if diagonal:
                k_pos = start + lax.broadcasted_iota(jnp.int32, s.shape, 0)
                q_pos = i * bq + (lax.broadcasted_iota(jnp.int32, s.shape, 1) & (bq - 1))
                s = jnp.where(k_pos <= q_pos, s, -jnp.inf)
                sm = scores(gi, kmc_ref[...], kmr_ref[...])
                sm = jnp.where(lax.broadcasted_iota(jnp.int32, sm.shape, 0) < N_META, sm, -jnp.inf)
                m_blk = jnp.maximum(jnp.max(s, axis=0, keepdims=True), jnp.max(sm, axis=0, keepdims=True))
                parts = [(s, kct), (sm, kmct_ref[...])]
            else:
                m_blk = mb_scr[slot, :, cols]
                parts = [(s, kct)]
            _attn_cols_step(parts, m_blk, cols, m_scr, l_scr, acc_scr)

    n_full = (i * bq) // bk
    m_scr[...] = jnp.full_like(m_scr, -jnp.inf)
    l_scr[...] = jnp.zeros_like(l_scr)
    acc_scr[...] = jnp.zeros_like(acc_scr)
    produce(0, 0)

    def pair(k, carry):
        j = 2 * k
        produce(j + 1, 1)
        consume(j, 0, False)
        produce(j + 2, 0)
        consume(j + 1, 1, False)
        return carry

    lax.fori_loop(0, n_full // 2, pair, 0)
    odd = (n_full & 1) == 1

    @pl.when(odd)
    def _():
        produce(n_full, 1)
        consume(n_full - 1, 0, False)
        consume(n_full, 1, True)

    @pl.when(jnp.logical_not(odd))
    def _():
        consume(n_full, 0, True)

    outs = []
    for hd in range(MLA_HEADS):
        cols = slice(hd * bq, (hd + 1) * bq)
        o_t = (acc_scr[:, cols] / l_scr[:, cols]).astype(BF16)
        outs.append(jnp.dot(wuv_ref[hd], o_t, preferred_element_type=F32))
    y = jnp.transpose(jnp.concatenate(outs, axis=0))
    o_ref[...] = _rms(y, g_ref[...]).astype(o_ref.dtype)


def _attn_prompt(qc, qr, kc, kct, kr, kmc, kmct, kmr, w_uvt, g_mla):
    nb, seq, d_kv = kc.shape
    bq, bk, ncol = ATTN_BQ, ATTN_BK, ATTN_COLS
    assert seq % bk == 0 and bk % bq == 0 and bq & (bq - 1) == 0 and N_META >= 1
    assert ncol % bq == 0 and (MLA_HEADS * bq) % ncol == 0
    nq = seq // bq
    rows = MLA_HEADS * bq
    d_out = w_uvt.shape[0] * w_uvt.shape[1]
    qmap = lambda b, i: (0, b * nq + i, 0)
    kmap = lambda b, i: (b, 0, 0)
    c2 = lambda b, i: (0, 0)
    c3 = lambda b, i: (0, 0, 0)
    return pl.pallas_call(
        functools.partial(_attn_prompt_body, bq, bk, ncol),
        grid=(nb, nq),
        in_specs=[
            pl.BlockSpec((MLA_HEADS, bq, d_kv), qmap),
            pl.BlockSpec((MLA_HEADS, bq, MLA_ROPE_DIM), qmap),
            pl.BlockSpec((1, seq, d_kv), kmap),
            pl.BlockSpec((d_kv, seq), lambda b, i: (0, b)),
            pl.BlockSpec((1, seq, MLA_ROPE_DIM), kmap),
            pl.BlockSpec(kmc.shape, c2),
            pl.BlockSpec(kmct.shape, c2),
            pl.BlockSpec(kmr.shape, c2),
            pl.BlockSpec(w_uvt.shape, c3),
            pl.BlockSpec(g_mla.shape, c2),
        ],
        out_specs=pl.BlockSpec((bq, d_out), lambda b, i: (b * nq + i, 0)),
        out_shape=jax.ShapeDtypeStruct((nb * seq, d_out), BF16),
        scratch_shapes=[pltpu.VMEM((1, rows), F32), pltpu.VMEM((1, rows), F32),
                        pltpu.VMEM((d_kv, rows), F32),
                        pltpu.VMEM((2, bk, rows), F32), pltpu.VMEM((2, 1, rows), F32)],
        compiler_params=_cparams(("parallel", "arbitrary")),
        name="attn_prompt",
    )(qc, qr, kc, kct, kr, kmc, kmct, kmr, w_uvt, g_mla)


def _attn_sample_body(npg, page, ds, n_part, pt_ref, qc_ref, qr_ref, cn_ref, rn_ref, wuv_ref, g_ref,
                      cache_c, cache_r, o_ref, pc_buf, pr_buf, sems, kc_scr, krt_scr, s_scr, mb_scr,
                      m_scr, l_scr, acc_scr):
    b = pl.program_id(0)
    j = pl.program_id(1)
    n_steps = pl.num_programs(1)
    step = b * n_steps + j
    slot = lax.rem(step, 2)
    qc = qc_ref[0]
    qr = qr_ref[0]
    nt = (((1,), (1,)), ((), ()))
    per = npg // n_part

    def page_copies(bb, jj, sl):
        out = []
        for pg in range(npg):
            idx = pt_ref[bb, jj * npg + pg]
            out.append(pltpu.make_async_copy(cache_c.at[idx], pc_buf.at[sl, pg], sems.at[0, sl]))
            out.append(pltpu.make_async_copy(cache_r.at[idx], pr_buf.at[sl, pg], sems.at[1, sl]))
        return out

    @pl.when(step == 0)
    def _():
        for cp in page_copies(0, 0, 0):
            cp.start()

    nxt = step + 1

    @pl.when(nxt < pl.num_programs(0) * n_steps)
    def _():
        for cp in page_copies(nxt // n_steps, lax.rem(nxt, n_steps), 1 - slot):
            cp.start()

    for cp in page_copies(b, j, slot):
        cp.wait()
    pc_refs = [pc_buf.at[slot, pg] for pg in range(npg)]
    pr_refs = [pr_buf.at[slot, pg] for pg in range(npg)]

    @pl.when(j == 0)
    def _():
        m_scr[...] = jnp.full_like(m_scr, -jnp.inf)
        l_scr[...] = jnp.zeros_like(l_scr)
        acc_scr[...] = jnp.zeros_like(acc_scr)

    def keys(part):
        return slice(part * per * page, (part + 1) * per * page)

    def produce(part):
        for pg in range(part * per, (part + 1) * per):
            kc_scr[pg * page:(pg + 1) * page, :] = pc_refs[pg][...].astype(BF16)
            krt_scr[:, pg * page:(pg + 1) * page] = pr_refs[pg][...].astype(BF16)
        s = (lax.dot_general(qc, kc_scr[keys(part), :], nt, preferred_element_type=F32)
             + jnp.dot(qr, krt_scr[:, keys(part)], preferred_element_type=F32))
        s_scr[part] = s
        mb_scr[part] = jnp.max(s, axis=-1, keepdims=True)

    def consume(part):
        s = s_scr[part]
        m_old = m_scr[...]
        m_new = jnp.maximum(m_old, mb_scr[part])
        alpha = jnp.exp(m_old - m_new)
        p = jnp.exp(s - m_new)
        l_scr[...] = alpha * l_scr[...] + jnp.sum(p, axis=-1, keepdims=True)
        acc_scr[...] = alpha * acc_scr[...] + jnp.dot(p.astype(BF16), kc_scr[keys(part), :],
                                                      preferred_element_type=F32)
        m_scr[...] = m_new

    produce(0)
    for part in range(n_part):
        if part + 1 < n_part:
            produce(part + 1)
        consume(part)

    @pl.when(j == pl.num_programs(1) - 1)
    def _():
        pad = LANES - ds
        kn = jnp.concatenate([cn_ref[0], jnp.zeros((pad, cn_ref.shape[-1]), F32)], axis=0).astype(BF16)
        rn = jnp.concatenate([rn_ref[0], jnp.zeros((pad, rn_ref.shape[-1]), F32)], axis=0).astype(BF16)
        s = _scores(qc, qr, kn, rn)
        t_q = lax.broadcasted_iota(jnp.int32, s.shape, 0) & (ds - 1)
        t_k = lax.broadcasted_iota(jnp.int32, s.shape, 1)
        s = jnp.where(t_k <= t_q, s, -jnp.inf)
        _softmax_step(s, kn, m_scr, l_scr, acc_scr)
        o_ref[0] = _attn_finish(ds, wuv_ref, g_ref, l_scr, acc_scr)


def _attn_sample(page_table, qc, qr, c_new, r_new, cache_c, cache_r, w_uv, g_mla):
    db, rows, d_kv = qc.shape
    ds = c_new.shape[1]
    n_pages = page_table.shape[1]
    page = cache_c.shape[1]
    npg = math.gcd(PAGES_PER_STEP, n_pages)
    n_part = math.gcd(SAMPLE_PARTS, npg)
    assert ds & (ds - 1) == 0 and ds <= LANES
    assert cache_r.shape[1:] == (MLA_ROPE_DIM, page)
    d_out = w_uv.shape[0] * w_uv.shape[2]
    bmap = lambda b, j, pt: (b, 0, 0)
    c2 = lambda b, j, pt: (0, 0)
    c3 = lambda b, j, pt: (0, 0, 0)
    in_specs = [
        pl.BlockSpec((1, rows, d_kv), bmap),
        pl.BlockSpec((1, rows, MLA_ROPE_DIM), bmap),
        pl.BlockSpec((1, ds, d_kv), bmap),
        pl.BlockSpec((1, ds, MLA_ROPE_DIM), bmap),
        pl.BlockSpec(w_uv.shape, c3),
        pl.BlockSpec(g_mla.shape, c2),
        pl.BlockSpec(memory_space=pl.ANY),
        pl.BlockSpec(memory_space=pl.ANY),
    ]
    part_keys = (npg // n_part) * page
    grid_spec = pltpu.PrefetchScalarGridSpec(
        num_scalar_prefetch=1,
        grid=(db, n_pages // npg),
        in_specs=in_specs,
        out_specs=pl.BlockSpec((1, ds, d_out), bmap),
        scratch_shapes=[pltpu.VMEM((2, npg, page, d_kv), F32), pltpu.VMEM((2, npg, MLA_ROPE_DIM, page), F32),
                        pltpu.SemaphoreType.DMA((2, 2)),
                        pltpu.VMEM((npg * page, d_kv), BF16), pltpu.VMEM((MLA_ROPE_DIM, npg * page), BF16),
                        pltpu.VMEM((n_part, rows, part_keys), F32), pltpu.VMEM((n_part, rows, 1), F32),
                        pltpu.VMEM((rows, 1), F32), pltpu.VMEM((rows, 1), F32), pltpu.VMEM((rows, d_kv), F32)],
    )
    return pl.pallas_call(
        functools.partial(_attn_sample_body, npg, page, ds, n_part),
        grid_spec=grid_spec,
        out_shape=jax.ShapeDtypeStruct((db, ds, d_out), F32),
        compiler_params=_cparams(("arbitrary", "arbitrary")),
        name="attn_sample",
    )(page_table, qc, qr, c_new, r_new, w_uv, g_mla, cache_c, cache_r)


def _cmul_add(cur, sh, a_r, a_i, half):
    return cur + a_r * sh + a_i * pltpu.roll(sh, half, axis=1)


def _ssm_prompt_body(nb, n_chunks, n_levels, pre, u_ref, um_ref, wy_ref, wd_ref, wc_ref, dv_ref, ar_ref, ai_ref,
                     y_ref, hl_ref, scr, e_scr):
    u = u_ref[0]
    ub = u.astype(BF16)
    half = wd_ref.shape[-1] // 2
    wd = wd_ref[0]
    d = jnp.dot(ub, wd, preferred_element_type=F32)
    h_meta = jnp.dot(um_ref[0].astype(BF16), wd, preferred_element_type=F32)[0:1]
    first = lax.broadcasted_iota(jnp.int32, (n_chunks, 1), 0) == 0
    for b in range(nb):
        scr[b, 0:pre, :] = jnp.zeros((pre, scr.shape[-1]), F32)
        scr[b, pre:pre + n_chunks, :] = d[b * n_chunks:(b + 1) * n_chunks]
    for b in range(nb):
        shifted = scr[b, pre - 1:pre - 1 + n_chunks, :]
        scr[b, pre:pre + n_chunks, :] = shifted + jnp.where(first, h_meta, 0.0)
    for k in range(n_levels):
        s = 1 << k
        for b in range(nb):
            cur = scr[b, pre:pre + n_chunks, :]
            sh = scr[b, pre - s:pre - s + n_chunks, :]
            scr[b, pre:pre + n_chunks, :] = _cmul_add(cur, sh, ar_ref[0, k:k + 1, :], ai_ref[0, k:k + 1, :], half)
    for b in range(nb):
        e_scr[b * n_chunks:(b + 1) * n_chunks, :] = scr[b, pre:pre + n_chunks, :]
    e = e_scr[...]
    y_ref[0] = (jnp.dot(ub, wy_ref[0], preferred_element_type=F32)
                + jnp.dot(e.astype(BF16), wc_ref[0], preferred_element_type=F32)
                + u * dv_ref[0])
    h_after = _cmul_add(d, e, ar_ref[0, 0:1, :], ai_ref[0, 0:1, :], half)
    for b in range(nb):
        last = (b + 1) * n_chunks - 1
        hl_ref[0, b:b + 1, :] = h_after[last:last + 1]


def _ssm_sample_body(u_ref, h_ref, wy_ref, wd_ref, wc_ref, dv_ref, ar_ref, ai_ref, y_ref, hl_ref):
    u = u_ref[0]
    ub = u.astype(BF16)
    e = h_ref[0]
    half = e.shape[-1] // 2
    d = jnp.dot(ub, wd_ref[0], preferred_element_type=F32)
    y_ref[0] = (jnp.dot(ub, wy_ref[0], preferred_element_type=F32)
                + jnp.dot(e.astype(BF16), wc_ref[0], preferred_element_type=F32)
                + u * dv_ref[0])
    hl_ref[0] = _cmul_add(d, e, ar_ref[0, 0:1, :], ai_ref[0, 0:1, :], half)


def _ssm_weights(a_re, a_im, log_dt, b_re, b_im, c_re, c_im, d_skip, t_chunk, n_levels):
    hi = lax.Precision.HIGHEST
    a = lax.complex(a_re.astype(F32), a_im.astype(F32))
    dt = jnp.exp(log_dt.astype(F32))[:, None]
    a_dt = a * dt
    a_bar = jnp.exp(a_dt)
    b_bar = ((a_bar - 1.0) / a)[..., None] * lax.complex(b_re.astype(F32), b_im.astype(F32))
    c = lax.complex(c_re.astype(F32), c_im.astype(F32))
    g, p_state, ch = b_bar.shape
    k = jnp.arange(t_chunk + 1, dtype=F32)
    a_pow = jnp.exp(a_dt[None] * k[:, None, None])
    kern = jnp.einsum('gcp,kgp,gpd->kgcd', c, a_pow[:t_chunk], b_bar, precision=hi).real
    s_idx = jnp.arange(t_chunk)[:, None]
    t_idx = jnp.arange(t_chunk)[None, :]
    lag = t_idx - s_idx
    wy = jnp.where((lag >= 0)[:, :, None, None, None], kern[jnp.clip(lag, 0, t_chunk - 1)], 0.0)
    wy = wy.transpose(2, 0, 4, 1, 3).reshape(g, t_chunk * ch, t_chunk * ch)
    wd = a_pow[:t_chunk][::-1][:, :, :, None] * b_bar[None]
    wd = wd.transpose(1, 0, 3, 2).reshape(g, t_chunk * ch, p_state)
    wd = jnp.concatenate([wd.real, wd.imag], axis=-1)
    gm = c[None] * a_pow[1:][:, :, None, :]
    gm = gm.transpose(1, 3, 0, 2).reshape(g, p_state, t_chunk * ch)
    wc = jnp.concatenate([gm.real, -gm.imag], axis=1)
    dv = jnp.tile(d_skip.astype(F32).reshape(g, 1, ch), (1, 1, t_chunk))
    lev = (t_chunk * (2.0 ** jnp.arange(n_levels, dtype=F32)))
    a_lev = jnp.exp(a_dt[:, None, :] * lev[None, :, None])
    a_r = jnp.concatenate([a_lev.real, a_lev.real], axis=-1)
    a_i = jnp.concatenate([-a_lev.imag, a_lev.imag], axis=-1)
    return wy.astype(BF16), wd.astype(BF16), wc.astype(BF16), dv, a_r, a_i


def _ssm_prompt(u_rows, um_rows, nb, ops):
    wy, wd, wc, dv, a_r, a_i = ops
    g, r, tc = u_rows.shape
    n_chunks = r // nb
    n_levels = a_r.shape[1]
    assert (1 << n_levels) >= n_chunks and n_chunks % SUBLANES == 0
    st = wd.shape[-1]
    pre = -(-(1 << (n_levels - 1)) // SUBLANES) * SUBLANES
    gmap = lambda i: (i, 0, 0)
    return pl.pallas_call(
        functools.partial(_ssm_prompt_body, nb, n_chunks, n_levels, pre),
        grid=(g,),
        in_specs=[pl.BlockSpec((1, r, tc), gmap), pl.BlockSpec((1,) + um_rows.shape[1:], gmap),
                  pl.BlockSpec((1,) + wy.shape[1:], gmap),
                  pl.BlockSpec((1,) + wd.shape[1:], gmap), pl.BlockSpec((1,) + wc.shape[1:], gmap),
                  pl.BlockSpec((1,) + dv.shape[1:], gmap), pl.BlockSpec((1,) + a_r.shape[1:], gmap),
                  pl.BlockSpec((1,) + a_i.shape[1:], gmap)],
        out_specs=[pl.BlockSpec((1, r, tc), gmap), pl.BlockSpec((1, nb, st), gmap)],
        out_shape=[jax.ShapeDtypeStruct((g, r, tc), F32), jax.ShapeDtypeStruct((g, nb, st), F32)],
        scratch_shapes=[pltpu.VMEM((nb, pre + n_chunks, st), F32), pltpu.VMEM((r, st), F32)],
        compiler_params=_cparams(("parallel",)),
        name="ssm_prompt",
    )(u_rows, um_rows, wy, wd, wc, dv, a_r, a_i)


def _ssm_sample(u_rows, h_rows, ops):
    wy, wd, wc, dv, a_r, a_i = ops
    g, r, tc = u_rows.shape
    st = wd.shape[-1]
    gmap = lambda i: (i, 0, 0)
    return pl.pallas_call(
        _ssm_sample_body,
        grid=(g,),
        in_specs=[pl.BlockSpec((1, r, tc), gmap), pl.BlockSpec((1, r, st), gmap),
                  pl.BlockSpec((1,) + wy.shape[1:], gmap), pl.BlockSpec((1,) + wd.shape[1:], gmap),
                  pl.BlockSpec((1,) + wc.shape[1:], gmap), pl.BlockSpec((1,) + dv.shape[1:], gmap),
                  pl.BlockSpec((1,) + a_r.shape[1:], gmap), pl.BlockSpec((1,) + a_i.shape[1:], gmap)],
        out_specs=[pl.BlockSpec((1, r, tc), gmap), pl.BlockSpec((1, r, st), gmap)],
        out_shape=[jax.ShapeDtypeStruct((g, r, tc), F32), jax.ShapeDtypeStruct((g, r, st), F32)],
        compiler_params=_cparams(("parallel",)),
        name="ssm_sample",
    )(u_rows, h_rows, wy, wd, wc, dv, a_r, a_i)


def _mix_out_body(t_chunk, ys_ref, ya_ref, x_ref, wglu_ref, bglu_ref, gs_ref, wo_ref, gpost_ref, o_ref,
                  *maybe_scr):
    if t_chunk:
        (slab_scr,) = maybe_scr
        n_slabs = slab_scr.shape[0]
        n_rows = slab_scr.shape[1] // t_chunk
        per_slab = LANES // SSM_GROUP
        for k in range(n_slabs):
            for t in range(t_chunk):
                sl = slice(t * SSM_GROUP, (t + 1) * SSM_GROUP)
                slab_scr[k, pl.ds(t, n_rows, stride=t_chunk), :] = jnp.concatenate(
                    [ys_ref[k * per_slab + gg, :, sl] for gg in range(per_slab)], axis=-1)
        ys = jnp.concatenate([slab_scr[k] for k in range(n_slabs)], axis=-1)
    else:
        ys = ys_ref[...]
    z = jax.nn.gelu(ys)
    gate = jax.nn.sigmoid(jnp.dot(z.astype(BF16), wglu_ref[...], preferred_element_type=F32) + bglu_ref[...])
    ns = _rms(z * gate, gs_ref[...]).astype(BF16)
    w = ns.shape[-1]
    y = (jnp.dot(ns, wo_ref[:w, :], preferred_element_type=F32)
         + jnp.dot(ya_ref[...].astype(BF16), wo_ref[w:, :], preferred_element_type=F32))
    o_ref[...] = x_ref[...] + _rms(y, gpost_ref[...])


def _mix_out(ys, ya, x, w_glu, b_glu, g_ssm, w_o, g_post, t_chunk, name):
    n, d = x.shape
    tm = _row_tile(n)
    row = lambda i: (i, 0)
    c2 = lambda i: (0, 0)
    if t_chunk:
        assert tm % (t_chunk * SUBLANES) == 0
        w = ys.shape[0] * SSM_GROUP
        assert w % LANES == 0
        ys_spec = pl.BlockSpec((ys.shape[0], tm // t_chunk, ys.shape[2]), lambda i: (0, i, 0))
        scratch = [pltpu.VMEM((w // LANES, tm, LANES), F32)]
    else:
        ys_spec = pl.BlockSpec((tm, ys.shape[1]), row)
        scratch = []
    return pl.pallas_call(
        functools.partial(_mix_out_body, t_chunk),
        grid=(n // tm,),
        in_specs=[ys_spec, pl.BlockSpec((tm, ya.shape[1]), row), pl.BlockSpec((tm, d), row),
                  pl.BlockSpec(w_glu.shape, c2), pl.BlockSpec(b_glu.shape, c2), pl.BlockSpec(g_ssm.shape, c2),
                  pl.BlockSpec(w_o.shape, c2), pl.BlockSpec(g_post.shape, c2)],
        out_specs=pl.BlockSpec((tm, d), row),
        out_shape=jax.ShapeDtypeStruct((n, d), F32),
        scratch_shapes=scratch,
        compiler_params=_cparams(("parallel",)),
        name=name,
    )(ys, ya, x, w_glu, b_glu, g_ssm, w_o, g_post)


def _rope_tables(pos):
    inv = ROPE_THETA ** (-jnp.arange(ROPE_HALF, dtype=F32) / ROPE_HALF)
    ang = pos.astype(F32)[:, None] * inv[None, :]
    reps = LANES // ROPE_HALF
    return jnp.tile(jnp.cos(ang), (1, reps)), jnp.tile(jnp.sin(ang), (1, reps))


def _uq_column_order():
    per = MLA_NOPE_DIM + MLA_ROPE_DIM
    heads = np.arange(MLA_HEADS)[:, None]
    nope = (heads * per + np.arange(MLA_NOPE_DIM)[None, :]).reshape(-1)
    rope1 = (heads * per + MLA_NOPE_DIM + np.arange(ROPE_HALF)[None, :]).reshape(-1)
    rope2 = (heads * per + MLA_NOPE_DIM + ROPE_HALF + np.arange(ROPE_HALF)[None, :]).reshape(-1)
    return np.concatenate([nope, rope1, rope2])


def kernel(x_prompt, x_sample, cache_kv_latent, cache_k_rope, state_ssm_re, state_ssm_im, page_table, meta_tokens, g_ff1_pre, w_ff1_gate, w_ff1_up, w_ff1_down, g_ff1_post, g_mix_pre, w_in, ssm_a_re, ssm_a_im, ssm_log_dt, ssm_b_re, ssm_b_im, ssm_c_re, ssm_c_im, ssm_d, w_glu, b_glu, g_q_norm, w_uq, g_kv_norm, w_uk, w_uv, g_ssm_out, g_mla_out, w_o, g_mix_post, g_ff2_pre, w_ff2_gate, w_ff2_up, w_ff2_down, g_ff2_post):
    depth = w_in.shape[0]
    assert depth == 1, "single-layer step"
    bp, seq, d_model = x_prompt.shape
    db, ds, _ = x_sample.shape
    n_pages = page_table.shape[1]
    page = cache_kv_latent.shape[2]
    past_len = n_pages * page
    d_kv = cache_kv_latent.shape[3]
    d_q = w_uq.shape[1]
    n_groups, n_state = ssm_a_re.shape[1], ssm_a_re.shape[2]
    d_u = n_groups * SSM_GROUP
    dims = (d_u, d_q, d_kv)
    l = 0
    row = lambda v: v[l].reshape(1, -1).astype(F32)

    ff1_w = _ffn_weights(w_ff1_gate[l], w_ff1_up[l], w_ff1_down[l])
    ff2_w = _ffn_weights(w_ff2_gate[l], w_ff2_up[l], w_ff2_down[l])
    w_in_b = w_in[l].astype(BF16)
    w_uq_b = w_uq[l][:, _uq_column_order()].astype(BF16)
    w_ukt = jnp.transpose(w_uk[l], (1, 2, 0)).astype(BF16)
    w_uv_b = jnp.transpose(w_uv[l], (1, 0, 2)).astype(BF16)
    w_uvt_b = jnp.transpose(w_uv[l], (1, 2, 0)).astype(BF16)
    w_glu_b = w_glu[l].astype(BF16)
    w_o_b = w_o[l].astype(BF16)

    xs = [x_prompt.reshape(bp * seq, d_model), x_sample.reshape(db * ds, d_model), meta_tokens.astype(F32)]
    names = ["prompt", "sample", "meta"]
    pos = [N_META + jnp.tile(jnp.arange(seq), bp), past_len + jnp.tile(jnp.arange(ds), db), jnp.arange(N_META)]

    t_p = SSM_CHUNK
    assert N_META == t_p and seq % t_p == 0
    x1, pr = [], []
    for x, nm, ps, t_rows in zip(xs, names, pos, (t_p, 0, 0)):
        y, hn = _ffn(x, row(g_ff1_pre), ff1_w, row(g_ff1_post), row(g_mix_pre), True, "ffn1_" + nm)
        cos, sin = _rope_tables(ps)
        x1.append(y)
        pr.append(_proj(hn, w_in_b, row(g_q_norm), w_uq_b, w_ukt, row(g_kv_norm), cos, sin, dims, t_rows,
                        "proj_" + nm))
    (_, ckv_p, ckvb_p, ckvt_p, kr_p, krb_p, qc_p, qr_p, u_rows) = pr[0]
    (u_s, ckv_s, _, _, kr_s, _, qc_s, qr_s) = pr[1]
    (u_m, ckv_m, ckvb_m, ckvt_m, kr_m, krb_m, _, _) = pr[2]

    n_levels = max(1, (seq // t_p - 1).bit_length())
    ssm_w = (ssm_a_re[l], ssm_a_im[l], ssm_log_dt[l], ssm_b_re[l], ssm_b_im[l], ssm_c_re[l], ssm_c_im[l], ssm_d[l])
    ops_p = _ssm_weights(*ssm_w, t_p, n_levels)
    ops_s = _ssm_weights(*ssm_w, ds, 1)
    tc = t_p * SSM_GROUP
    um_rows = u_m.reshape(1, t_p, n_groups, SSM_GROUP).transpose(2, 0, 1, 3).reshape(n_groups, 1, tc)
    um_rows = jnp.pad(um_rows, ((0, 0), (0, SUBLANES - 1), (0, 0)))
    ys_p, hl_p = _ssm_prompt(u_rows, um_rows, bp, ops_p)
    hl_p = hl_p.transpose(1, 0, 2)

    us_rows = u_s.reshape(db, ds, n_groups, SSM_GROUP).transpose(2, 0, 1, 3).reshape(n_groups, db, ds * SSM_GROUP)
    h0_rows = jnp.concatenate([state_ssm_re[l], state_ssm_im[l]], axis=-1).astype(F32).transpose(1, 0, 2)
    ysr, hl_s = _ssm_sample(us_rows, h0_rows, ops_s)
    ys_s = ysr.reshape(n_groups, db, ds, SSM_GROUP).transpose(1, 2, 0, 3).reshape(db * ds, d_u)
    hl_s = hl_s.transpose(1, 0, 2)

    g_mla = row(g_mla_out)
    pad_m = LANES - N_META
    kmc = jnp.pad(ckvb_m, ((0, pad_m), (0, 0)))
    kmr = jnp.pad(krb_m, ((0, pad_m), (0, 0)))
    kmct = jnp.pad(ckvt_m, ((0, 0), (0, pad_m)))
    ya_p = _attn_prompt(qc_p, qr_p, ckvb_p.reshape(bp, seq, d_kv), ckvt_p,
                        krb_p.reshape(bp, seq, MLA_ROPE_DIM), kmc, kmct, kmr, w_uvt_b, g_mla)
    to_seq = lambda q: q.reshape(MLA_HEADS, db, ds, q.shape[-1]).transpose(1, 0, 2, 3).reshape(db, MLA_HEADS * ds, q.shape[-1])
    ya_s = _attn_sample(page_table, to_seq(qc_s), to_seq(qr_s), ckv_s.reshape(db, ds, d_kv),
                        kr_s.reshape(db, ds, MLA_ROPE_DIM), cache_kv_latent[l],
                        jnp.swapaxes(cache_k_rope[l], 1, 2), w_uv_b, g_mla)
    ya_s = ya_s.reshape(db * ds, -1)

    outs = []
    for x, ys, ya, t_rows, nm in ((x1[0], ys_p, ya_p, t_p, "prompt"), (x1[1], ys_s, ya_s, 0, "sample")):
        x2 = _mix_out(ys, ya, x, w_glu_b, row(b_glu), row(g_ssm_out), w_o_b, row(g_mix_post), t_rows,
                      "mix_out_" + nm)
        y, _ = _ffn(x2, row(g_ff2_pre), ff2_w, row(g_ff2_post), row(g_ff2_post), False, "ffn2_" + nm)
        outs.append(y)

    y_prompt = outs[0].reshape(bp, seq, d_model)
    y_sample = outs[1].reshape(db, ds, d_model)
    meta_b = lambda v: jnp.broadcast_to(v[None], (bp,) + v.shape)
    new_ckv_p = jnp.concatenate([meta_b(ckv_m), ckv_p.reshape(bp, seq, d_kv)], axis=1)[None]
    new_kr_p = jnp.concatenate([meta_b(kr_m), kr_p.reshape(bp, seq, MLA_ROPE_DIM)], axis=1)[None]
    return (y_prompt, y_sample, new_ckv_p, new_kr_p,
            hl_p[None, :, :, :n_state], hl_p[None, :, :, n_state:],
            ckv_s.reshape(1, db, ds, d_kv), kr_s.reshape(1, db, ds, MLA_ROPE_DIM),
            hl_s[None, :, :, :n_state], hl_s[None, :, :, n_state:])
```

```python
import functools
import math

import numpy as np
import jax
import jax.numpy as jnp
from jax import lax
from jax.experimental import pallas as pl
from jax.experimental.pallas import tpu as pltpu

F32 = jnp.float32
BF16 = jnp.bfloat16

N_META = 16
SSM_GROUP = 16
MLA_HEADS = 8
MLA_NOPE_DIM = 64
MLA_ROPE_DIM = 32
ROPE_HALF = MLA_ROPE_DIM // 2
ROPE_THETA = 10000.0
RMS_EPS = 1e-6
ATTN_SCALE = (MLA_NOPE_DIM + MLA_ROPE_DIM) ** -0.5

LANES = 128
SUBLANES = 8
VMEM_LIMIT = 56 * 1024 * 1024

ROW_TILE = 512
FFN_ROW_TILE = 1024
FF_TILE = 256
ATTN_BQ = 256
ATTN_BK = 512
ATTN_COLS = 512
PAGES_PER_STEP = 64
SAMPLE_PARTS = 2
SSM_CHUNK = 16


def _rms(x, g):
    return (x * lax.rsqrt(jnp.mean(x * x, axis=-1, keepdims=True) + RMS_EPS)) * g


def _row_tile(n):
    return ROW_TILE if n % ROW_TILE == 0 else n


def _cparams(sem):
    return pltpu.CompilerParams(dimension_semantics=sem, vmem_limit_bytes=VMEM_LIMIT)


def _ffn_body(emit_norm, x_ref, gpre_ref, wg_ref, wu_ref, wd_ref, gpost_ref, gnext_ref, *refs):
    if emit_norm:
        y_ref, hn_ref, h_scr, acc_scr = refs
    else:
        y_ref, h_scr, acc_scr = refs
        hn_ref = None
    j = pl.program_id(1)

    @pl.when(j == 0)
    def _():
        h_scr[...] = _rms(x_ref[...], gpre_ref[...]).astype(BF16)
        acc_scr[...] = jnp.zeros_like(acc_scr)

    h = h_scr[...]
    g = jnp.dot(h, wg_ref[...], preferred_element_type=F32)
    u = jnp.dot(h, wu_ref[...], preferred_element_type=F32)
    a = (g * jax.nn.sigmoid(g)) * u
    acc_scr[...] += jnp.dot(a.astype(BF16), wd_ref[...], preferred_element_type=F32)

    @pl.when(j == pl.num_programs(1) - 1)
    def _():
        y = x_ref[...] + 0.5 * _rms(acc_scr[...], gpost_ref[...])
        y_ref[...] = y
        if emit_norm:
            hn_ref[...] = _rms(y, gnext_ref[...]).astype(BF16)


def _ffn_weights(wg, wu, wd):
    d, d_ff = wg.shape
    tf = FF_TILE
    assert d_ff % tf == 0
    chunk = lambda w: w.astype(BF16).reshape(d, d_ff // tf, tf).transpose(1, 0, 2)
    return chunk(wg), chunk(wu), wd.astype(BF16).reshape(d_ff // tf, tf, d)


def _ffn(x, g_pre, weights, g_post, g_next, emit_norm, name):
    wg, wu, wd = weights
    n, d = x.shape
    nj, _, tf = wg.shape
    tm = FFN_ROW_TILE if n % FFN_ROW_TILE == 0 else n
    row = lambda i, j: (i, 0)
    vec = lambda i, j: (0, 0)
    chunk = lambda i, j: (j, 0, 0)
    out_shape = [jax.ShapeDtypeStruct((n, d), F32)]
    out_specs = [pl.BlockSpec((tm, d), row)]
    if emit_norm:
        out_shape.append(jax.ShapeDtypeStruct((n, d), BF16))
        out_specs.append(pl.BlockSpec((tm, d), row))
    res = pl.pallas_call(
        functools.partial(_ffn_body, emit_norm),
        grid=(n // tm, nj),
        in_specs=[
            pl.BlockSpec((tm, d), row),
            pl.BlockSpec((1, d), vec),
            pl.BlockSpec((None, d, tf), chunk),
            pl.BlockSpec((None, d, tf), chunk),
            pl.BlockSpec((None, tf, d), chunk),
            pl.BlockSpec((1, d), vec),
            pl.BlockSpec((1, d), vec),
        ],
        out_specs=out_specs,
        out_shape=out_shape,
        scratch_shapes=[pltpu.VMEM((tm, d), BF16), pltpu.VMEM((tm, d), F32)],
        compiler_params=_cparams(("parallel", "arbitrary")),
        name=name,
    )(x, g_pre, wg, wu, wd, g_post, g_next)
    return res if emit_norm else (res[0], None)


def _proj_body(d_u, d_q, d_kv, t_chunk, h_ref, win_ref, gq_ref, wuq_ref, wukt_ref, gkv_ref, wuvt_ref,
               cos_ref, sin_ref,
               u_ref, ckv_ref, ckvb_ref, vt_ref, kr_ref, krb_ref, qc_ref, qr_ref, *maybe_urows):
    proj = jnp.dot(h_ref[...], win_ref[...], preferred_element_type=F32)
    off_q, off_kv, off_kr = d_u, d_u + d_q, d_u + d_q + d_kv
    u_ref[...] = proj[:, :off_q]
    if t_chunk:
        urows_ref, slab_scr = maybe_urows
        n_rows = u_ref.shape[0] // t_chunk
        per_slab = LANES // SSM_GROUP
        for k in range(d_u // LANES):
            slab_scr[k] = proj[:, k * LANES:(k + 1) * LANES]
        for k in range(d_u // LANES):
            per_step = [slab_scr[k, pl.ds(s, n_rows, stride=t_chunk), :] for s in range(t_chunk)]
            for gg in range(per_slab):
                sl = slice(gg * SSM_GROUP, (gg + 1) * SSM_GROUP)
                urows_ref[k * per_slab + gg] = jnp.concatenate([x[:, sl] for x in per_step], axis=-1)
    cq = _rms(proj[:, off_q:off_kv], gq_ref[...]).astype(BF16)
    q = jnp.dot(cq, wuq_ref[...], preferred_element_type=F32) * ATTN_SCALE
    n_nope = MLA_HEADS * MLA_NOPE_DIM
    for hd in range(MLA_HEADS):
        qn = q[:, hd * MLA_NOPE_DIM:(hd + 1) * MLA_NOPE_DIM].astype(BF16)
        qc_ref[hd] = jnp.dot(qn, wukt_ref[hd], preferred_element_type=F32).astype(BF16)
    cos = cos_ref[...]
    sin = sin_ref[...]
    r1 = q[:, n_nope:n_nope + LANES]
    r2 = q[:, n_nope + LANES:n_nope + 2 * LANES]
    o1 = r1 * cos - r2 * sin
    o2 = r2 * cos + r1 * sin
    for hd in range(MLA_HEADS):
        sl = slice(hd * ROPE_HALF, (hd + 1) * ROPE_HALF)
        qr_ref[hd] = jnp.concatenate([o1[:, sl], o2[:, sl]], axis=-1).astype(BF16)
    ckv = _rms(proj[:, off_kv:off_kr], gkv_ref[...])
    ckv_ref[...] = ckv
    ckvb = ckv.astype(BF16)
    ckvb_ref[...] = ckvb
    vt_ref[...] = lax.dot_general(wuvt_ref[...], ckvb, (((1,), (1,)), ((), ())),
                                  preferred_element_type=F32).astype(BF16)
    x1 = proj[:, off_kr:off_kr + ROPE_HALF]
    x2 = proj[:, off_kr + ROPE_HALF:off_kr + MLA_ROPE_DIM]
    c16 = cos[:, :ROPE_HALF]
    s16 = sin[:, :ROPE_HALF]
    kr = jnp.concatenate([x1 * c16 - x2 * s16, x2 * c16 + x1 * s16], axis=-1)
    kr_ref[...] = kr
    krb_ref[...] = kr.astype(BF16)


def _proj(hn, w_in, g_q, w_uq, w_ukt, g_kv, w_uvt, cos, sin, dims, t_chunk, name):
    n, d = hn.shape
    d_u, d_q, d_kv = dims
    tm = _row_tile(n)
    row = lambda i: (i, 0)
    full2 = lambda i: (0, 0)
    full3 = lambda i: (0, 0, 0)
    hrow = lambda i: (0, i, 0)
    extra_specs, extra_shapes, scratch = [], [], []
    if t_chunk:
        assert tm % (t_chunk * SUBLANES) == 0 and d_u % LANES == 0
        n_groups = d_u // SSM_GROUP
        extra_specs = [pl.BlockSpec((n_groups, tm // t_chunk, t_chunk * SSM_GROUP), hrow)]
        extra_shapes = [jax.ShapeDtypeStruct((n_groups, n // t_chunk, t_chunk * SSM_GROUP), F32)]
        scratch = [pltpu.VMEM((d_u // LANES, tm, LANES), F32)]
    return pl.pallas_call(
        functools.partial(_proj_body, d_u, d_q, d_kv, t_chunk),
        grid=(n // tm,),
        in_specs=[
            pl.BlockSpec((tm, d), row),
            pl.BlockSpec(w_in.shape, full2),
            pl.BlockSpec(g_q.shape, full2),
            pl.BlockSpec(w_uq.shape, full2),
            pl.BlockSpec(w_ukt.shape, full3),
            pl.BlockSpec(g_kv.shape, full2),
            pl.BlockSpec(w_uvt.shape, full2),
            pl.BlockSpec((tm, LANES), row),
            pl.BlockSpec((tm, LANES), row),
        ],
        out_specs=[
            pl.BlockSpec((tm, d_u), row),
            pl.BlockSpec((tm, d_kv), row),
            pl.BlockSpec((tm, d_kv), row),
            pl.BlockSpec((w_uvt.shape[0], tm), lambda i: (0, i)),
            pl.BlockSpec((tm, MLA_ROPE_DIM), row),
            pl.BlockSpec((tm, MLA_ROPE_DIM), row),
            pl.BlockSpec((MLA_HEADS, tm, d_kv), hrow),
            pl.BlockSpec((MLA_HEADS, tm, MLA_ROPE_DIM), hrow),
        ] + extra_specs,
        out_shape=[
            jax.ShapeDtypeStruct((n, d_u), F32),
            jax.ShapeDtypeStruct((n, d_kv), F32),
            jax.ShapeDtypeStruct((n, d_kv), BF16),
            jax.ShapeDtypeStruct((w_uvt.shape[0], n), BF16),
            jax.ShapeDtypeStruct((n, MLA_ROPE_DIM), F32),
            jax.ShapeDtypeStruct((n, MLA_ROPE_DIM), BF16),
            jax.ShapeDtypeStruct((MLA_HEADS, n, d_kv), BF16),
            jax.ShapeDtypeStruct((MLA_HEADS, n, MLA_ROPE_DIM), BF16),
        ] + extra_shapes,
        scratch_shapes=scratch,
        compiler_params=_cparams(("parallel",)),
        name=name,
    )(hn, w_in, g_q, w_uq, w_ukt, g_kv, w_uvt, cos, sin)


def _scores(qc, qr, kc, kr):
    nt = (((1,), (1,)), ((), ()))
    return (lax.dot_general(qc, kc, nt, preferred_element_type=F32)
            + lax.dot_general(qr, kr, nt, preferred_element_type=F32))


def _softmax_step(s, kc, m_scr, l_scr, acc_scr):
    m_old = m_scr[...]
    m_new = jnp.maximum(m_old, jnp.max(s, axis=-1, keepdims=True))
    alpha = jnp.exp(m_old - m_new)
    p = jnp.exp(s - m_new)
    l_scr[...] = alpha * l_scr[...] + jnp.sum(p, axis=-1, keepdims=True)
    acc_scr[...] = alpha * acc_scr[...] + jnp.dot(p.astype(BF16), kc, preferred_element_type=F32)
    m_scr[...] = m_new


def _attn_finish(rows_per_head, wuv_ref, g_ref, l_scr, acc_scr):
    o = (acc_scr[...] / l_scr[...]).astype(BF16)
    outs = []
    for hd in range(MLA_HEADS):
        oh = o[hd * rows_per_head:(hd + 1) * rows_per_head]
        outs.append(jnp.dot(oh, wuv_ref[hd], preferred_element_type=F32))
    return _rms(jnp.concatenate(outs, axis=-1), g_ref[...])


def _attn_cols_step(parts, m_blk, cols, heads, bq, m_scr, l_scr, acc_scr):
    m_old = m_scr[:, cols]
    m_new = jnp.maximum(m_old, m_blk)
    alpha = jnp.exp(m_old - m_new)
    ps = [jnp.exp(s - m_new) for s, _ in parts]
    l_new = alpha * l_scr[:, cols]
    for p in ps:
        l_new = l_new + jnp.sum(p, axis=0, keepdims=True)
    l_scr[:, cols] = l_new
    m_scr[:, cols] = m_new
    v_dim = acc_scr.shape[0] // MLA_HEADS
    for n, hd in enumerate(heads):
        hc = slice(n * bq, (n + 1) * bq)
        rows = slice(hd * v_dim, (hd + 1) * v_dim)
        acc = alpha[:, hc] * acc_scr[rows, :]
        for p, (_, vt) in zip(ps, parts):
            acc = acc + jnp.dot(vt[rows, :], p[:, hc].astype(BF16), preferred_element_type=F32)
        acc_scr[rows, :] = acc


def _attn_prompt_body(bq, bk, ncol, qc_ref, qr_ref, kc_ref, vt_ref, kr_ref, kmc_ref, vmt_ref, kmr_ref,
                      g_ref, o_ref, m_scr, l_scr, acc_scr, s_scr, mb_scr):
    i = pl.program_id(1)
    nt = (((1,), (1,)), ((), ()))
    heads_per = ncol // bq
    n_groups = MLA_HEADS // heads_per
    groups = [slice(gi * ncol, (gi + 1) * ncol) for gi in range(n_groups)]

    def scores(gi, kc, kr):
        hs = slice(gi * heads_per, (gi + 1) * heads_per)
        qc = qc_ref[hs].reshape(ncol, qc_ref.shape[-1])
        qr = qr_ref[hs].reshape(ncol, qr_ref.shape[-1])
        return (lax.dot_general(kc, qc, nt, preferred_element_type=F32)
                + lax.dot_general(kr, qr, nt, preferred_element_type=F32))

    def produce(j, slot):
        start = pl.multiple_of(j * bk, bk)
        kc = kc_ref[0, pl.ds(start, bk), :]
        kr = kr_ref[0, pl.ds(start, bk), :]
        for gi, cols in enumerate(groups):
            s = scores(gi, kc, kr)
            s_scr[slot, :, cols] = s
            mb_scr[slot, :, cols] = jnp.max(s, axis=0, keepdims=True)

    def consume(j, slot, diagonal):
        start = pl.multiple_of(j * bk, bk)
        vt = vt_ref[:, pl.ds(start, bk)]
        for gi, cols in enumerate(groups):
            s = s_scr[slot, :, cols]
            if diagonal:
                k_pos = start + lax.broadcasted_iota(jnp.int32, s.shape, 0)
                q_pos = i * bq + (lax.broadcasted_iota(jnp.int32, s.shape, 1) & (bq - 1))
                s = jnp.where(k_pos <= q_pos, s, -jnp.inf)
                sm = scores(gi, kmc_ref[...], kmr_ref[...])
                sm = jnp.where(lax.broadcasted_iota(jnp.int32, sm.shape, 0) < N_META, sm, -jnp.inf)
                m_blk = jnp.maximum(jnp.max(s, axis=0, keepdims=True), jnp.max(sm, axis=0, keepdims=True))
                parts = [(s, vt), (sm, vmt_ref[...])]
            else:
                m_blk = mb_scr[slot, :, cols]
                parts = [(s, vt)]
            heads = range(gi * heads_per, (gi + 1) * heads_per)
            _attn_cols_step(parts, m_blk, cols, heads, bq, m_scr, l_scr, acc_scr)

    n_full = (i * bq) // bk
    m_scr[...] = jnp.full_like(m_scr, -jnp.inf)
    l_scr[...] = jnp.zeros_like(l_scr)
    acc_scr[...] = jnp.zeros_like(acc_scr)
    produce(0, 0)

    def pair(k, carry):
        j = 2 * k
        produce(j + 1, 1)
        consume(j, 0, False)
        produce(j + 2, 0)
        consume(j + 1, 1, False)
        return carry

    lax.fori_loop(0, n_full // 2, pair, 0)
    odd = (n_full & 1) == 1

    @pl.when(odd)
    def _():
        produce(n_full, 1)
        consume(n_full - 1, 0, False)
        consume(n_full, 1, True)

    @pl.when(jnp.logical_not(odd))
    def _():
        consume(n_full, 0, True)

    v_dim = acc_scr.shape[0] // MLA_HEADS
    outs = [acc_scr[hd * v_dim:(hd + 1) * v_dim, :] / l_scr[:, hd * bq:(hd + 1) * bq] for hd in range(MLA_HEADS)]
    y = jnp.transpose(jnp.concatenate(outs, axis=0))
    o_ref[...] = _rms(y, g_ref[...]).astype(o_ref.dtype)


def _attn_prompt(qc, qr, kc, vt, kr, kmc, vmt, kmr, g_mla):
    nb, seq, d_kv = kc.shape
    bq, bk, ncol = ATTN_BQ, ATTN_BK, ATTN_COLS
    assert seq % bk == 0 and bk % bq == 0 and bq & (bq - 1) == 0 and N_META >= 1
    assert ncol % bq == 0 and (MLA_HEADS * bq) % ncol == 0
    nq = seq // bq
    rows = MLA_HEADS * bq
    d_out = vt.shape[0]
    qmap = lambda b, i: (0, b * nq + i, 0)
    kmap = lambda b, i: (b, 0, 0)
    c2 = lambda b, i: (0, 0)
    return pl.pallas_call(
        functools.partial(_attn_prompt_body, bq, bk, ncol),
        grid=(nb, nq),
        in_specs=[
            pl.BlockSpec((MLA_HEADS, bq, d_kv), qmap),
            pl.BlockSpec((MLA_HEADS, bq, MLA_ROPE_DIM), qmap),
            pl.BlockSpec((1, seq, d_kv), kmap),
            pl.BlockSpec((d_out, seq), lambda b, i: (0, b)),
            pl.BlockSpec((1, seq, MLA_ROPE_DIM), kmap),
            pl.BlockSpec(kmc.shape, c2),
            pl.BlockSpec(vmt.shape, c2),
            pl.BlockSpec(kmr.shape, c2),
            pl.BlockSpec(g_mla.shape, c2),
        ],
        out_specs=pl.BlockSpec((bq, d_out), lambda b, i: (b * nq + i, 0)),
        out_shape=jax.ShapeDtypeStruct((nb * seq, d_out), BF16),
        scratch_shapes=[pltpu.VMEM((1, rows), F32), pltpu.VMEM((1, rows), F32),
                        pltpu.VMEM((d_out, bq), F32),
                        pltpu.VMEM((2, bk, rows), F32), pltpu.VMEM((2, 1, rows), F32)],
        compiler_params=_cparams(("parallel", "arbitrary")),
        name="attn_prompt",
    )(qc, qr, kc, vt, kr, kmc, vmt, kmr, g_mla)


def _attn_sample_body(npg, page, ds, n_part, pt_ref, qc_ref, qr_ref, cn_ref, rn_ref, wuv_ref, g_ref,
                      cache_c, cache_r, o_ref, pc_buf, pr_buf, sems, kc_scr, krt_scr, s_scr, mb_scr,
                      m_scr, l_scr, acc_scr):
    b = pl.program_id(0)
    j = pl.program_id(1)
    n_steps = pl.num_programs(1)
    step = b * n_steps + j
    slot = lax.rem(step, 2)
    qc = qc_ref[0]
    qr = qr_ref[0]
    nt = (((1,), (1,)), ((), ()))
    per = npg // n_part

    def page_copies(bb, jj, sl):
        out = []
        for pg in range(npg):
            idx = pt_ref[bb, jj * npg + pg]
            out.append(pltpu.make_async_copy(cache_c.at[idx], pc_buf.at[sl, pg], sems.at[0, sl]))
            out.append(pltpu.make_async_copy(cache_r.at[idx], pr_buf.at[sl, pg], sems.at[1, sl]))
        return out

    @pl.when(step == 0)
    def _():
        for cp in page_copies(0, 0, 0):
            cp.start()

    nxt = step + 1

    @pl.when(nxt < pl.num_programs(0) * n_steps)
    def _():
        for cp in page_copies(nxt // n_steps, lax.rem(nxt, n_steps), 1 - slot):
            cp.start()

    for cp in page_copies(b, j, slot):
        cp.wait()
    pc_refs = [pc_buf.at[slot, pg] for pg in range(npg)]
    pr_refs = [pr_buf.at[slot, pg] for pg in range(npg)]

    @pl.when(j == 0)
    def _():
        m_scr[...] = jnp.full_like(m_scr, -jnp.inf)
        l_scr[...] = jnp.zeros_like(l_scr)
        acc_scr[...] = jnp.zeros_like(acc_scr)

    def keys(part):
        return slice(part * per * page, (part + 1) * per * page)

    def produce(part):
        for pg in range(part * per, (part + 1) * per):
            kc_scr[pg * page:(pg + 1) * page, :] = pc_refs[pg][...].astype(BF16)
            krt_scr[:, pg * page:(pg + 1) * page] = pr_refs[pg][...].astype(BF16)
        s = (lax.dot_general(qc, kc_scr[keys(part), :], nt, preferred_element_type=F32)
             + jnp.dot(qr, krt_scr[:, keys(part)], preferred_element_type=F32))
        s_scr[part] = s
        mb_scr[part] = jnp.max(s, axis=-1, keepdims=True)

    def consume(part):
        s = s_scr[part]
        m_old = m_scr[...]
        m_new = jnp.maximum(m_old, mb_scr[part])
        alpha = jnp.exp(m_old - m_new)
        p = jnp.exp(s - m_new)
        l_scr[...] = alpha * l_scr[...] + jnp.sum(p, axis=-1, keepdims=True)
        acc_scr[...] = alpha * acc_scr[...] + jnp.dot(p.astype(BF16), kc_scr[keys(part), :],
                                                      preferred_element_type=F32)
        m_scr[...] = m_new

    produce(0)
    for part in range(n_part):
        if part + 1 < n_part:
            produce(part + 1)
        consume(part)

    @pl.when(j == pl.num_programs(1) - 1)
    def _():
        pad = LANES - ds
        kn = jnp.concatenate([cn_ref[0], jnp.zeros((pad, cn_ref.shape[-1]), F32)], axis=0).astype(BF16)
        rn = jnp.concatenate([rn_ref[0], jnp.zeros((pad, rn_ref.shape[-1]), F32)], axis=0).astype(BF16)
        s = _scores(qc, qr, kn, rn)
        t_q = lax.broadcasted_iota(jnp.int32, s.shape, 0) & (ds - 1)
        t_k = lax.broadcasted_iota(jnp.int32, s.shape, 1)
        s = jnp.where(t_k <= t_q, s, -jnp.inf)
        _softmax_step(s, kn, m_scr, l_scr, acc_scr)
        o_ref[0] = _attn_finish(ds, wuv_ref, g_ref, l_scr, acc_scr)


def _attn_sample(page_table, qc, qr, c_new, r_new, cache_c, cache_r, w_uv, g_mla):
    db, rows, d_kv = qc.shape
    ds = c_new.shape[1]
    n_pages = page_table.shape[1]
    page = cache_c.shape[1]
    npg = math.gcd(PAGES_PER_STEP, n_pages)
    n_part = math.gcd(SAMPLE_PARTS, npg)
    assert ds & (ds - 1) == 0 and ds <= LANES
    assert cache_r.shape[1:] == (MLA_ROPE_DIM, page)
    d_out = w_uv.shape[0] * w_uv.shape[2]
    bmap = lambda b, j, pt: (b, 0, 0)
    c2 = lambda b, j, pt: (0, 0)
    c3 = lambda b, j, pt: (0, 0, 0)
    in_specs = [
        pl.BlockSpec((1, rows, d_kv), bmap),
        pl.BlockSpec((1, rows, MLA_ROPE_DIM), bmap),
        pl.BlockSpec((1, ds, d_kv), bmap),
        pl.BlockSpec((1, ds, MLA_ROPE_DIM), bmap),
        pl.BlockSpec(w_uv.shape, c3),
        pl.BlockSpec(g_mla.shape, c2),
        pl.BlockSpec(memory_space=pl.ANY),
        pl.BlockSpec(memory_space=pl.ANY),
    ]
    part_keys = (npg // n_part) * page
    grid_spec = pltpu.PrefetchScalarGridSpec(
        num_scalar_prefetch=1,
        grid=(db, n_pages // npg),
        in_specs=in_specs,
        out_specs=pl.BlockSpec((1, ds, d_out), bmap),
        scratch_shapes=[pltpu.VMEM((2, npg, page, d_kv), F32), pltpu.VMEM((2, npg, MLA_ROPE_DIM, page), F32),
                        pltpu.SemaphoreType.DMA((2, 2)),
                        pltpu.VMEM((npg * page, d_kv), BF16), pltpu.VMEM((MLA_ROPE_DIM, npg * page), BF16),
                        pltpu.VMEM((n_part, rows, part_keys), F32), pltpu.VMEM((n_part, rows, 1), F32),
                        pltpu.VMEM((rows, 1), F32), pltpu.VMEM((rows, 1), F32), pltpu.VMEM((rows, d_kv), F32)],
    )
    return pl.pallas_call(
        functools.partial(_attn_sample_body, npg, page, ds, n_part),
        grid_spec=grid_spec,
        out_shape=jax.ShapeDtypeStruct((db, ds, d_out), F32),
        compiler_params=_cparams(("arbitrary", "arbitrary")),
        name="attn_sample",
    )(page_table, qc, qr, c_new, r_new, w_uv, g_mla, cache_c, cache_r)


def _cmul_add(cur, sh, a_r, a_i, half):
    return cur + a_r * sh + a_i * pltpu.roll(sh, half, axis=1)


def _ssm_prompt_body(nb, n_chunks, n_levels, pre, u_ref, um_ref, wy_ref, wd_ref, wc_ref, dv_ref, ar_ref, ai_ref,
                     y_ref, hl_ref, scr, e_scr):
    u = u_ref[0]
    ub = u.astype(BF16)
    half = wd_ref.shape[-1] // 2
    wd = wd_ref[0]
    d = jnp.dot(ub, wd, preferred_element_type=F32)
    h_meta = jnp.dot(um_ref[0].astype(BF16), wd, preferred_element_type=F32)[0:1]
    first = lax.broadcasted_iota(jnp.int32, (n_chunks, 1), 0) == 0
    for b in range(nb):
        scr[b, 0:pre, :] = jnp.zeros((pre, scr.shape[-1]), F32)
        scr[b, pre:pre + n_chunks, :] = d[b * n_chunks:(b + 1) * n_chunks]
    for b in range(nb):
        shifted = scr[b, pre - 1:pre - 1 + n_chunks, :]
        scr[b, pre:pre + n_chunks, :] = shifted + jnp.where(first, h_meta, 0.0)
    for k in range(n_levels):
        s = 1 << k
        for b in range(nb):
            cur = scr[b, pre:pre + n_chunks, :]
            sh = scr[b, pre - s:pre - s + n_chunks, :]
            scr[b, pre:pre + n_chunks, :] = _cmul_add(cur, sh, ar_ref[0, k:k + 1, :], ai_ref[0, k:k + 1, :], half)
    for b in range(nb):
        e_scr[b * n_chunks:(b + 1) * n_chunks, :] = scr[b, pre:pre + n_chunks, :]
    e = e_scr[...]
    y_ref[0] = (jnp.dot(ub, wy_ref[0], preferred_element_type=F32)
                + jnp.dot(e.astype(BF16), wc_ref[0], preferred_element_type=F32)
                + u * dv_ref[0])
    h_after = _cmul_add(d, e, ar_ref[0, 0:1, :], ai_ref[0, 0:1, :], half)
    for b in range(nb):
        last = (b + 1) * n_chunks - 1
        hl_ref[0, b:b + 1, :] = h_after[last:last + 1]


def _ssm_sample_body(u_ref, h_ref, wy_ref, wd_ref, wc_ref, dv_ref, ar_ref, ai_ref, y_ref, hl_ref):
    u = u_ref[0]
    ub = u.astype(BF16)
    e = h_ref[0]
    half = e.shape[-1] // 2
    d = jnp.dot(ub, wd_ref[0], preferred_element_type=F32)
    y_ref[0] = (jnp.dot(ub, wy_ref[0], preferred_element_type=F32)
                + jnp.dot(e.astype(BF16), wc_ref[0], preferred_element_type=F32)
                + u * dv_ref[0])
    hl_ref[0] = _cmul_add(d, e, ar_ref[0, 0:1, :], ai_ref[0, 0:1, :], half)


def _ssm_weights(a_re, a_im, log_dt, b_re, b_im, c_re, c_im, d_skip, t_chunk, n_levels):
    hi = lax.Precision.HIGHEST
    a = lax.complex(a_re.astype(F32), a_im.astype(F32))
    dt = jnp.exp(log_dt.astype(F32))[:, None]
    a_dt = a * dt
    a_bar = jnp.exp(a_dt)
    b_bar = ((a_bar - 1.0) / a)[..., None] * lax.complex(b_re.astype(F32), b_im.astype(F32))
    c = lax.complex(c_re.astype(F32), c_im.astype(F32))
    g, p_state, ch = b_bar.shape
    k = jnp.arange(t_chunk + 1, dtype=F32)
    a_pow = jnp.exp(a_dt[None] * k[:, None, None])
    kern = jnp.einsum('gcp,kgp,gpd->kgcd', c, a_pow[:t_chunk], b_bar, precision=hi).real
    s_idx = jnp.arange(t_chunk)[:, None]
    t_idx = jnp.arange(t_chunk)[None, :]
    lag = t_idx - s_idx
    wy = jnp.where((lag >= 0)[:, :, None, None, None], kern[jnp.clip(lag, 0, t_chunk - 1)], 0.0)
    wy = wy.transpose(2, 0, 4, 1, 3).reshape(g, t_chunk * ch, t_chunk * ch)
    wd = a_pow[:t_chunk][::-1][:, :, :, None] * b_bar[None]
    wd = wd.transpose(1, 0, 3, 2).reshape(g, t_chunk * ch, p_state)
    wd = jnp.concatenate([wd.real, wd.imag], axis=-1)
    gm = c[None] * a_pow[1:][:, :, None, :]
    gm = gm.transpose(1, 3, 0, 2).reshape(g, p_state, t_chunk * ch)
    wc = jnp.concatenate([gm.real, -gm.imag], axis=1)
    dv = jnp.tile(d_skip.astype(F32).reshape(g, 1, ch), (1, 1, t_chunk))
    lev = (t_chunk * (2.0 ** jnp.arange(n_levels, dtype=F32)))
    a_lev = jnp.exp(a_dt[:, None, :] * lev[None, :, None])
    a_r = jnp.concatenate([a_lev.real, a_lev.real], axis=-1)
    a_i = jnp.concatenate([-a_lev.imag, a_lev.imag], axis=-1)
    return wy.astype(BF16), wd.astype(BF16), wc.astype(BF16), dv, a_r, a_i


def _ssm_prompt(u_rows, um_rows, nb, ops):
    wy, wd, wc, dv, a_r, a_i = ops
    g, r, tc = u_rows.shape
    n_chunks = r // nb
    n_levels = a_r.shape[1]
    assert (1 << n_levels) >= n_chunks and n_chunks % SUBLANES == 0
    st = wd.shape[-1]
    pre = -(-(1 << (n_levels - 1)) // SUBLANES) * SUBLANES
    gmap = lambda i: (i, 0, 0)
    return pl.pallas_call(
        functools.partial(_ssm_prompt_body, nb, n_chunks, n_levels, pre),
        grid=(g,),
        in_specs=[pl.BlockSpec((1, r, tc), gmap), pl.BlockSpec((1,) + um_rows.shape[1:], gmap),
                  pl.BlockSpec((1,) + wy.shape[1:], gmap),
                  pl.BlockSpec((1,) + wd.shape[1:], gmap), pl.BlockSpec((1,) + wc.shape[1:], gmap),
                  pl.BlockSpec((1,) + dv.shape[1:], gmap), pl.BlockSpec((1,) + a_r.shape[1:], gmap),
                  pl.BlockSpec((1,) + a_i.shape[1:], gmap)],
        out_specs=[pl.BlockSpec((1, r, tc), gmap), pl.BlockSpec((1, nb, st), gmap)],
        out_shape=[jax.ShapeDtypeStruct((g, r, tc), F32), jax.ShapeDtypeStruct((g, nb, st), F32)],
        scratch_shapes=[pltpu.VMEM((nb, pre + n_chunks, st), F32), pltpu.VMEM((r, st), F32)],
        compiler_params=_cparams(("parallel",)),
        name="ssm_prompt",
    )(u_rows, um_rows, wy, wd, wc, dv, a_r, a_i)


def _ssm_sample(u_rows, h_rows, ops):
    wy, wd, wc, dv, a_r, a_i = ops
    g, r, tc = u_rows.shape
    st = wd.shape[-1]
    gmap = lambda i: (i, 0, 0)
    return pl.pallas_call(
        _ssm_sample_body,
        grid=(g,),
        in_specs=[pl.BlockSpec((1, r, tc), gmap), pl.BlockSpec((1, r, st), gmap),
                  pl.BlockSpec((1,) + wy.shape[1:], gmap), pl.BlockSpec((1,) + wd.shape[1:], gmap),
                  pl.BlockSpec((1,) + wc.shape[1:], gmap), pl.BlockSpec((1,) + dv.shape[1:], gmap),
                  pl.BlockSpec((1,) + a_r.shape[1:], gmap), pl.BlockSpec((1,) + a_i.shape[1:], gmap)],
        out_specs=[pl.BlockSpec((1, r, tc), gmap), pl.BlockSpec((1, r, st), gmap)],
        out_shape=[jax.ShapeDtypeStruct((g, r, tc), F32), jax.ShapeDtypeStruct((g, r, st), F32)],
        compiler_params=_cparams(("parallel",)),
        name="ssm_sample",
    )(u_rows, h_rows, wy, wd, wc, dv, a_r, a_i)


def _mix_out_body(t_chunk, ys_ref, ya_ref, x_ref, wglu_ref, bglu_ref, gs_ref, wo_ref, gpost_ref, o_ref,
                  *maybe_scr):
    if t_chunk:
        (slab_scr,) = maybe_scr
        n_slabs = slab_scr.shape[0]
        n_rows = slab_scr.shape[1] // t_chunk
        per_slab = LANES // SSM_GROUP
        for k in range(n_slabs):
            for t in range(t_chunk):
                sl = slice(t * SSM_GROUP, (t + 1) * SSM_GROUP)
                slab_scr[k, pl.ds(t, n_rows, stride=t_chunk), :] = jnp.concatenate(
                    [ys_ref[k * per_slab + gg, :, sl] for gg in range(per_slab)], axis=-1)
        ys = jnp.concatenate([slab_scr[k] for k in range(n_slabs)], axis=-1)
    else:
        ys = ys_ref[...]
    z = jax.nn.gelu(ys)
    gate = jax.nn.sigmoid(jnp.dot(z.astype(BF16), wglu_ref[...], preferred_element_type=F32) + bglu_ref[...])
    ns = _rms(z * gate, gs_ref[...]).astype(BF16)
    w = ns.shape[-1]
    y = (jnp.dot(ns, wo_ref[:w, :], preferred_element_type=F32)
         + jnp.dot(ya_ref[...].astype(BF16), wo_ref[w:, :], preferred_element_type=F32))
    o_ref[...] = x_ref[...] + _rms(y, gpost_ref[...])


def _mix_out(ys, ya, x, w_glu, b_glu, g_ssm, w_o, g_post, t_chunk, name):
    n, d = x.shape
    tm = _row_tile(n)
    row = lambda i: (i, 0)
    c2 = lambda i: (0, 0)
    if t_chunk:
        assert tm % (t_chunk * SUBLANES) == 0
        w = ys.shape[0] * SSM_GROUP
        assert w % LANES == 0
        ys_spec = pl.BlockSpec((ys.shape[0], tm // t_chunk, ys.shape[2]), lambda i: (0, i, 0))
        scratch = [pltpu.VMEM((w // LANES, tm, LANES), F32)]
    else:
        ys_spec = pl.BlockSpec((tm, ys.shape[1]), row)
        scratch = []
    return pl.pallas_call(
        functools.partial(_mix_out_body, t_chunk),
        grid=(n // tm,),
        in_specs=[ys_spec, pl.BlockSpec((tm, ya.shape[1]), row), pl.BlockSpec((tm, d), row),
                  pl.BlockSpec(w_glu.shape, c2), pl.BlockSpec(b_glu.shape, c2), pl.BlockSpec(g_ssm.shape, c2),
                  pl.BlockSpec(w_o.shape, c2), pl.BlockSpec(g_post.shape, c2)],
        out_specs=pl.BlockSpec((tm, d), row),
        out_shape=jax.ShapeDtypeStruct((n, d), F32),
        scratch_shapes=scratch,
        compiler_params=_cparams(("parallel",)),
        name=name,
    )(ys, ya, x, w_glu, b_glu, g_ssm, w_o, g_post)


def _rope_tables(pos):
    inv = ROPE_THETA ** (-jnp.arange(ROPE_HALF, dtype=F32) / ROPE_HALF)
    ang = pos.astype(F32)[:, None] * inv[None, :]
    reps = LANES // ROPE_HALF
    return jnp.tile(jnp.cos(ang), (1, reps)), jnp.tile(jnp.sin(ang), (1, reps))


def _uq_column_order():
    per = MLA_NOPE_DIM + MLA_ROPE_DIM
    heads = np.arange(MLA_HEADS)[:, None]
    nope = (heads * per + np.arange(MLA_NOPE_DIM)[None, :]).reshape(-1)
    rope1 = (heads * per + MLA_NOPE_DIM + np.arange(ROPE_HALF)[None, :]).reshape(-1)
    rope2 = (heads * per + MLA_NOPE_DIM + ROPE_HALF + np.arange(ROPE_HALF)[None, :]).reshape(-1)
    return np.concatenate([nope, rope1, rope2])


def kernel(x_prompt, x_sample, cache_kv_latent, cache_k_rope, state_ssm_re, state_ssm_im, page_table, meta_tokens, g_ff1_pre, w_ff1_gate, w_ff1_up, w_ff1_down, g_ff1_post, g_mix_pre, w_in, ssm_a_re, ssm_a_im, ssm_log_dt, ssm_b_re, ssm_b_im, ssm_c_re, ssm_c_im, ssm_d, w_glu, b_glu, g_q_norm, w_uq, g_kv_norm, w_uk, w_uv, g_ssm_out, g_mla_out, w_o, g_mix_post, g_ff2_pre, w_ff2_gate, w_ff2_up, w_ff2_down, g_ff2_post):
    depth = w_in.shape[0]
    assert depth == 1, "single-layer step"
    bp, seq, d_model = x_prompt.shape
    db, ds, _ = x_sample.shape
    n_pages = page_table.shape[1]
    page = cache_kv_latent.shape[2]
    past_len = n_pages * page
    d_kv = cache_kv_latent.shape[3]
    d_q = w_uq.shape[1]
    n_groups, n_state = ssm_a_re.shape[1], ssm_a_re.shape[2]
    d_u = n_groups * SSM_GROUP
    dims = (d_u, d_q, d_kv)
    l = 0
    row = lambda v: v[l].reshape(1, -1).astype(F32)

    ff1_w = _ffn_weights(w_ff1_gate[l], w_ff1_up[l], w_ff1_down[l])
    ff2_w = _ffn_weights(w_ff2_gate[l], w_ff2_up[l], w_ff2_down[l])
    w_in_b = w_in[l].astype(BF16)
    w_uq_b = w_uq[l][:, _uq_column_order()].astype(BF16)
    w_ukt = jnp.transpose(w_uk[l], (1, 2, 0)).astype(BF16)
    w_uv_b = jnp.transpose(w_uv[l], (1, 0, 2)).astype(BF16)
    w_uvt_b = w_uv[l].reshape(d_kv, -1).T.astype(BF16)
    w_glu_b = w_glu[l].astype(BF16)
    w_o_b = w_o[l].astype(BF16)

    xs = [x_prompt.reshape(bp * seq, d_model), x_sample.reshape(db * ds, d_model), meta_tokens.astype(F32)]
    names = ["prompt", "sample", "meta"]
    pos = [N_META + jnp.tile(jnp.arange(seq), bp), past_len + jnp.tile(jnp.arange(ds), db), jnp.arange(N_META)]

    t_p = SSM_CHUNK
    assert N_META == t_p and seq % t_p == 0
    x1, pr = [], []
    for x, nm, ps, t_rows in zip(xs, names, pos, (t_p, 0, 0)):
        y, hn = _ffn(x, row(g_ff1_pre), ff1_w, row(g_ff1_post), row(g_mix_pre), True, "ffn1_" + nm)
        cos, sin = _rope_tables(ps)
        x1.append(y)
        pr.append(_proj(hn, w_in_b, row(g_q_norm), w_uq_b, w_ukt, row(g_kv_norm), w_uvt_b, cos, sin, dims, t_rows,
                        "proj_" + nm))
    (_, ckv_p, ckvb_p, vt_p, kr_p, krb_p, qc_p, qr_p, u_rows) = pr[0]
    (u_s, ckv_s, _, _, kr_s, _, qc_s, qr_s) = pr[1]
    (u_m, ckv_m, ckvb_m, vt_m, kr_m, krb_m, _, _) = pr[2]

    n_levels = max(1, (seq // t_p - 1).bit_length())
    ssm_w = (ssm_a_re[l], ssm_a_im[l], ssm_log_dt[l], ssm_b_re[l], ssm_b_im[l], ssm_c_re[l], ssm_c_im[l], ssm_d[l])
    ops_p = _ssm_weights(*ssm_w, t_p, n_levels)
    ops_s = _ssm_weights(*ssm_w, ds, 1)
    tc = t_p * SSM_GROUP
    um_rows = u_m.reshape(1, t_p, n_groups, SSM_GROUP).transpose(2, 0, 1, 3).reshape(n_groups, 1, tc)
    um_rows = jnp.pad(um_rows, ((0, 0), (0, SUBLANES - 1), (0, 0)))
    ys_p, hl_p = _ssm_prompt(u_rows, um_rows, bp, ops_p)
    hl_p = hl_p.transpose(1, 0, 2)

    us_rows = u_s.reshape(db, ds, n_groups, SSM_GROUP).transpose(2, 0, 1, 3).reshape(n_groups, db, ds * SSM_GROUP)
    h0_rows = jnp.concatenate([state_ssm_re[l], state_ssm_im[l]], axis=-1).astype(F32).transpose(1, 0, 2)
    ysr, hl_s = _ssm_sample(us_rows, h0_rows, ops_s)
    ys_s = ysr.reshape(n_groups, db, ds, SSM_GROUP).transpose(1, 2, 0, 3).reshape(db * ds, d_u)
    hl_s = hl_s.transpose(1, 0, 2)

    g_mla = row(g_mla_out)
    pad_m = LANES - N_META
    kmc = jnp.pad(ckvb_m, ((0, pad_m), (0, 0)))
    kmr = jnp.pad(krb_m, ((0, pad_m), (0, 0)))
    vmt = jnp.pad(vt_m, ((0, 0), (0, pad_m)))
    ya_p = _attn_prompt(qc_p, qr_p, ckvb_p.reshape(bp, seq, d_kv), vt_p,
                        krb_p.reshape(bp, seq, MLA_ROPE_DIM), kmc, vmt, kmr, g_mla)
    to_seq = lambda q: q.reshape(MLA_HEADS, db, ds, q.shape[-1]).transpose(1, 0, 2, 3).reshape(db, MLA_HEADS * ds, q.shape[-1])
    ya_s = _attn_sample(page_table, to_seq(qc_s), to_seq(qr_s), ckv_s.reshape(db, ds, d_kv),
                        kr_s.reshape(db, ds, MLA_ROPE_DIM), cache_kv_latent[l],
                        jnp.swapaxes(cache_k_rope[l], 1, 2), w_uv_b, g_mla)
    ya_s = ya_s.reshape(db * ds, -1)

    outs = []
    for x, ys, ya, t_rows, nm in ((x1[0], ys_p, ya_p, t_p, "prompt"), (x1[1], ys_s, ya_s, 0, "sample")):
        x2 = _mix_out(ys, ya, x, w_glu_b, row(b_glu), row(g_ssm_out), w_o_b, row(g_mix_post), t_rows,
                      "mix_out_" + nm)
        y, _ = _ffn(x2, row(g_ff2_pre), ff2_w, row(g_ff2_post), row(g_ff2_post), False, "ffn2_" + nm)
        outs.append(y)

    y_prompt = outs[0].reshape(bp, seq, d_model)
    y_sample = outs[1].reshape(db, ds, d_model)
    meta_b = lambda v: jnp.broadcast_to(v[None], (bp,) + v.shape)
    new_ckv_p = jnp.concatenate([meta_b(ckv_m), ckv_p.reshape(bp, seq, d_kv)], axis=1)[None]
    new_kr_p = jnp.concatenate([meta_b(kr_m), kr_p.reshape(bp, seq, MLA_ROPE_DIM)], axis=1)[None]
    return (y_prompt, y_sample, new_ckv_p, new_kr_p,
            hl_p[None, :, :, :n_state], hl_p[None, :, :, n_state:],
            ckv_s.reshape(1, db, ds, d_kv), kr_s.reshape(1, db, ds, MLA_ROPE_DIM),
            hl_s[None, :, :, :n_state], hl_s[None, :, :, n_state:])
```

```python
import functools
import math

import numpy as np
import jax
import jax.numpy as jnp
from jax import lax
from jax.experimental import pallas as pl
from jax.experimental.pallas import tpu as pltpu

F32 = jnp.float32
BF16 = jnp.bfloat16

N_META = 16
SSM_GROUP = 16
MLA_HEADS = 8
MLA_NOPE_DIM = 64
MLA_ROPE_DIM = 32
ROPE_HALF = MLA_ROPE_DIM // 2
ROPE_THETA = 10000.0
RMS_EPS = 1e-6
ATTN_SCALE = (MLA_NOPE_DIM + MLA_ROPE_DIM) ** -0.5

LANES = 128
SUBLANES = 8
VMEM_LIMIT = 56 * 1024 * 1024

ROW_TILE = 512
FFN_ROW_TILE = 1024
FF_TILE = 256
ATTN_BQ = 256
ATTN_BK = 512
ATTN_COLS = 512
PAGES_PER_STEP = 64
SAMPLE_PARTS = 2
SAMPLE_RING = 3
SSM_CHUNK = 16


def _rms(x, g):
    return (x * lax.rsqrt(jnp.mean(x * x, axis=-1, keepdims=True) + RMS_EPS)) * g


def _row_tile(n):
    return ROW_TILE if n % ROW_TILE == 0 else n


def _cparams(sem):
    return pltpu.CompilerParams(dimension_semantics=sem, vmem_limit_bytes=VMEM_LIMIT)


def _ffn_body(emit_norm, x_ref, gpre_ref, wg_ref, wu_ref, wd_ref, gpost_ref, gnext_ref, *refs):
    if emit_norm:
        y_ref, hn_ref, h_scr, acc_scr = refs
    else:
        y_ref, h_scr, acc_scr = refs
        hn_ref = None
    j = pl.program_id(1)

    @pl.when(j == 0)
    def _():
        h_scr[...] = _rms(x_ref[...], gpre_ref[...]).astype(BF16)
        acc_scr[...] = jnp.zeros_like(acc_scr)

    h = h_scr[...]
    g = jnp.dot(h, wg_ref[...], preferred_element_type=F32)
    u = jnp.dot(h, wu_ref[...], preferred_element_type=F32)
    a = (g * jax.nn.sigmoid(g)) * u
    acc_scr[...] += jnp.dot(a.astype(BF16), wd_ref[...], preferred_element_type=F32)

    @pl.when(j == pl.num_programs(1) - 1)
    def _():
        y = x_ref[...] + 0.5 * _rms(acc_scr[...], gpost_ref[...])
        y_ref[...] = y
        if emit_norm:
            hn_ref[...] = _rms(y, gnext_ref[...]).astype(BF16)


def _ffn_weights(wg, wu, wd):
    d, d_ff = wg.shape
    tf = FF_TILE
    assert d_ff % tf == 0
    chunk = lambda w: w.astype(BF16).reshape(d, d_ff // tf, tf).transpose(1, 0, 2)
    return chunk(wg), chunk(wu), wd.astype(BF16).reshape(d_ff // tf, tf, d)


def _ffn(x, g_pre, weights, g_post, g_next, emit_norm, name):
    wg, wu, wd = weights
    n, d = x.shape
    nj, _, tf = wg.shape
    tm = FFN_ROW_TILE if n % FFN_ROW_TILE == 0 else n
    row = lambda i, j: (i, 0)
    vec = lambda i, j: (0, 0)
    chunk = lambda i, j: (j, 0, 0)
    out_shape = [jax.ShapeDtypeStruct((n, d), F32)]
    out_specs = [pl.BlockSpec((tm, d), row)]
    if emit_norm:
        out_shape.append(jax.ShapeDtypeStruct((n, d), BF16))
        out_specs.append(pl.BlockSpec((tm, d), row))
    res = pl.pallas_call(
        functools.partial(_ffn_body, emit_norm),
        grid=(n // tm, nj),
        in_specs=[
            pl.BlockSpec((tm, d), row),
            pl.BlockSpec((1, d), vec),
            pl.BlockSpec((None, d, tf), chunk),
            pl.BlockSpec((None, d, tf), chunk),
            pl.BlockSpec((None, tf, d), chunk),
            pl.BlockSpec((1, d), vec),
            pl.BlockSpec((1, d), vec),
        ],
        out_specs=out_specs,
        out_shape=out_shape,
        scratch_shapes=[pltpu.VMEM((tm, d), BF16), pltpu.VMEM((tm, d), F32)],
        compiler_params=_cparams(("parallel", "arbitrary")),
        name=name,
    )(x, g_pre, wg, wu, wd, g_post, g_next)
    return res if emit_norm else (res[0], None)


def _proj_body(d_u, d_q, d_kv, t_chunk, h_ref, win_ref, gq_ref, wuq_ref, wukt_ref, gkv_ref, wuvt_ref,
               cos_ref, sin_ref,
               u_ref, ckv_ref, ckvb_ref, vt_ref, kr_ref, krb_ref, qc_ref, qr_ref, *maybe_urows):
    proj = jnp.dot(h_ref[...], win_ref[...], preferred_element_type=F32)
    off_q, off_kv, off_kr = d_u, d_u + d_q, d_u + d_q + d_kv
    u_ref[...] = proj[:, :off_q]
    if t_chunk:
        urows_ref, slab_scr = maybe_urows
        n_rows = u_ref.shape[0] // t_chunk
        per_slab = LANES // SSM_GROUP
        lane_blk = lax.broadcasted_iota(jnp.int32, (n_rows, LANES), 1) // SSM_GROUP
        for k in range(d_u // LANES):
            slab_scr[k] = proj[:, k * LANES:(k + 1) * LANES]
        for k in range(d_u // LANES):
            rolled = []
            for s in range(t_chunk):
                x = slab_scr[k, pl.ds(s, n_rows, stride=t_chunk), :]
                shift = (s % per_slab) * SSM_GROUP
                rolled.append(pltpu.roll(x, shift, axis=1) if shift else x)
            for gg in range(per_slab):
                tiles = []
                for t in range(t_chunk // per_slab):
                    tile = rolled[t * per_slab]
                    for s1 in range(1, per_slab):
                        tile = jnp.where(lane_blk == (gg + s1) % per_slab, rolled[t * per_slab + s1], tile)
                    tiles.append(tile)
                urows_ref[k * per_slab + gg] = jnp.concatenate(tiles, axis=-1)
    cq = _rms(proj[:, off_q:off_kv], gq_ref[...]).astype(BF16)
    q = jnp.dot(cq, wuq_ref[...], preferred_element_type=F32) * ATTN_SCALE
    n_nope = MLA_HEADS * MLA_NOPE_DIM
    for hd in range(MLA_HEADS):
        qn = q[:, hd * MLA_NOPE_DIM:(hd + 1) * MLA_NOPE_DIM].astype(BF16)
        qc_ref[hd] = jnp.dot(qn, wukt_ref[hd], preferred_element_type=F32).astype(BF16)
    cos = cos_ref[...]
    sin = sin_ref[...]
    r1 = q[:, n_nope:n_nope + LANES]
    r2 = q[:, n_nope + LANES:n_nope + 2 * LANES]
    o1 = r1 * cos - r2 * sin
    o2 = r2 * cos + r1 * sin
    for hd in range(MLA_HEADS):
        sl = slice(hd * ROPE_HALF, (hd + 1) * ROPE_HALF)
        qr_ref[hd] = jnp.concatenate([o1[:, sl], o2[:, sl]], axis=-1).astype(BF16)
    ckv = _rms(proj[:, off_kv:off_kr], gkv_ref[...])
    ckv_ref[...] = ckv
    ckvb = ckv.astype(BF16)
    ckvb_ref[...] = ckvb
    vt_ref[...] = lax.dot_general(wuvt_ref[...], ckvb, (((1,), (1,)), ((), ())),
                                  preferred_element_type=F32).astype(BF16)
    x1 = proj[:, off_kr:off_kr + ROPE_HALF]
    x2 = proj[:, off_kr + ROPE_HALF:off_kr + MLA_ROPE_DIM]
    c16 = cos[:, :ROPE_HALF]
    s16 = sin[:, :ROPE_HALF]
    kr = jnp.concatenate([x1 * c16 - x2 * s16, x2 * c16 + x1 * s16], axis=-1)
    kr_ref[...] = kr
    krb_ref[...] = kr.astype(BF16)


def _proj(hn, w_in, g_q, w_uq, w_ukt, g_kv, w_uvt, cos, sin, dims, t_chunk, name):
    n, d = hn.shape
    d_u, d_q, d_kv = dims
    tm = _row_tile(n)
    row = lambda i: (i, 0)
    full2 = lambda i: (0, 0)
    full3 = lambda i: (0, 0, 0)
    hrow = lambda i: (0, i, 0)
    extra_specs, extra_shapes, scratch = [], [], []
    if t_chunk:
        assert tm % (t_chunk * SUBLANES) == 0 and d_u % LANES == 0
        n_groups = d_u // SSM_GROUP
        extra_specs = [pl.BlockSpec((n_groups, tm // t_chunk, t_chunk * SSM_GROUP), hrow)]
        extra_shapes = [jax.ShapeDtypeStruct((n_groups, n // t_chunk, t_chunk * SSM_GROUP), F32)]
        scratch = [pltpu.VMEM((d_u // LANES, tm, LANES), F32)]
    return pl.pallas_call(
        functools.partial(_proj_body, d_u, d_q, d_kv, t_chunk),
        grid=(n // tm,),
        in_specs=[
            pl.BlockSpec((tm, d), row),
            pl.BlockSpec(w_in.shape, full2),
            pl.BlockSpec(g_q.shape, full2),
            pl.BlockSpec(w_uq.shape, full2),
            pl.BlockSpec(w_ukt.shape, full3),
            pl.BlockSpec(g_kv.shape, full2),
            pl.BlockSpec(w_uvt.shape, full2),
            pl.BlockSpec((tm, LANES), row),
            pl.BlockSpec((tm, LANES), row),
        ],
        out_specs=[
            pl.BlockSpec((tm, d_u), row),
            pl.BlockSpec((tm, d_kv), row),
            pl.BlockSpec((tm, d_kv), row),
            pl.BlockSpec((w_uvt.shape[0], tm), lambda i: (0, i)),
            pl.BlockSpec((tm, MLA_ROPE_DIM), row),
            pl.BlockSpec((tm, MLA_ROPE_DIM), row),
            pl.BlockSpec((MLA_HEADS, tm, d_kv), hrow),
            pl.BlockSpec((MLA_HEADS, tm, MLA_ROPE_DIM), hrow),
        ] + extra_specs,
        out_shape=[
            jax.ShapeDtypeStruct((n, d_u), F32),
            jax.ShapeDtypeStruct((n, d_kv), F32),
            jax.ShapeDtypeStruct((n, d_kv), BF16),
            jax.ShapeDtypeStruct((w_uvt.shape[0], n), BF16),
            jax.ShapeDtypeStruct((n, MLA_ROPE_DIM), F32),
            jax.ShapeDtypeStruct((n, MLA_ROPE_DIM), BF16),
            jax.ShapeDtypeStruct((MLA_HEADS, n, d_kv), BF16),
            jax.ShapeDtypeStruct((MLA_HEADS, n, MLA_ROPE_DIM), BF16),
        ] + extra_shapes,
        scratch_shapes=scratch,
        compiler_params=_cparams(("parallel",)),
        name=name,
    )(hn, w_in, g_q, w_uq, w_ukt, g_kv, w_uvt, cos, sin)


def _scores(qc, qr, kc, kr):
    nt = (((1,), (1,)), ((), ()))
    return (lax.dot_general(qc, kc, nt, preferred_element_type=F32)
            + lax.dot_general(qr, kr, nt, preferred_element_type=F32))


def _softmax_step(s, kc, m_scr, l_scr, acc_scr):
    m_old = m_scr[...]
    m_new = jnp.maximum(m_old, jnp.max(s, axis=-1, keepdims=True))
    alpha = jnp.exp(m_old - m_new)
    p = jnp.exp(s - m_new)
    l_scr[...] = alpha * l_scr[...] + jnp.sum(p, axis=-1, keepdims=True)
    acc_scr[...] = alpha * acc_scr[...] + jnp.dot(p.astype(BF16), kc, preferred_element_type=F32)
    m_scr[...] = m_new


def _attn_finish(rows_per_head, wuv_ref, g_ref, l_scr, acc_scr):
    o = (acc_scr[...] / l_scr[...]).astype(BF16)
    outs = []
    for hd in range(MLA_HEADS):
        oh = o[hd * rows_per_head:(hd + 1) * rows_per_head]
        outs.append(jnp.dot(oh, wuv_ref[hd], preferred_element_type=F32))
    return _rms(jnp.concatenate(outs, axis=-1), g_ref[...])


def _attn_cols_step(parts, m_blk, cols, heads, bq, m_scr, l_scr, acc_scr):
    m_old = m_scr[:, cols]
    m_new = jnp.maximum(m_old, m_blk)
    alpha = jnp.exp(m_old - m_new)
    ps = [jnp.exp(s - m_new) for s, _ in parts]
    l_new = alpha * l_scr[:, cols]
    for p in ps:
        l_new = l_new + jnp.sum(p, axis=0, keepdims=True)
    l_scr[:, cols] = l_new
    m_scr[:, cols] = m_new
    v_dim = acc_scr.shape[0] // MLA_HEADS
    for n, hd in enumerate(heads):
        hc = slice(n * bq, (n + 1) * bq)
        rows = slice(hd * v_dim, (hd + 1) * v_dim)
        acc = alpha[:, hc] * acc_scr[rows, :]
        for p, (_, vt) in zip(ps, parts):
            acc = acc + jnp.dot(vt[rows, :], p[:, hc].astype(BF16), preferred_element_type=F32)
        acc_scr[rows, :] = acc


def _attn_prompt_body(bq, bk, ncol, qc_ref, qr_ref, kc_ref, vt_ref, kr_ref, kmc_ref, vmt_ref, kmr_ref,
                      g_ref, o_ref, m_scr, l_scr, acc_scr, s_scr, mb_scr):
    i = pl.program_id(1)
    nt = (((1,), (1,)), ((), ()))
    heads_per = ncol // bq
    n_groups = MLA_HEADS // heads_per
    groups = [slice(gi * ncol, (gi + 1) * ncol) for gi in range(n_groups)]

    def scores(gi, kc, kr):
        hs = slice(gi * heads_per, (gi + 1) * heads_per)
        qc = qc_ref[hs].reshape(ncol, qc_ref.shape[-1])
        qr = qr_ref[hs].reshape(ncol, qr_ref.shape[-1])
        return (lax.dot_general(kc, qc, nt, preferred_element_type=F32)
                + lax.dot_general(kr, qr, nt, preferred_element_type=F32))

    def produce(j, slot):
        start = pl.multiple_of(j * bk, bk)
        kc = kc_ref[0, pl.ds(start, bk), :]
        kr = kr_ref[0, pl.ds(start, bk), :]
        for gi, cols in enumerate(groups):
            s = scores(gi, kc, kr)
            s_scr[slot, :, cols] = s
            mb_scr[slot, :, cols] = jnp.max(s, axis=0, keepdims=True)

    def consume(j, slot, diagonal):
        start = pl.multiple_of(j * bk, bk)
        vt = vt_ref[:, pl.ds(start, bk)]
        for gi, cols in enumerate(groups):
            s = s_scr[slot, :, cols]
            if diagonal:
                k_pos = start + lax.broadcasted_iota(jnp.int32, s.shape, 0)
                q_pos = i * bq + (lax.broadcasted_iota(jnp.int32, s.shape, 1) & (bq - 1))
                s = jnp.where(k_pos <= q_pos, s, -jnp.inf)
                sm = scores(gi, kmc_ref[...], kmr_ref[...])
                sm = jnp.where(lax.broadcasted_iota(jnp.int32, sm.shape, 0) < N_META, sm, -jnp.inf)
                m_blk = jnp.maximum(jnp.max(s, axis=0, keepdims=True), jnp.max(sm, axis=0, keepdims=True))
                parts = [(s, vt), (sm, vmt_ref[...])]
            else:
                m_blk = mb_scr[slot, :, cols]
                parts = [(s, vt)]
            heads = range(gi * heads_per, (gi + 1) * heads_per)
            _attn_cols_step(parts, m_blk, cols, heads, bq, m_scr, l_scr, acc_scr)

    n_full = (i * bq) // bk
    m_scr[...] = jnp.full_like(m_scr, -jnp.inf)
    l_scr[...] = jnp.zeros_like(l_scr)
    acc_scr[...] = jnp.zeros_like(acc_scr)
    produce(0, 0)

    def pair(k, carry):
        j = 2 * k
        produce(j + 1, 1)
        consume(j, 0, False)
        produce(j + 2, 0)
        consume(j + 1, 1, False)
        return carry

    lax.fori_loop(0, n_full // 2, pair, 0)
    odd = (n_full & 1) == 1

    @pl.when(odd)
    def _():
        produce(n_full, 1)
        consume(n_full - 1, 0, False)
        consume(n_full, 1, True)

    @pl.when(jnp.logical_not(odd))
    def _():
        consume(n_full, 0, True)

    v_dim = acc_scr.shape[0] // MLA_HEADS
    outs = [acc_scr[hd * v_dim:(hd + 1) * v_dim, :] / l_scr[:, hd * bq:(hd + 1) * bq] for hd in range(MLA_HEADS)]
    y = jnp.transpose(jnp.concatenate(outs, axis=0))
    o_ref[...] = _rms(y, g_ref[...]).astype(o_ref.dtype)


def _attn_prompt(qc, qr, kc, vt, kr, kmc, vmt, kmr, g_mla):
    nb, seq, d_kv = kc.shape
    bq, bk, ncol = ATTN_BQ, ATTN_BK, ATTN_COLS
    assert seq % bk == 0 and bk % bq == 0 and bq & (bq - 1) == 0 and N_META >= 1
    assert ncol % bq == 0 and (MLA_HEADS * bq) % ncol == 0
    nq = seq // bq
    rows = MLA_HEADS * bq
    d_out = vt.shape[0]
    qmap = lambda b, i: (0, b * nq + i, 0)
    kmap = lambda b, i: (b, 0, 0)
    c2 = lambda b, i: (0, 0)
    return pl.pallas_call(
        functools.partial(_attn_prompt_body, bq, bk, ncol),
        grid=(nb, nq),
        in_specs=[
            pl.BlockSpec((MLA_HEADS, bq, d_kv), qmap),
            pl.BlockSpec((MLA_HEADS, bq, MLA_ROPE_DIM), qmap),
            pl.BlockSpec((1, seq, d_kv), kmap),
            pl.BlockSpec((d_out, seq), lambda b, i: (0, b)),
            pl.BlockSpec((1, seq, MLA_ROPE_DIM), kmap),
            pl.BlockSpec(kmc.shape, c2),
            pl.BlockSpec(vmt.shape, c2),
            pl.BlockSpec(kmr.shape, c2),
            pl.BlockSpec(g_mla.shape, c2),
        ],
        out_specs=pl.BlockSpec((bq, d_out), lambda b, i: (b * nq + i, 0)),
        out_shape=jax.ShapeDtypeStruct((nb * seq, d_out), BF16),
        scratch_shapes=[pltpu.VMEM((1, rows), F32), pltpu.VMEM((1, rows), F32),
                        pltpu.VMEM((d_out, bq), F32),
                        pltpu.VMEM((2, bk, rows), F32), pltpu.VMEM((2, 1, rows), F32)],
        compiler_params=_cparams(("parallel", "arbitrary")),
        name="attn_prompt",
    )(qc, qr, kc, vt, kr, kmc, vmt, kmr, g_mla)


def _attn_sample_body(npg, page, ds, n_part, pt_ref, qc_ref, qr_ref, cn_ref, rn_ref, wuv_ref, g_ref,
                      cache_c, cache_r, o_ref, pc_buf, pr_buf, sems, kc_scr, krt_scr, s_scr, mb_scr,
                      m_scr, l_scr, acc_scr):
    b = pl.program_id(0)
    j = pl.program_id(1)
    n_steps = pl.num_programs(1)
    step = b * n_steps + j
    n_total = pl.num_programs(0) * n_steps
    ring = pc_buf.shape[0]
    slot = lax.rem(step, ring)
    qc = qc_ref[0]
    qr = qr_ref[0]
    nt = (((1,), (1,)), ((), ()))
    per = npg // n_part

    def page_copies(bb, jj, sl):
        out = []
        for pg in range(npg):
            idx = pt_ref[bb, jj * npg + pg]
            out.append(pltpu.make_async_copy(cache_c.at[idx], pc_buf.at[sl, pg], sems.at[0, sl]))
            out.append(pltpu.make_async_copy(cache_r.at[idx], pr_buf.at[sl, pg], sems.at[1, sl]))
        return out

    for ahead in range(ring - 1):
        @pl.when(jnp.logical_and(step == 0, ahead < n_total))
        def _(ahead=ahead):
            for cp in page_copies(jnp.int32(ahead) // n_steps, lax.rem(jnp.int32(ahead), n_steps), ahead):
                cp.start()

    nxt = step + (ring - 1)

    @pl.when(nxt < n_total)
    def _():
        for cp in page_copies(nxt // n_steps, lax.rem(nxt, n_steps), lax.rem(nxt, ring)):
            cp.start()

    for cp in page_copies(b, j, slot):
        cp.wait()
    pc_refs = [pc_buf.at[slot, pg] for pg in range(npg)]
    pr_refs = [pr_buf.at[slot, pg] for pg in range(npg)]

    @pl.when(j == 0)
    def _():
        m_scr[...] = jnp.full_like(m_scr, -jnp.inf)
        l_scr[...] = jnp.zeros_like(l_scr)
        acc_scr[...] = jnp.zeros_like(acc_scr)

    def keys(part):
        return slice(part * per * page, (part + 1) * per * page)

    def produce(part):
        for pg in range(part * per, (part + 1) * per):
            kc_scr[pg * page:(pg + 1) * page, :] = pc_refs[pg][...].astype(BF16)
            krt_scr[:, pg * page:(pg + 1) * page] = pr_refs[pg][...].astype(BF16)
        s = (lax.dot_general(qc, kc_scr[keys(part), :], nt, preferred_element_type=F32)
             + jnp.dot(qr, krt_scr[:, keys(part)], preferred_element_type=F32))
        s_scr[part] = s
        mb_scr[part] = jnp.max(s, axis=-1, keepdims=True)

    def consume(part):
        s = s_scr[part]
        m_old = m_scr[...]
        m_new = jnp.maximum(m_old, mb_scr[part])
        alpha = jnp.exp(m_old - m_new)
        p = jnp.exp(s - m_new)
        l_scr[...] = alpha * l_scr[...] + jnp.sum(p, axis=-1, keepdims=True)
        acc_scr[...] = alpha * acc_scr[...] + jnp.dot(p.astype(BF16), kc_scr[keys(part), :],
                                                      preferred_element_type=F32)
        m_scr[...] = m_new

    produce(0)
    for part in range(n_part):
        if part + 1 < n_part:
            produce(part + 1)
        consume(part)

    @pl.when(j == pl.num_programs(1) - 1)
    def _():
        pad = LANES - ds
        kn = jnp.concatenate([cn_ref[0], jnp.zeros((pad, cn_ref.shape[-1]), F32)], axis=0).astype(BF16)
        rn = jnp.concatenate([rn_ref[0], jnp.zeros((pad, rn_ref.shape[-1]), F32)], axis=0).astype(BF16)
        s = _scores(qc, qr, kn, rn)
        t_q = lax.broadcasted_iota(jnp.int32, s.shape, 0) & (ds - 1)
        t_k = lax.broadcasted_iota(jnp.int32, s.shape, 1)
        s = jnp.where(t_k <= t_q, s, -jnp.inf)
        _softmax_step(s, kn, m_scr, l_scr, acc_scr)
        o_ref[0] = _attn_finish(ds, wuv_ref, g_ref, l_scr, acc_scr)


def _attn_sample(page_table, qc, qr, c_new, r_new, cache_c, cache_r, w_uv, g_mla):
    db, rows, d_kv = qc.shape
    ds = c_new.shape[1]
    n_pages = page_table.shape[1]
    page = cache_c.shape[1]
    npg = math.gcd(PAGES_PER_STEP, n_pages)
    n_part = math.gcd(SAMPLE_PARTS, npg)
    assert ds & (ds - 1) == 0 and ds <= LANES
    assert cache_r.shape[1:] == (MLA_ROPE_DIM, page)
    d_out = w_uv.shape[0] * w_uv.shape[2]
    bmap = lambda b, j, pt: (b, 0, 0)
    c2 = lambda b, j, pt: (0, 0)
    c3 = lambda b, j, pt: (0, 0, 0)
    in_specs = [
        pl.BlockSpec((1, rows, d_kv), bmap),
        pl.BlockSpec((1, rows, MLA_ROPE_DIM), bmap),
        pl.BlockSpec((1, ds, d_kv), bmap),
        pl.BlockSpec((1, ds, MLA_ROPE_DIM), bmap),
        pl.BlockSpec(w_uv.shape, c3),
        pl.BlockSpec(g_mla.shape, c2),
        pl.BlockSpec(memory_space=pl.ANY),
        pl.BlockSpec(memory_space=pl.ANY),
    ]
    part_keys = (npg // n_part) * page
    grid_spec = pltpu.PrefetchScalarGridSpec(
        num_scalar_prefetch=1,
        grid=(db, n_pages // npg),
        in_specs=in_specs,
        out_specs=pl.BlockSpec((1, ds, d_out), bmap),
        scratch_shapes=[pltpu.VMEM((SAMPLE_RING, npg, page, d_kv), F32),
                        pltpu.VMEM((SAMPLE_RING, npg, MLA_ROPE_DIM, page), F32),
                        pltpu.SemaphoreType.DMA((2, SAMPLE_RING)),
                        pltpu.VMEM((npg * page, d_kv), BF16), pltpu.VMEM((MLA_ROPE_DIM, npg * page), BF16),
                        pltpu.VMEM((n_part, rows, part_keys), F32), pltpu.VMEM((n_part, rows, 1), F32),
                        pltpu.VMEM((rows, 1), F32), pltpu.VMEM((rows, 1), F32), pltpu.VMEM((rows, d_kv), F32)],
    )
    return pl.pallas_call(
        functools.partial(_attn_sample_body, npg, page, ds, n_part),
        grid_spec=grid_spec,
        out_shape=jax.ShapeDtypeStruct((db, ds, d_out), F32),
        compiler_params=_cparams(("arbitrary", "arbitrary")),
        name="attn_sample",
    )(page_table, qc, qr, c_new, r_new, w_uv, g_mla, cache_c, cache_r)


def _cmul_add(cur, sh, a_r, a_i, half):
    return cur + a_r * sh + a_i * pltpu.roll(sh, half, axis=1)


def _ssm_prompt_body(nb, n_chunks, n_levels, pre, u_ref, um_ref, wy_ref, wd_ref, wc_ref, dv_ref, ar_ref, ai_ref,
                     y_ref, hl_ref, scr, e_scr):
    u = u_ref[0]
    ub = u.astype(BF16)
    half = wd_ref.shape[-1] // 2
    wd = wd_ref[0]
    d = jnp.dot(ub, wd, preferred_element_type=F32)
    h_meta = jnp.dot(um_ref[0].astype(BF16), wd, preferred_element_type=F32)[0:1]
    first = lax.broadcasted_iota(jnp.int32, (n_chunks, 1), 0) == 0
    for b in range(nb):
        scr[b, 0:pre, :] = jnp.zeros((pre, scr.shape[-1]), F32)
        scr[b, pre:pre + n_chunks, :] = d[b * n_chunks:(b + 1) * n_chunks]
    for b in range(nb):
        shifted = scr[b, pre - 1:pre - 1 + n_chunks, :]
        scr[b, pre:pre + n_chunks, :] = shifted + jnp.where(first, h_meta, 0.0)
    for k in range(n_levels):
        s = 1 << k
        for b in range(nb):
            cur = scr[b, pre:pre + n_chunks, :]
            sh = scr[b, pre - s:pre - s + n_chunks, :]
            scr[b, pre:pre + n_chunks, :] = _cmul_add(cur, sh, ar_ref[0, k:k + 1, :], ai_ref[0, k:k + 1, :], half)
    for b in range(nb):
        e_scr[b * n_chunks:(b + 1) * n_chunks, :] = scr[b, pre:pre + n_chunks, :]
    e = e_scr[...]
    y_ref[0] = (jnp.dot(ub, wy_ref[0], preferred_element_type=F32)
                + jnp.dot(e.astype(BF16), wc_ref[0], preferred_element_type=F32)
                + u * dv_ref[0])
    h_after = _cmul_add(d, e, ar_ref[0, 0:1, :], ai_ref[0, 0:1, :], half)
    for b in range(nb):
        last = (b + 1) * n_chunks - 1
        hl_ref[0, b:b + 1, :] = h_after[last:last + 1]


def _ssm_sample_body(u_ref, h_ref, wy_ref, wd_ref, wc_ref, dv_ref, ar_ref, ai_ref, y_ref, hl_ref):
    u = u_ref[0]
    ub = u.astype(BF16)
    e = h_ref[0]
    half = e.shape[-1] // 2
    d = jnp.dot(ub, wd_ref[0], preferred_element_type=F32)
    y_ref[0] = (jnp.dot(ub, wy_ref[0], preferred_element_type=F32)
                + jnp.dot(e.astype(BF16), wc_ref[0], preferred_element_type=F32)
                + u * dv_ref[0])
    hl_ref[0] = _cmul_add(d, e, ar_ref[0, 0:1, :], ai_ref[0, 0:1, :], half)


def _chunk_lane_order(n_groups, t_chunk):
    per_slab = LANES // SSM_GROUP
    assert t_chunk % per_slab == 0
    lane = np.arange(t_chunk * SSM_GROUP)
    tile, blk, ch = lane // LANES, (lane % LANES) // SSM_GROUP, lane % SSM_GROUP
    gg = (np.arange(n_groups) % per_slab)[:, None]
    step = per_slab * tile[None] + (blk[None] - gg) % per_slab
    return step * SSM_GROUP + ch[None]


def _ssm_weights(a_re, a_im, log_dt, b_re, b_im, c_re, c_im, d_skip, t_chunk, n_levels, lane_order=None):
    hi = lax.Precision.HIGHEST
    a = lax.complex(a_re.astype(F32), a_im.astype(F32))
    dt = jnp.exp(log_dt.astype(F32))[:, None]
    a_dt = a * dt
    a_bar = jnp.exp(a_dt)
    b_bar = ((a_bar - 1.0) / a)[..., None] * lax.complex(b_re.astype(F32), b_im.astype(F32))
    c = lax.complex(c_re.astype(F32), c_im.astype(F32))
    g, p_state, ch = b_bar.shape
    k = jnp.arange(t_chunk + 1, dtype=F32)
    a_pow = jnp.exp(a_dt[None] * k[:, None, None])
    kern = jnp.einsum('gcp,kgp,gpd->kgcd', c, a_pow[:t_chunk], b_bar, precision=hi).real
    s_idx = jnp.arange(t_chunk)[:, None]
    t_idx = jnp.arange(t_chunk)[None, :]
    lag = t_idx - s_idx
    wy = jnp.where((lag >= 0)[:, :, None, None, None], kern[jnp.clip(lag, 0, t_chunk - 1)], 0.0)
    wy = wy.transpose(2, 0, 4, 1, 3).reshape(g, t_chunk * ch, t_chunk * ch)
    wd = a_pow[:t_chunk][::-1][:, :, :, None] * b_bar[None]
    wd = wd.transpose(1, 0, 3, 2).reshape(g, t_chunk * ch, p_state)
    wd = jnp.concatenate([wd.real, wd.imag], axis=-1)
    gm = c[None] * a_pow[1:][:, :, None, :]
    gm = gm.transpose(1, 3, 0, 2).reshape(g, p_state, t_chunk * ch)
    wc = jnp.concatenate([gm.real, -gm.imag], axis=1)
    dv = jnp.tile(d_skip.astype(F32).reshape(g, 1, ch), (1, 1, t_chunk))
    lev = (t_chunk * (2.0 ** jnp.arange(n_levels, dtype=F32)))
    a_lev = jnp.exp(a_dt[:, None, :] * lev[None, :, None])
    a_r = jnp.concatenate([a_lev.real, a_lev.real], axis=-1)
    a_i = jnp.concatenate([-a_lev.imag, a_lev.imag], axis=-1)
    if lane_order is not None:
        rows, cols = lane_order[:, :, None], lane_order[:, None, :]
        wy = jnp.take_along_axis(jnp.take_along_axis(wy, rows, axis=1), cols, axis=2)
        wd = jnp.take_along_axis(wd, rows, axis=1)
        wc = jnp.take_along_axis(wc, cols, axis=2)
        dv = jnp.take_along_axis(dv, cols, axis=2)
    return wy.astype(BF16), wd.astype(BF16), wc.astype(BF16), dv, a_r, a_i


def _ssm_prompt(u_rows, um_rows, nb, ops):
    wy, wd, wc, dv, a_r, a_i = ops
    g, r, tc = u_rows.shape
    n_chunks = r // nb
    n_levels = a_r.shape[1]
    assert (1 << n_levels) >= n_chunks and n_chunks % SUBLANES == 0
    st = wd.shape[-1]
    pre = -(-(1 << (n_levels - 1)) // SUBLANES) * SUBLANES
    gmap = lambda i: (i, 0, 0)
    return pl.pallas_call(
        functools.partial(_ssm_prompt_body, nb, n_chunks, n_levels, pre),
        grid=(g,),
        in_specs=[pl.BlockSpec((1, r, tc), gmap), pl.BlockSpec((1,) + um_rows.shape[1:], gmap),
                  pl.BlockSpec((1,) + wy.shape[1:], gmap),
                  pl.BlockSpec((1,) + wd.shape[1:], gmap), pl.BlockSpec((1,) + wc.shape[1:], gmap),
                  pl.BlockSpec((1,) + dv.shape[1:], gmap), pl.BlockSpec((1,) + a_r.shape[1:], gmap),
                  pl.BlockSpec((1,) + a_i.shape[1:], gmap)],
        out_specs=[pl.BlockSpec((1, r, tc), gmap), pl.BlockSpec((1, nb, st), gmap)],
        out_shape=[jax.ShapeDtypeStruct((g, r, tc), F32), jax.ShapeDtypeStruct((g, nb, st), F32)],
        scratch_shapes=[pltpu.VMEM((nb, pre + n_chunks, st), F32), pltpu.VMEM((r, st), F32)],
        compiler_params=_cparams(("parallel",)),
        name="ssm_prompt",
    )(u_rows, um_rows, wy, wd, wc, dv, a_r, a_i)


def _ssm_sample(u_rows, h_rows, ops):
    wy, wd, wc, dv, a_r, a_i = ops
    g, r, tc = u_rows.shape
    st = wd.shape[-1]
    gmap = lambda i: (i, 0, 0)
    return pl.pallas_call(
        _ssm_sample_body,
        grid=(g,),
        in_specs=[pl.BlockSpec((1, r, tc), gmap), pl.BlockSpec((1, r, st), gmap),
                  pl.BlockSpec((1,) + wy.shape[1:], gmap), pl.BlockSpec((1,) + wd.shape[1:], gmap),
                  pl.BlockSpec((1,) + wc.shape[1:], gmap), pl.BlockSpec((1,) + dv.shape[1:], gmap),
                  pl.BlockSpec((1,) + a_r.shape[1:], gmap), pl.BlockSpec((1,) + a_i.shape[1:], gmap)],
        out_specs=[pl.BlockSpec((1, r, tc), gmap), pl.BlockSpec((1, r, st), gmap)],
        out_shape=[jax.ShapeDtypeStruct((g, r, tc), F32), jax.ShapeDtypeStruct((g, r, st), F32)],
        compiler_params=_cparams(("parallel",)),
        name="ssm_sample",
    )(u_rows, h_rows, wy, wd, wc, dv, a_r, a_i)


def _mix_out_body(t_chunk, ys_ref, ya_ref, x_ref, wglu_ref, bglu_ref, gs_ref, wo_ref, gpost_ref, o_ref,
                  *maybe_scr):
    if t_chunk:
        (slab_scr,) = maybe_scr
        n_slabs = slab_scr.shape[0]
        n_rows = slab_scr.shape[1] // t_chunk
        per_slab = LANES // SSM_GROUP
        lane_blk = lax.broadcasted_iota(jnp.int32, (n_rows, LANES), 1) // SSM_GROUP
        for k in range(n_slabs):
            for t in range(t_chunk // per_slab):
                tiles = [ys_ref[k * per_slab + gg, :, t * LANES:(t + 1) * LANES] for gg in range(per_slab)]
                for s1 in range(per_slab):
                    w = tiles[(-s1) % per_slab]
                    for q in range(1, per_slab):
                        w = jnp.where(lane_blk == q, tiles[(q - s1) % per_slab], w)
                    if s1:
                        w = pltpu.roll(w, (per_slab - s1) * SSM_GROUP, axis=1)
                    slab_scr[k, pl.ds(t * per_slab + s1, n_rows, stride=t_chunk), :] = w
        ys = jnp.concatenate([slab_scr[k] for k in range(n_slabs)], axis=-1)
    else:
        ys = ys_ref[...]
    z = jax.nn.gelu(ys)
    gate = jax.nn.sigmoid(jnp.dot(z.astype(BF16), wglu_ref[...], preferred_element_type=F32) + bglu_ref[...])
    ns = _rms(z * gate, gs_ref[...]).astype(BF16)
    w = ns.shape[-1]
    y = (jnp.dot(ns, wo_ref[:w, :], preferred_element_type=F32)
         + jnp.dot(ya_ref[...].astype(BF16), wo_ref[w:, :], preferred_element_type=F32))
    o_ref[...] = x_ref[...] + _rms(y, gpost_ref[...])


def _mix_out(ys, ya, x, w_glu, b_glu, g_ssm, w_o, g_post, t_chunk, name):
    n, d = x.shape
    tm = _row_tile(n)
    row = lambda i: (i, 0)
    c2 = lambda i: (0, 0)
    if t_chunk:
        assert tm % (t_chunk * SUBLANES) == 0
        w = ys.shape[0] * SSM_GROUP
        assert w % LANES == 0
        ys_spec = pl.BlockSpec((ys.shape[0], tm // t_chunk, ys.shape[2]), lambda i: (0, i, 0))
        scratch = [pltpu.VMEM((w // LANES, tm, LANES), F32)]
    else:
        ys_spec = pl.BlockSpec((tm, ys.shape[1]), row)
        scratch = []
    return pl.pallas_call(
        functools.partial(_mix_out_body, t_chunk),
        grid=(n // tm,),
        in_specs=[ys_spec, pl.BlockSpec((tm, ya.shape[1]), row), pl.BlockSpec((tm, d), row),
                  pl.BlockSpec(w_glu.shape, c2), pl.BlockSpec(b_glu.shape, c2), pl.BlockSpec(g_ssm.shape, c2),
                  pl.BlockSpec(w_o.shape, c2), pl.BlockSpec(g_post.shape, c2)],
        out_specs=pl.BlockSpec((tm, d), row),
        out_shape=jax.ShapeDtypeStruct((n, d), F32),
        scratch_shapes=scratch,
        compiler_params=_cparams(("parallel",)),
        name=name,
    )(ys, ya, x, w_glu, b_glu, g_ssm, w_o, g_post)


def _rope_tables(pos):
    inv = ROPE_THETA ** (-jnp.arange(ROPE_HALF, dtype=F32) / ROPE_HALF)
    ang = pos.astype(F32)[:, None] * inv[None, :]
    reps = LANES // ROPE_HALF
    return jnp.tile(jnp.cos(ang), (1, reps)), jnp.tile(jnp.sin(ang), (1, reps))


def _uq_column_order():
    per = MLA_NOPE_DIM + MLA_ROPE_DIM
    heads = np.arange(MLA_HEADS)[:, None]
    nope = (heads * per + np.arange(MLA_NOPE_DIM)[None, :]).reshape(-1)
    rope1 = (heads * per + MLA_NOPE_DIM + np.arange(ROPE_HALF)[None, :]).reshape(-1)
    rope2 = (heads * per + MLA_NOPE_DIM + ROPE_HALF + np.arange(ROPE_HALF)[None, :]).reshape(-1)
    return np.concatenate([nope, rope1, rope2])


def kernel(x_prompt, x_sample, cache_kv_latent, cache_k_rope, state_ssm_re, state_ssm_im, page_table, meta_tokens, g_ff1_pre, w_ff1_gate, w_ff1_up, w_ff1_down, g_ff1_post, g_mix_pre, w_in, ssm_a_re, ssm_a_im, ssm_log_dt, ssm_b_re, ssm_b_im, ssm_c_re, ssm_c_im, ssm_d, w_glu, b_glu, g_q_norm, w_uq, g_kv_norm, w_uk, w_uv, g_ssm_out, g_mla_out, w_o, g_mix_post, g_ff2_pre, w_ff2_gate, w_ff2_up, w_ff2_down, g_ff2_post):
    depth = w_in.shape[0]
    assert depth == 1, "single-layer step"
    bp, seq, d_model = x_prompt.shape
    db, ds, _ = x_sample.shape
    n_pages = page_table.shape[1]
    page = cache_kv_latent.shape[2]
    past_len = n_pages * page
    d_kv = cache_kv_latent.shape[3]
    d_q = w_uq.shape[1]
    n_groups, n_state = ssm_a_re.shape[1], ssm_a_re.shape[2]
    d_u = n_groups * SSM_GROUP
    dims = (d_u, d_q, d_kv)
    l = 0
    row = lambda v: v[l].reshape(1, -1).astype(F32)

    ff1_w = _ffn_weights(w_ff1_gate[l], w_ff1_up[l], w_ff1_down[l])
    ff2_w = _ffn_weights(w_ff2_gate[l], w_ff2_up[l], w_ff2_down[l])
    w_in_b = w_in[l].astype(BF16)
    w_uq_b = w_uq[l][:, _uq_column_order()].astype(BF16)
    w_ukt = jnp.transpose(w_uk[l], (1, 2, 0)).astype(BF16)
    w_uv_b = jnp.transpose(w_uv[l], (1, 0, 2)).astype(BF16)
    w_uvt_b = w_uv[l].reshape(d_kv, -1).T.astype(BF16)
    w_glu_b = w_glu[l].astype(BF16)
    w_o_b = w_o[l].astype(BF16)

    xs = [x_prompt.reshape(bp * seq, d_model), x_sample.reshape(db * ds, d_model), meta_tokens.astype(F32)]
    names = ["prompt", "sample", "meta"]
    pos = [N_META + jnp.tile(jnp.arange(seq), bp), past_len + jnp.tile(jnp.arange(ds), db), jnp.arange(N_META)]

    t_p = SSM_CHUNK
    assert N_META == t_p and seq % t_p == 0
    x1, pr = [], []
    for x, nm, ps, t_rows in zip(xs, names, pos, (t_p, 0, 0)):
        y, hn = _ffn(x, row(g_ff1_pre), ff1_w, row(g_ff1_post), row(g_mix_pre), True, "ffn1_" + nm)
        cos, sin = _rope_tables(ps)
        x1.append(y)
        pr.append(_proj(hn, w_in_b, row(g_q_norm), w_uq_b, w_ukt, row(g_kv_norm), w_uvt_b, cos, sin, dims, t_rows,
                        "proj_" + nm))
    (_, ckv_p, ckvb_p, vt_p, kr_p, krb_p, qc_p, qr_p, u_rows) = pr[0]
    (u_s, ckv_s, _, _, kr_s, _, qc_s, qr_s) = pr[1]
    (u_m, ckv_m, ckvb_m, vt_m, kr_m, krb_m, _, _) = pr[2]

    n_levels = max(1, (seq // t_p - 1).bit_length())
    ssm_w = (ssm_a_re[l], ssm_a_im[l], ssm_log_dt[l], ssm_b_re[l], ssm_b_im[l], ssm_c_re[l], ssm_c_im[l], ssm_d[l])
    lane_order = jnp.asarray(_chunk_lane_order(n_groups, t_p))
    ops_p = _ssm_weights(*ssm_w, t_p, n_levels, lane_order)
    ops_s = _ssm_weights(*ssm_w, ds, 1)
    tc = t_p * SSM_GROUP
    um_rows = u_m.reshape(1, t_p, n_groups, SSM_GROUP).transpose(2, 0, 1, 3).reshape(n_groups, 1, tc)
    um_rows = jnp.take_along_axis(um_rows, lane_order[:, None, :], axis=2)
    um_rows = jnp.pad(um_rows, ((0, 0), (0, SUBLANES - 1), (0, 0)))
    ys_p, hl_p = _ssm_prompt(u_rows, um_rows, bp, ops_p)
    hl_p = hl_p.transpose(1, 0, 2)

    us_rows = u_s.reshape(db, ds, n_groups, SSM_GROUP).transpose(2, 0, 1, 3).reshape(n_groups, db, ds * SSM_GROUP)
    h0_rows = jnp.concatenate([state_ssm_re[l], state_ssm_im[l]], axis=-1).astype(F32).transpose(1, 0, 2)
    ysr, hl_s = _ssm_sample(us_rows, h0_rows, ops_s)
    ys_s = ysr.reshape(n_groups, db, ds, SSM_GROUP).transpose(1, 2, 0, 3).reshape(db * ds, d_u)
    hl_s = hl_s.transpose(1, 0, 2)

    g_mla = row(g_mla_out)
    pad_m = LANES - N_META
    kmc = jnp.pad(ckvb_m, ((0, pad_m), (0, 0)))
    kmr = jnp.pad(krb_m, ((0, pad_m), (0, 0)))
    vmt = jnp.pad(vt_m, ((0, 0), (0, pad_m)))
    ya_p = _attn_prompt(qc_p, qr_p, ckvb_p.reshape(bp, seq, d_kv), vt_p,
                        krb_p.reshape(bp, seq, MLA_ROPE_DIM), kmc, vmt, kmr, g_mla)
    to_seq = lambda q: q.reshape(MLA_HEADS, db, ds, q.shape[-1]).transpose(1, 0, 2, 3).reshape(db, MLA_HEADS * ds, q.shape[-1])
    ya_s = _attn_sample(page_table, to_seq(qc_s), to_seq(qr_s), ckv_s.reshape(db, ds, d_kv),
                        kr_s.reshape(db, ds, MLA_ROPE_DIM), cache_kv_latent[l],
                        jnp.swapaxes(cache_k_rope[l], 1, 2), w_uv_b, g_mla)
    ya_s = ya_s.reshape(db * ds, -1)

    outs = []
    for x, ys, ya, t_rows, nm in ((x1[0], ys_p, ya_p, t_p, "prompt"), (x1[1], ys_s, ya_s, 0, "sample")):
        x2 = _mix_out(ys, ya, x, w_glu_b, row(b_glu), row(g_ssm_out), w_o_b, row(g_mix_post), t_rows,
                      "mix_out_" + nm)
        y, _ = _ffn(x2, row(g_ff2_pre), ff2_w, row(g_ff2_post), row(g_ff2_post), False, "ffn2_" + nm)
        outs.append(y)

    y_prompt = outs[0].reshape(bp, seq, d_model)
    y_sample = outs[1].reshape(db, ds, d_model)
    meta_b = lambda v: jnp.broadcast_to(v[None], (bp,) + v.shape)
    new_ckv_p = jnp.concatenate([meta_b(ckv_m), ckv_p.reshape(bp, seq, d_kv)], axis=1)[None]
    new_kr_p = jnp.concatenate([meta_b(kr_m), kr_p.reshape(bp, seq, MLA_ROPE_DIM)], axis=1)[None]
    return (y_prompt, y_sample, new_ckv_p, new_kr_p,
            hl_p[None, :, :, :n_state], hl_p[None, :, :, n_state:],
            ckv_s.reshape(1, db, ds, d_kv), kr_s.reshape(1, db, ds, MLA_ROPE_DIM),
            hl_s[None, :, :, :n_state], hl_s[None, :, :, n_state:])
```

```python
import functools
import math

import jax
import jax.numpy as jnp
from jax import lax
from jax.experimental import pallas as pl
from jax.experimental.pallas import tpu as pltpu

F32 = jnp.float32
BF16 = jnp.bfloat16

N_META = 16
SSM_GROUP = 16
MLA_HEADS = 8
MLA_NOPE_DIM = 64
MLA_ROPE_DIM = 32
ROPE_HALF = MLA_ROPE_DIM // 2
ROPE_THETA = 10000.0
RMS_EPS = 1e-6
ATTN_SCALE = (MLA_NOPE_DIM + MLA_ROPE_DIM) ** -0.5

LANES = 128
SUBLANES = 8
VMEM_LIMIT = 56 * 1024 * 1024

ROW_TILE = 512
FFN_ROW_TILE = 1024
FF_TILE = 256
ATTN_BQ = 256
ATTN_BK = 512
ATTN_COLS = 512
PAGES_PER_STEP = 64
SAMPLE_PARTS = 2
SAMPLE_RING = 2
SSM_CHUNK = 16


def _rms(x, g):
    return (x * lax.rsqrt(jnp.mean(x * x, axis=-1, keepdims=True) + RMS_EPS)) * g


def _row_tile(n):
    return ROW_TILE if n % ROW_TILE == 0 else n


def _cparams(sem):
    return pltpu.CompilerParams(dimension_semantics=sem, vmem_limit_bytes=VMEM_LIMIT)


def _ffn_body(emit_norm, x_ref, gpre_ref, wg_ref, wu_ref, wd_ref, gpost_ref, gnext_ref, *refs):
    if emit_norm:
        y_ref, hn_ref, h_scr, acc_scr = refs
    else:
        y_ref, h_scr, acc_scr = refs
        hn_ref = None
    j = pl.program_id(1)

    @pl.when(j == 0)
    def _():
        h_scr[...] = _rms(x_ref[...], gpre_ref[...]).astype(BF16)
        acc_scr[...] = jnp.zeros_like(acc_scr)

    h = h_scr[...]
    g = jnp.dot(h, wg_ref[...], preferred_element_type=F32)
    u = jnp.dot(h, wu_ref[...], preferred_element_type=F32)
    a = (g * jax.nn.sigmoid(g)) * u
    acc_scr[...] += jnp.dot(a.astype(BF16), wd_ref[...], preferred_element_type=F32)

    @pl.when(j == pl.num_programs(1) - 1)
    def _():
        y = x_ref[...] + 0.5 * _rms(acc_scr[...], gpost_ref[...])
        y_ref[...] = y
        if emit_norm:
            hn_ref[...] = _rms(y, gnext_ref[...]).astype(BF16)


def _ffn_weights(wg, wu, wd):
    d, d_ff = wg.shape
    tf = FF_TILE
    assert d_ff % tf == 0
    chunk = lambda w: w.astype(BF16).reshape(d, d_ff // tf, tf).transpose(1, 0, 2)
    return chunk(wg), chunk(wu), wd.astype(BF16).reshape(d_ff // tf, tf, d)


def _ffn(x, g_pre, weights, g_post, g_next, emit_norm, name):
    wg, wu, wd = weights
    n, d = x.shape
    nj, _, tf = wg.shape
    tm = FFN_ROW_TILE if n % FFN_ROW_TILE == 0 else n
    row = lambda i, j: (i, 0)
    vec = lambda i, j: (0, 0)
    chunk = lambda i, j: (j, 0, 0)
    out_shape = [jax.ShapeDtypeStruct((n, d), F32)]
    out_specs = [pl.BlockSpec((tm, d), row)]
    if emit_norm:
        out_shape.append(jax.ShapeDtypeStruct((n, d), BF16))
        out_specs.append(pl.BlockSpec((tm, d), row))
    res = pl.pallas_call(
        functools.partial(_ffn_body, emit_norm),
        grid=(n // tm, nj),
        in_specs=[
            pl.BlockSpec((tm, d), row),
            pl.BlockSpec((1, d), vec),
            pl.BlockSpec((None, d, tf), chunk),
            pl.BlockSpec((None, d, tf), chunk),
            pl.BlockSpec((None, tf, d), chunk),
            pl.BlockSpec((1, d), vec),
            pl.BlockSpec((1, d), vec),
        ],
        out_specs=out_specs,
        out_shape=out_shape,
        scratch_shapes=[pltpu.VMEM((tm, d), BF16), pltpu.VMEM((tm, d), F32)],
        compiler_params=_cparams(("parallel", "arbitrary")),
        name=name,
    )(x, g_pre, wg, wu, wd, g_post, g_next)
    return res if emit_norm else (res[0], None)


def _proj_body(d_u, d_q, d_kv, t_chunk, h_ref, win_ref, gq_ref, wuq_ref, wukt_ref, gkv_ref, wuvt_ref,
               cos_ref, sin_ref,
               u_ref, ckv_ref, ckvb_ref, vt_ref, kr_ref, krb_ref, qc_ref, qr_ref, *maybe_urows):
    proj = jnp.dot(h_ref[...], win_ref[...], preferred_element_type=F32)
    off_q, off_kv, off_kr = d_u, d_u + d_q, d_u + d_q + d_kv
    u_ref[...] = proj[:, :off_q]
    if t_chunk:
        urows_ref, slab_scr = maybe_urows
        n_rows = u_ref.shape[0] // t_chunk
        per_slab = LANES // SSM_GROUP
        lane_blk = lax.broadcasted_iota(jnp.int32, (n_rows, LANES), 1) // SSM_GROUP
        for k in range(d_u // LANES):
            slab_scr[k] = proj[:, k * LANES:(k + 1) * LANES]
        for k in range(d_u // LANES):
            rolled = []
            for s in range(t_chunk):
                x = slab_scr[k, pl.ds(s, n_rows, stride=t_chunk), :]
                shift = (s % per_slab) * SSM_GROUP
                rolled.append(pltpu.roll(x, shift, axis=1) if shift else x)
            for gg in range(per_slab):
                tiles = []
                for t in range(t_chunk // per_slab):
                    tile = rolled[t * per_slab]
                    for s1 in range(1, per_slab):
                        tile = jnp.where(lane_blk == (gg + s1) % per_slab, rolled[t * per_slab + s1], tile)
                    tiles.append(tile)
                urows_ref[k * per_slab + gg] = jnp.concatenate(tiles, axis=-1)
    cq = _rms(proj[:, off_q:off_kv], gq_ref[...]).astype(BF16)
    q = jnp.dot(cq, wuq_ref[...], preferred_element_type=F32) * ATTN_SCALE
    n_nope = MLA_HEADS * MLA_NOPE_DIM
    for hd in range(MLA_HEADS):
        qn = q[:, hd * MLA_NOPE_DIM:(hd + 1) * MLA_NOPE_DIM].astype(BF16)
        qc_ref[hd] = jnp.dot(qn, wukt_ref[hd], preferred_element_type=F32).astype(BF16)
    cos = cos_ref[...]
    sin = sin_ref[...]
    r1 = q[:, n_nope:n_nope + LANES]
    r2 = q[:, n_nope + LANES:n_nope + 2 * LANES]
    o1 = r1 * cos - r2 * sin
    o2 = r2 * cos + r1 * sin
    for hd in range(MLA_HEADS):
        sl = slice(hd * ROPE_HALF, (hd + 1) * ROPE_HALF)
        qr_ref[hd] = jnp.concatenate([o1[:, sl], o2[:, sl]], axis=-1).astype(BF16)
    ckv = _rms(proj[:, off_kv:off_kr], gkv_ref[...])
    ckv_ref[...] = ckv
    ckvb = ckv.astype(BF16)
    ckvb_ref[...] = ckvb
    vt_ref[...] = lax.dot_general(wuvt_ref[...], ckvb, (((1,), (1,)), ((), ())),
                                  preferred_element_type=F32).astype(BF16)
    x1 = proj[:, off_kr:off_kr + ROPE_HALF]
    x2 = proj[:, off_kr + ROPE_HALF:off_kr + MLA_ROPE_DIM]
    c16 = cos[:, :ROPE_HALF]
    s16 = sin[:, :ROPE_HALF]
    kr = jnp.concatenate([x1 * c16 - x2 * s16, x2 * c16 + x1 * s16], axis=-1)
    kr_ref[...] = kr
    krb_ref[...] = kr.astype(BF16)


def _proj(hn, w_in, g_q, w_uq, w_ukt, g_kv, w_uvt, cos, sin, dims, t_chunk, name):
    n, d = hn.shape
    d_u, d_q, d_kv = dims
    tm = _row_tile(n)
    row = lambda i: (i, 0)
    full2 = lambda i: (0, 0)
    full3 = lambda i: (0, 0, 0)
    hrow = lambda i: (0, i, 0)
    extra_specs, extra_shapes, scratch = [], [], []
    if t_chunk:
        assert tm % (t_chunk * SUBLANES) == 0 and d_u % LANES == 0
        n_groups = d_u // SSM_GROUP
        extra_specs = [pl.BlockSpec((n_groups, tm // t_chunk, t_chunk * SSM_GROUP), hrow)]
        extra_shapes = [jax.ShapeDtypeStruct((n_groups, n // t_chunk, t_chunk * SSM_GROUP), F32)]
        scratch = [pltpu.VMEM((d_u // LANES, tm, LANES), F32)]
    return pl.pallas_call(
        functools.partial(_proj_body, d_u, d_q, d_kv, t_chunk),
        grid=(n // tm,),
        in_specs=[
            pl.BlockSpec((tm, d), row),
            pl.BlockSpec(w_in.shape, full2),
            pl.BlockSpec(g_q.shape, full2),
            pl.BlockSpec(w_uq.shape, full2),
            pl.BlockSpec(w_ukt.shape, full3),
            pl.BlockSpec(g_kv.shape, full2),
            pl.BlockSpec(w_uvt.shape, full2),
            pl.BlockSpec((tm, LANES), row),
            pl.BlockSpec((tm, LANES), row),
        ],
        out_specs=[
            pl.BlockSpec((tm, d_u), row),
            pl.BlockSpec((tm, d_kv), row),
            pl.BlockSpec((tm, d_kv), row),
            pl.BlockSpec((w_uvt.shape[0], tm), lambda i: (0, i)),
            pl.BlockSpec((tm, MLA_ROPE_DIM), row),
            pl.BlockSpec((tm, MLA_ROPE_DIM), row),
            pl.BlockSpec((MLA_HEADS, tm, d_kv), hrow),
            pl.BlockSpec((MLA_HEADS, tm, MLA_ROPE_DIM), hrow),
        ] + extra_specs,
        out_shape=[
            jax.ShapeDtypeStruct((n, d_u), F32),
            jax.ShapeDtypeStruct((n, d_kv), F32),
            jax.ShapeDtypeStruct((n, d_kv), BF16),
            jax.ShapeDtypeStruct((w_uvt.shape[0], n), BF16),
            jax.ShapeDtypeStruct((n, MLA_ROPE_DIM), F32),
            jax.ShapeDtypeStruct((n, MLA_ROPE_DIM), BF16),
            jax.ShapeDtypeStruct((MLA_HEADS, n, d_kv), BF16),
            jax.ShapeDtypeStruct((MLA_HEADS, n, MLA_ROPE_DIM), BF16),
        ] + extra_shapes,
        scratch_shapes=scratch,
        compiler_params=_cparams(("parallel",)),
        name=name,
    )(hn, w_in, g_q, w_uq, w_ukt, g_kv, w_uvt, cos, sin)


def _scores(qc, qr, kc, kr):
    nt = (((1,), (1,)), ((), ()))
    return (lax.dot_general(qc, kc, nt, preferred_element_type=F32)
            + lax.dot_general(qr, kr, nt, preferred_element_type=F32))


def _softmax_step(s, kc, m_scr, l_scr, acc_scr):
    m_old = m_scr[...]
    m_new = jnp.maximum(m_old, jnp.max(s, axis=-1, keepdims=True))
    alpha = jnp.exp(m_old - m_new)
    p = jnp.exp(s - m_new)
    l_scr[...] = alpha * l_scr[...] + jnp.sum(p, axis=-1, keepdims=True)
    acc_scr[...] = alpha * acc_scr[...] + jnp.dot(p.astype(BF16), kc, preferred_element_type=F32)
    m_scr[...] = m_new


def _attn_finish(rows_per_head, wuv_ref, g_ref, l_scr, acc_scr):
    o = (acc_scr[...] / l_scr[...]).astype(BF16)
    outs = []
    for hd in range(MLA_HEADS):
        oh = o[hd * rows_per_head:(hd + 1) * rows_per_head]
        outs.append(jnp.dot(oh, wuv_ref[hd], preferred_element_type=F32))
    return _rms(jnp.concatenate(outs, axis=-1), g_ref[...])


def _attn_cols_step(parts, m_blk, cols, heads, bq, m_scr, l_scr, acc_scr):
    m_old = m_scr[:, cols]
    m_new = jnp.maximum(m_old, m_blk)
    alpha = jnp.exp(m_old - m_new)
    ps = [jnp.exp(s - m_new) for s, _ in parts]
    l_new = alpha * l_scr[:, cols]
    for p in ps:
        l_new = l_new + jnp.sum(p, axis=0, keepdims=True)
    l_scr[:, cols] = l_new
    m_scr[:, cols] = m_new
    v_dim = acc_scr.shape[0] // MLA_HEADS
    for n, hd in enumerate(heads):
        hc = slice(n * bq, (n + 1) * bq)
        rows = slice(hd * v_dim, (hd + 1) * v_dim)
        acc = alpha[:, hc] * acc_scr[rows, :]
        for p, (_, vt) in zip(ps, parts):
            acc = acc + jnp.dot(vt[rows, :], p[:, hc].astype(BF16), preferred_element_type=F32)
        acc_scr[rows, :] = acc


def _attn_prompt_body(bq, bk, ncol, qc_ref, qr_ref, kc_ref, vt_ref, kr_ref, kmc_ref, vmt_ref, kmr_ref,
                      g_ref, o_ref, m_scr, l_scr, acc_scr, s_scr, mb_scr):
    i = pl.program_id(1)
    nt = (((1,), (1,)), ((), ()))
    heads_per = ncol // bq
    n_groups = MLA_HEADS // heads_per
    groups = [slice(gi * ncol, (gi + 1) * ncol) for gi in range(n_groups)]

    def scores(gi, kc, kr):
        hs = slice(gi * heads_per, (gi + 1) * heads_per)
        qc = qc_ref[hs].reshape(ncol, qc_ref.shape[-1])
        qr = qr_ref[hs].reshape(ncol, qr_ref.shape[-1])
        return (lax.dot_general(kc, qc, nt, preferred_element_type=F32)
                + lax.dot_general(kr, qr, nt, preferred_element_type=F32))

    def produce(j, slot):
        start = pl.multiple_of(j * bk, bk)
        kc = kc_ref[0, pl.ds(start, bk), :]
        kr = kr_ref[0, pl.ds(start, bk), :]
        for gi, cols in enumerate(groups):
            s = scores(gi, kc, kr)
            s_scr[slot, :, cols] = s
            mb_scr[slot, :, cols] = jnp.max(s, axis=0, keepdims=True)

    def consume(j, slot, diagonal):
        start = pl.multiple_of(j * bk, bk)
        vt = vt_ref[:, pl.ds(start, bk)]
        for gi, cols in enumerate(groups):
            s = s_scr[slot, :, cols]
            if diagonal:
                k_pos = start + lax.broadcasted_iota(jnp.int32, s.shape, 0)
                q_pos = i * bq + (lax.broadcasted_iota(jnp.int32, s.shape, 1) & (bq - 1))
                s = jnp.where(k_pos <= q_pos, s, -jnp.inf)
                sm = scores(gi, kmc_ref[...], kmr_ref[...])
                sm = jnp.where(lax.broadcasted_iota(jnp.int32, sm.shape, 0) < N_META, sm, -jnp.inf)
                m_blk = jnp.maximum(jnp.max(s, axis=0, keepdims=True), jnp.max(sm, axis=0, keepdims=True))
                parts = [(s, vt), (sm, vmt_ref[...])]
            else:
                m_blk = mb_scr[slot, :, cols]
                parts = [(s, vt)]
            heads = range(gi * heads_per, (gi + 1) * heads_per)
            _attn_cols_step(parts, m_blk, cols, heads, bq, m_scr, l_scr, acc_scr)

    n_full = (i * bq) // bk
    m_scr[...] = jnp.full_like(m_scr, -jnp.inf)
    l_scr[...] = jnp.zeros_like(l_scr)
    acc_scr[...] = jnp.zeros_like(acc_scr)
    produce(0, 0)

    def pair(k, carry):
        j = 2 * k
        produce(j + 1, 1)
        consume(j, 0, False)
        produce(j + 2, 0)
        consume(j + 1, 1, False)
        return carry

    lax.fori_loop(0, n_full // 2, pair, 0)
    odd = (n_full & 1) == 1

    @pl.when(odd)
    def _():
        produce(n_full, 1)
        consume(n_full - 1, 0, False)
        consume(n_full, 1, True)

    @pl.when(jnp.logical_not(odd))
    def _():
        consume(n_full, 0, True)

    v_dim = acc_scr.shape[0] // MLA_HEADS
    outs = [acc_scr[hd * v_dim:(hd + 1) * v_dim, :] / l_scr[:, hd * bq:(hd + 1) * bq] for hd in range(MLA_HEADS)]
    y = jnp.transpose(jnp.concatenate(outs, axis=0))
    o_ref[...] = _rms(y, g_ref[...]).astype(o_ref.dtype)


def _attn_prompt(qc, qr, kc, vt, kr, kmc, vmt, kmr, g_mla):
    nb, seq, d_kv = kc.shape
    bq, bk, ncol = ATTN_BQ, ATTN_BK, ATTN_COLS
    assert seq % bk == 0 and bk % bq == 0 and bq & (bq - 1) == 0 and N_META >= 1
    assert ncol % bq == 0 and (MLA_HEADS * bq) % ncol == 0
    nq = seq // bq
    rows = MLA_HEADS * bq
    d_out = vt.shape[0]
    qmap = lambda b, i: (0, b * nq + i, 0)
    kmap = lambda b, i: (b, 0, 0)
    c2 = lambda b, i: (0, 0)
    return pl.pallas_call(
        functools.partial(_attn_prompt_body, bq, bk, ncol),
        grid=(nb, nq),
        in_specs=[
            pl.BlockSpec((MLA_HEADS, bq, d_kv), qmap),
            pl.BlockSpec((MLA_HEADS, bq, MLA_ROPE_DIM), qmap),
            pl.BlockSpec((1, seq, d_kv), kmap),
            pl.BlockSpec((d_out, seq), lambda b, i: (0, b)),
            pl.BlockSpec((1, seq, MLA_ROPE_DIM), kmap),
            pl.BlockSpec(kmc.shape, c2),
            pl.BlockSpec(vmt.shape, c2),
            pl.BlockSpec(kmr.shape, c2),
            pl.BlockSpec(g_mla.shape, c2),
        ],
        out_specs=pl.BlockSpec((bq, d_out), lambda b, i: (b * nq + i, 0)),
        out_shape=jax.ShapeDtypeStruct((nb * seq, d_out), BF16),
        scratch_shapes=[pltpu.VMEM((1, rows), F32), pltpu.VMEM((1, rows), F32),
                        pltpu.VMEM((d_out, bq), F32),
                        pltpu.VMEM((2, bk, rows), F32), pltpu.VMEM((2, 1, rows), F32)],
        compiler_params=_cparams(("parallel", "arbitrary")),
        name="attn_prompt",
    )(qc, qr, kc, vt, kr, kmc, vmt, kmr, g_mla)


def _attn_sample_body(npg, page, ds, n_part, pt_ref, qc_ref, qr_ref, cn_ref, rn_ref, wuv_ref, g_ref,
                      cache_c, cache_r, o_ref, pc_buf, pr_buf, sems, kc_scr, krt_scr, s_scr, mb_scr,
                      m_scr, l_scr, acc_scr):
    b = pl.program_id(0)
    j = pl.program_id(1)
    n_steps = pl.num_programs(1)
    step = b * n_steps + j
    n_total = pl.num_programs(0) * n_steps
    ring = pc_buf.shape[0]
    slot = lax.rem(step, ring)
    qc = qc_ref[0]
    qr = qr_ref[0]
    nt = (((1,), (1,)), ((), ()))
    per = npg // n_part

    def page_copies(bb, jj, sl):
        out = []
        for pg in range(npg):
            idx = pt_ref[bb, jj * npg + pg]
            out.append((pltpu.make_async_copy(cache_c.at[idx], pc_buf.at[sl, pg], sems.at[0, sl]), pg % 2))
            out.append((pltpu.make_async_copy(cache_r.at[idx], pr_buf.at[sl, pg], sems.at[1, sl]), pg % 2))
        return out

    for ahead in range(ring - 1):
        @pl.when(jnp.logical_and(step == 0, ahead < n_total))
        def _(ahead=ahead):
            for cp, prio in page_copies(jnp.int32(ahead) // n_steps, lax.rem(jnp.int32(ahead), n_steps), ahead):
                cp.start(priority=prio)

    nxt = step + (ring - 1)

    @pl.when(nxt < n_total)
    def _():
        for cp, prio in page_copies(nxt // n_steps, lax.rem(nxt, n_steps), lax.rem(nxt, ring)):
            cp.start(priority=prio)

    for cp, _ in page_copies(b, j, slot):
        cp.wait()
    pc_refs = [pc_buf.at[slot, pg] for pg in range(npg)]
    pr_refs = [pr_buf.at[slot, pg] for pg in range(npg)]

    @pl.when(j == 0)
    def _():
        m_scr[...] = jnp.full_like(m_scr, -jnp.inf)
        l_scr[...] = jnp.zeros_like(l_scr)
        acc_scr[...] = jnp.zeros_like(acc_scr)

    def keys(part):
        return slice(part * per * page, (part + 1) * per * page)

    def produce(part):
        for pg in range(part * per, (part + 1) * per):
            kc_scr[pg * page:(pg + 1) * page, :] = pc_refs[pg][...].astype(BF16)
            krt_scr[:, pg * page:(pg + 1) * page] = pr_refs[pg][...].astype(BF16)
        s = (lax.dot_general(qc, kc_scr[keys(part), :], nt, preferred_element_type=F32)
             + jnp.dot(qr, krt_scr[:, keys(part)], preferred_element_type=F32))
        s_scr[part] = s
        mb_scr[part] = jnp.max(s, axis=-1, keepdims=True)

    def consume(part):
        s = s_scr[part]
        m_old = m_scr[...]
        m_new = jnp.maximum(m_old, mb_scr[part])
        alpha = jnp.exp(m_old - m_new)
        p = jnp.exp(s - m_new)
        l_scr[...] = alpha * l_scr[...] + jnp.sum(p, axis=-1, keepdims=True)
        acc_scr[...] = alpha * acc_scr[...] + jnp.dot(p.astype(BF16), kc_scr[keys(part), :],
                                                      preferred_element_type=F32)
        m_scr[...] = m_new

    produce(0)
    for part in range(n_part):
        if part + 1 < n_part:
            produce(part + 1)
        consume(part)

    @pl.when(j == pl.num_programs(1) - 1)
    def _():
        pad = LANES - ds
        kn = jnp.concatenate([cn_ref[0], jnp.zeros((pad, cn_ref.shape[-1]), F32)], axis=0).astype(BF16)
        rn = jnp.concatenate([rn_ref[0], jnp.zeros((pad, rn_ref.shape[-1]), F32)], axis=0).astype(BF16)
        s = _scores(qc, qr, kn, rn)
        t_q = lax.broadcasted_iota(jnp.int32, s.shape, 0) & (ds - 1)
        t_k = lax.broadcasted_iota(jnp.int32, s.shape, 1)
        s = jnp.where(t_k <= t_q, s, -jnp.inf)
        _softmax_step(s, kn, m_scr, l_scr, acc_scr)
        o_ref[0] = _attn_finish(ds, wuv_ref, g_ref, l_scr, acc_scr)


def _attn_sample(page_table, qc, qr, c_new, r_new, cache_c, cache_r, w_uv, g_mla):
    db, rows, d_kv = qc.shape
    ds = c_new.shape[1]
    n_pages = page_table.shape[1]
    page = cache_c.shape[1]
    npg = math.gcd(PAGES_PER_STEP, n_pages)
    n_part = math.gcd(SAMPLE_PARTS, npg)
    assert ds & (ds - 1) == 0 and ds <= LANES
    assert cache_r.shape[1:] == (MLA_ROPE_DIM, page)
    d_out = w_uv.shape[0] * w_uv.shape[2]
    bmap = lambda b, j, pt: (b, 0, 0)
    c2 = lambda b, j, pt: (0, 0)
    c3 = lambda b, j, pt: (0, 0, 0)
    in_specs = [
        pl.BlockSpec((1, rows, d_kv), bmap),
        pl.BlockSpec((1, rows, MLA_ROPE_DIM), bmap),
        pl.BlockSpec((1, ds, d_kv), bmap),
        pl.BlockSpec((1, ds, MLA_ROPE_DIM), bmap),
        pl.BlockSpec(w_uv.shape, c3),
        pl.BlockSpec(g_mla.shape, c2),
        pl.BlockSpec(memory_space=pl.ANY),
        pl.BlockSpec(memory_space=pl.ANY),
    ]
    part_keys = (npg // n_part) * page
    grid_spec = pltpu.PrefetchScalarGridSpec(
        num_scalar_prefetch=1,
        grid=(db, n_pages // npg),
        in_specs=in_specs,
        out_specs=pl.BlockSpec((1, ds, d_out), bmap),
        scratch_shapes=[pltpu.VMEM((SAMPLE_RING, npg, page, d_kv), F32),
                        pltpu.VMEM((SAMPLE_RING, npg, MLA_ROPE_DIM, page), F32),
                        pltpu.SemaphoreType.DMA((2, SAMPLE_RING)),
                        pltpu.VMEM((npg * page, d_kv), BF16), pltpu.VMEM((MLA_ROPE_DIM, npg * page), BF16),
                        pltpu.VMEM((n_part, rows, part_keys), F32), pltpu.VMEM((n_part, rows, 1), F32),
                        pltpu.VMEM((rows, 1), F32), pltpu.VMEM((rows, 1), F32), pltpu.VMEM((rows, d_kv), F32)],
    )
    return pl.pallas_call(
        functools.partial(_attn_sample_body, npg, page, ds, n_part),
        grid_spec=grid_spec,
        out_shape=jax.ShapeDtypeStruct((db, ds, d_out), F32),
        compiler_params=_cparams(("arbitrary", "arbitrary")),
        name="attn_sample",
    )(page_table, qc, qr, c_new, r_new, w_uv, g_mla, cache_c, cache_r)


def _cmul_add(cur, sh, a_r, a_i, half):
    return cur + a_r * sh + a_i * pltpu.roll(sh, half, axis=1)


def _ssm_prompt_body(nb, n_chunks, n_levels, pre, u_ref, um_ref, wy_ref, wd_ref, wc_ref, dv_ref, ar_ref, ai_ref,
                     y_ref, hl_ref, scr, e_scr):
    u = u_ref[0]
    ub = u.astype(BF16)
    half = wd_ref.shape[-1] // 2
    wd = wd_ref[0]
    d = jnp.dot(ub, wd, preferred_element_type=F32)
    h_meta = jnp.dot(um_ref[0].astype(BF16), wd, preferred_element_type=F32)[0:1]
    first = lax.broadcasted_iota(jnp.int32, (n_chunks, 1), 0) == 0
    for b in range(nb):
        scr[b, 0:pre, :] = jnp.zeros((pre, scr.shape[-1]), F32)
        scr[b, pre:pre + n_chunks, :] = d[b * n_chunks:(b + 1) * n_chunks]
    for b in range(nb):
        shifted = scr[b, pre - 1:pre - 1 + n_chunks, :]
        scr[b, pre:pre + n_chunks, :] = shifted + jnp.where(first, h_meta, 0.0)
    for k in range(n_levels):
        s = 1 << k
        for b in range(nb):
            cur = scr[b, pre:pre + n_chunks, :]
            sh = scr[b, pre - s:pre - s + n_chunks, :]
            scr[b, pre:pre + n_chunks, :] = _cmul_add(cur, sh, ar_ref[0, k:k + 1, :], ai_ref[0, k:k + 1, :], half)
    for b in range(nb):
        e_scr[b * n_chunks:(b + 1) * n_chunks, :] = scr[b, pre:pre + n_chunks, :]
    e = e_scr[...]
    y_ref[0] = (jnp.dot(ub, wy_ref[0], preferred_element_type=F32)
                + jnp.dot(e.astype(BF16), wc_ref[0], preferred_element_type=F32)
                + u * dv_ref[0])
    h_after = _cmul_add(d, e, ar_ref[0, 0:1, :], ai_ref[0, 0:1, :], half)
    for b in range(nb):
        last = (b + 1) * n_chunks - 1
        hl_ref[0, b:b + 1, :] = h_after[last:last + 1]


def _ssm_sample_body(u_ref, h_ref, wy_ref, wd_ref, wc_ref, dv_ref, ar_ref, ai_ref, y_ref, hl_ref):
    u = u_ref[0]
    ub = u.astype(BF16)
    e = h_ref[0]
    half = e.shape[-1] // 2
    d = jnp.dot(ub, wd_ref[0], preferred_element_type=F32)
    y_ref[0] = (jnp.dot(ub, wy_ref[0], preferred_element_type=F32)
                + jnp.dot(e.astype(BF16), wc_ref[0], preferred_element_type=F32)
                + u * dv_ref[0])
    hl_ref[0] = _cmul_add(d, e, ar_ref[0, 0:1, :], ai_ref[0, 0:1, :], half)


def _to_chunk_lane_order(w, axis):
    per_slab = LANES // SSM_GROUP
    n_groups, n = w.shape[0], w.shape[axis]
    assert n % LANES == 0 and n_groups % per_slab == 0
    split = w.shape[:axis] + (n // LANES, per_slab, SSM_GROUP) + w.shape[axis + 1:]
    parts = [jnp.roll(w[gg::per_slab].reshape((n_groups // per_slab,) + split[1:]), gg, axis=axis + 1)
             for gg in range(per_slab)]
    return jnp.stack(parts, axis=1).reshape(w.shape)


def _ssm_weights(a_re, a_im, log_dt, b_re, b_im, c_re, c_im, d_skip, t_chunk, n_levels, lane_order=False):
    hi = lax.Precision.HIGHEST
    a = lax.complex(a_re.astype(F32), a_im.astype(F32))
    dt = jnp.exp(log_dt.astype(F32))[:, None]
    a_dt = a * dt
    a_bar = jnp.exp(a_dt)
    b_bar = ((a_bar - 1.0) / a)[..., None] * lax.complex(b_re.astype(F32), b_im.astype(F32))
    c = lax.complex(c_re.astype(F32), c_im.astype(F32))
    g, p_state, ch = b_bar.shape
    k = jnp.arange(t_chunk + 1, dtype=F32)
    a_pow = jnp.exp(a_dt[None] * k[:, None, None])
    kern = jnp.einsum('gcp,kgp,gpd->kgcd', c, a_pow[:t_chunk], b_bar, precision=hi).real
    s_idx = jnp.arange(t_chunk)[:, None]
    t_idx = jnp.arange(t_chunk)[None, :]
    lag = t_idx - s_idx
    wy = jnp.where((lag >= 0)[:, :, None, None, None], kern[jnp.clip(lag, 0, t_chunk - 1)], 0.0)
    wy = wy.transpose(2, 0, 4, 1, 3).reshape(g, t_chunk * ch, t_chunk * ch)
    wd = a_pow[:t_chunk][::-1][:, :, :, None] * b_bar[None]
    wd = wd.transpose(1, 0, 3, 2).reshape(g, t_chunk * ch, p_state)
    wd = jnp.concatenate([wd.real, wd.imag], axis=-1)
    gm = c[None] * a_pow[1:][:, :, None, :]
    gm = gm.transpose(1, 3, 0, 2).reshape(g, p_state, t_chunk * ch)
    wc = jnp.concatenate([gm.real, -gm.imag], axis=1)
    dv = jnp.tile(d_skip.astype(F32).reshape(g, 1, ch), (1, 1, t_chunk))
    lev = (t_chunk * (2.0 ** jnp.arange(n_levels, dtype=F32)))
    a_lev = jnp.exp(a_dt[:, None, :] * lev[None, :, None])
    a_r = jnp.concatenate([a_lev.real, a_lev.real], axis=-1)
    a_i = jnp.concatenate([-a_lev.imag, a_lev.imag], axis=-1)
    if lane_order:
        wy = _to_chunk_lane_order(_to_chunk_lane_order(wy, 1), 2)
        wd = _to_chunk_lane_order(wd, 1)
        wc = _to_chunk_lane_order(wc, 2)
        dv = _to_chunk_lane_order(dv, 2)
    return wy.astype(BF16), wd.astype(BF16), wc.astype(BF16), dv, a_r, a_i


def _ssm_prompt(u_rows, um_rows, nb, ops):
    wy, wd, wc, dv, a_r, a_i = ops
    g, r, tc = u_rows.shape
    n_chunks = r // nb
    n_levels = a_r.shape[1]
    assert (1 << n_levels) >= n_chunks and n_chunks % SUBLANES == 0
    st = wd.shape[-1]
    pre = -(-(1 << (n_levels - 1)) // SUBLANES) * SUBLANES
    gmap = lambda i: (i, 0, 0)
    return pl.pallas_call(
        functools.partial(_ssm_prompt_body, nb, n_chunks, n_levels, pre),
        grid=(g,),
        in_specs=[pl.BlockSpec((1, r, tc), gmap), pl.BlockSpec((1,) + um_rows.shape[1:], gmap),
                  pl.BlockSpec((1,) + wy.shape[1:], gmap),
                  pl.BlockSpec((1,) + wd.shape[1:], gmap), pl.BlockSpec((1,) + wc.shape[1:], gmap),
                  pl.BlockSpec((1,) + dv.shape[1:], gmap), pl.BlockSpec((1,) + a_r.shape[1:], gmap),
                  pl.BlockSpec((1,) + a_i.shape[1:], gmap)],
        out_specs=[pl.BlockSpec((1, r, tc), gmap), pl.BlockSpec((1, nb, st), gmap)],
        out_shape=[jax.ShapeDtypeStruct((g, r, tc), F32), jax.ShapeDtypeStruct((g, nb, st), F32)],
        scratch_shapes=[pltpu.VMEM((nb, pre + n_chunks, st), F32), pltpu.VMEM((r, st), F32)],
        compiler_params=_cparams(("parallel",)),
        name="ssm_prompt",
    )(u_rows, um_rows, wy, wd, wc, dv, a_r, a_i)


def _ssm_sample(u_rows, h_rows, ops):
    wy, wd, wc, dv, a_r, a_i = ops
    g, r, tc = u_rows.shape
    st = wd.shape[-1]
    gmap = lambda i: (i, 0, 0)
    return pl.pallas_call(
        _ssm_sample_body,
        grid=(g,),
        in_specs=[pl.BlockSpec((1, r, tc), gmap), pl.BlockSpec((1, r, st), gmap),
                  pl.BlockSpec((1,) + wy.shape[1:], gmap), pl.BlockSpec((1,) + wd.shape[1:], gmap),
                  pl.BlockSpec((1,) + wc.shape[1:], gmap), pl.BlockSpec((1,) + dv.shape[1:], gmap),
                  pl.BlockSpec((1,) + a_r.shape[1:], gmap), pl.BlockSpec((1,) + a_i.shape[1:], gmap)],
        out_specs=[pl.BlockSpec((1, r, tc), gmap), pl.BlockSpec((1, r, st), gmap)],
        out_shape=[jax.ShapeDtypeStruct((g, r, tc), F32), jax.ShapeDtypeStruct((g, r, st), F32)],
        compiler_params=_cparams(("parallel",)),
        name="ssm_sample",
    )(u_rows, h_rows, wy, wd, wc, dv, a_r, a_i)


def _mix_out_body(t_chunk, ys_ref, ya_ref, x_ref, wglu_ref, bglu_ref, gs_ref, wo_ref, gpost_ref, o_ref,
                  *maybe_scr):
    if t_chunk:
        (slab_scr,) = maybe_scr
        n_slabs = slab_scr.shape[0]
        n_rows = slab_scr.shape[1] // t_chunk
        per_slab = LANES // SSM_GROUP
        lane_blk = lax.broadcasted_iota(jnp.int32, (n_rows, LANES), 1) // SSM_GROUP
        for k in range(n_slabs):
            for t in range(t_chunk // per_slab):
                tiles = [ys_ref[k * per_slab + gg, :, t * LANES:(t + 1) * LANES] for gg in range(per_slab)]
                for s1 in range(per_slab):
                    w = tiles[(-s1) % per_slab]
                    for q in range(1, per_slab):
                        w = jnp.where(lane_blk == q, tiles[(q - s1) % per_slab], w)
                    if s1:
                        w = pltpu.roll(w, (per_slab - s1) * SSM_GROUP, axis=1)
                    slab_scr[k, pl.ds(t * per_slab + s1, n_rows, stride=t_chunk), :] = w
        ys = jnp.concatenate([slab_scr[k] for k in range(n_slabs)], axis=-1)
    else:
        ys = ys_ref[...]
    z = jax.nn.gelu(ys)
    gate = jax.nn.sigmoid(jnp.dot(z.astype(BF16), wglu_ref[...], preferred_element_type=F32) + bglu_ref[...])
    ns = _rms(z * gate, gs_ref[...]).astype(BF16)
    w = ns.shape[-1]
    y = (jnp.dot(ns, wo_ref[:w, :], preferred_element_type=F32)
         + jnp.dot(ya_ref[...].astype(BF16), wo_ref[w:, :], preferred_element_type=F32))
    o_ref[...] = x_ref[...] + _rms(y, gpost_ref[...])


def _mix_out(ys, ya, x, w_glu, b_glu, g_ssm, w_o, g_post, t_chunk, name):
    n, d = x.shape
    tm = _row_tile(n)
    row = lambda i: (i, 0)
    c2 = lambda i: (0, 0)
    if t_chunk:
        assert tm % (t_chunk * SUBLANES) == 0
        w = ys.shape[0] * SSM_GROUP
        assert w % LANES == 0
        ys_spec = pl.BlockSpec((ys.shape[0], tm // t_chunk, ys.shape[2]), lambda i: (0, i, 0))
        scratch = [pltpu.VMEM((w // LANES, tm, LANES), F32)]
    else:
        ys_spec = pl.BlockSpec((tm, ys.shape[1]), row)
        scratch = []
    return pl.pallas_call(
        functools.partial(_mix_out_body, t_chunk),
        grid=(n // tm,),
        in_specs=[ys_spec, pl.BlockSpec((tm, ya.shape[1]), row), pl.BlockSpec((tm, d), row),
                  pl.BlockSpec(w_glu.shape, c2), pl.BlockSpec(b_glu.shape, c2), pl.BlockSpec(g_ssm.shape, c2),
                  pl.BlockSpec(w_o.shape, c2), pl.BlockSpec(g_post.shape, c2)],
        out_specs=pl.BlockSpec((tm, d), row),
        out_shape=jax.ShapeDtypeStruct((n, d), F32),
        scratch_shapes=scratch,
        compiler_params=_cparams(("parallel",)),
        name=name,
    )(ys, ya, x, w_glu, b_glu, g_ssm, w_o, g_post)


def _rope_tables(pos):
    inv = ROPE_THETA ** (-jnp.arange(ROPE_HALF, dtype=F32) / ROPE_HALF)
    ang = pos.astype(F32)[:, None] * inv[None, :]
    reps = LANES // ROPE_HALF
    return jnp.tile(jnp.cos(ang), (1, reps)), jnp.tile(jnp.sin(ang), (1, reps))


def _regroup_uq_columns(w_uq):
    w = w_uq.reshape(w_uq.shape[0], MLA_HEADS, MLA_NOPE_DIM + MLA_ROPE_DIM)
    parts = (w[:, :, :MLA_NOPE_DIM], w[:, :, MLA_NOPE_DIM:MLA_NOPE_DIM + ROPE_HALF], w[:, :, MLA_NOPE_DIM + ROPE_HALF:])
    return jnp.concatenate([p.reshape(w_uq.shape[0], -1) for p in parts], axis=1)


def kernel(x_prompt, x_sample, cache_kv_latent, cache_k_rope, state_ssm_re, state_ssm_im, page_table, meta_tokens, g_ff1_pre, w_ff1_gate, w_ff1_up, w_ff1_down, g_ff1_post, g_mix_pre, w_in, ssm_a_re, ssm_a_im, ssm_log_dt, ssm_b_re, ssm_b_im, ssm_c_re, ssm_c_im, ssm_d, w_glu, b_glu, g_q_norm, w_uq, g_kv_norm, w_uk, w_uv, g_ssm_out, g_mla_out, w_o, g_mix_post, g_ff2_pre, w_ff2_gate, w_ff2_up, w_ff2_down, g_ff2_post):
    depth = w_in.shape[0]
    assert depth == 1, "single-layer step"
    bp, seq, d_model = x_prompt.shape
    db, ds, _ = x_sample.shape
    n_pages = page_table.shape[1]
    page = cache_kv_latent.shape[2]
    past_len = n_pages * page
    d_kv = cache_kv_latent.shape[3]
    d_q = w_uq.shape[1]
    n_groups, n_state = ssm_a_re.shape[1], ssm_a_re.shape[2]
    d_u = n_groups * SSM_GROUP
    dims = (d_u, d_q, d_kv)
    l = 0
    row = lambda v: v[l].reshape(1, -1).astype(F32)

    ff1_w = _ffn_weights(w_ff1_gate[l], w_ff1_up[l], w_ff1_down[l])
    ff2_w = _ffn_weights(w_ff2_gate[l], w_ff2_up[l], w_ff2_down[l])
    w_in_b = w_in[l].astype(BF16)
    w_uq_b = _regroup_uq_columns(w_uq[l]).astype(BF16)
    w_ukt = jnp.transpose(w_uk[l], (1, 2, 0)).astype(BF16)
    w_uv_b = jnp.transpose(w_uv[l], (1, 0, 2)).astype(BF16)
    w_uvt_b = w_uv[l].reshape(d_kv, -1).T.astype(BF16)
    w_glu_b = w_glu[l].astype(BF16)
    w_o_b = w_o[l].astype(BF16)

    xs = [x_prompt.reshape(bp * seq, d_model), x_sample.reshape(db * ds, d_model), meta_tokens.astype(F32)]
    names = ["prompt", "sample", "meta"]
    pos = [N_META + jnp.tile(jnp.arange(seq), bp), past_len + jnp.tile(jnp.arange(ds), db), jnp.arange(N_META)]

    t_p = SSM_CHUNK
    assert N_META == t_p and seq % t_p == 0
    x1, pr = [], []
    for x, nm, ps, t_rows in zip(xs, names, pos, (t_p, 0, 0)):
        y, hn = _ffn(x, row(g_ff1_pre), ff1_w, row(g_ff1_post), row(g_mix_pre), True, "ffn1_" + nm)
        cos, sin = _rope_tables(ps)
        x1.append(y)
        pr.append(_proj(hn, w_in_b, row(g_q_norm), w_uq_b, w_ukt, row(g_kv_norm), w_uvt_b, cos, sin, dims, t_rows,
                        "proj_" + nm))
    (_, ckv_p, ckvb_p, vt_p, kr_p, krb_p, qc_p, qr_p, u_rows) = pr[0]
    (u_s, ckv_s, _, _, kr_s, _, qc_s, qr_s) = pr[1]
    (u_m, ckv_m, ckvb_m, vt_m, kr_m, krb_m, _, _) = pr[2]

    n_levels = max(1, (seq // t_p - 1).bit_length())
    ssm_w = (ssm_a_re[l], ssm_a_im[l], ssm_log_dt[l], ssm_b_re[l], ssm_b_im[l], ssm_c_re[l], ssm_c_im[l], ssm_d[l])
    ops_p = _ssm_weights(*ssm_w, t_p, n_levels, True)
    ops_s = _ssm_weights(*ssm_w, ds, 1)
    tc = t_p * SSM_GROUP
    um_rows = u_m.reshape(1, t_p, n_groups, SSM_GROUP).transpose(2, 0, 1, 3).reshape(n_groups, 1, tc)
    um_rows = _to_chunk_lane_order(um_rows, 2)
    um_rows = jnp.pad(um_rows, ((0, 0), (0, SUBLANES - 1), (0, 0)))
    ys_p, hl_p = _ssm_prompt(u_rows, um_rows, bp, ops_p)
    hl_p = hl_p.transpose(1, 0, 2)

    us_rows = u_s.reshape(db, ds, n_groups, SSM_GROUP).transpose(2, 0, 1, 3).reshape(n_groups, db, ds * SSM_GROUP)
    h0_rows = jnp.concatenate([state_ssm_re[l], state_ssm_im[l]], axis=-1).astype(F32).transpose(1, 0, 2)
    ysr, hl_s = _ssm_sample(us_rows, h0_rows, ops_s)
    ys_s = ysr.reshape(n_groups, db, ds, SSM_GROUP).transpose(1, 2, 0, 3).reshape(db * ds, d_u)
    hl_s = hl_s.transpose(1, 0, 2)

    g_mla = row(g_mla_out)
    pad_m = LANES - N_META
    kmc = jnp.pad(ckvb_m, ((0, pad_m), (0, 0)))
    kmr = jnp.pad(krb_m, ((0, pad_m), (0, 0)))
    vmt = jnp.pad(vt_m, ((0, 0), (0, pad_m)))
    ya_p = _attn_prompt(qc_p, qr_p, ckvb_p.reshape(bp, seq, d_kv), vt_p,
                        krb_p.reshape(bp, seq, MLA_ROPE_DIM), kmc, vmt, kmr, g_mla)
    to_seq = lambda q: q.reshape(MLA_HEADS, db, ds, q.shape[-1]).transpose(1, 0, 2, 3).reshape(db, MLA_HEADS * ds, q.shape[-1])
    ya_s = _attn_sample(page_table, to_seq(qc_s), to_seq(qr_s), ckv_s.reshape(db, ds, d_kv),
                        kr_s.reshape(db, ds, MLA_ROPE_DIM), cache_kv_latent[l],
                        jnp.swapaxes(cache_k_rope[l], 1, 2), w_uv_b, g_mla)
    ya_s = ya_s.reshape(db * ds, -1)

    outs = []
    for x, ys, ya, t_rows, nm in ((x1[0], ys_p, ya_p, t_p, "prompt"), (x1[1], ys_s, ya_s, 0, "sample")):
        x2 = _mix_out(ys, ya, x, w_glu_b, row(b_glu), row(g_ssm_out), w_o_b, row(g_mix_post), t_rows,
                      "mix_out_" + nm)
        y, _ = _ffn(x2, row(g_ff2_pre), ff2_w, row(g_ff2_post), row(g_ff2_post), False, "ffn2_" + nm)
        outs.append(y)

    y_prompt = outs[0].reshape(bp, seq, d_model)
    y_sample = outs[1].reshape(db, ds, d_model)
    meta_b = lambda v: jnp.broadcast_to(v[None], (bp,) + v.shape)
    new_ckv_p = jnp.concatenate([meta_b(ckv_m), ckv_p.reshape(bp, seq, d_kv)], axis=1)[None]
    new_kr_p = jnp.concatenate([meta_b(kr_m), kr_p.reshape(bp, seq, MLA_ROPE_DIM)], axis=1)[None]
    return (y_prompt, y_sample, new_ckv_p, new_kr_p,
            hl_p[None, :, :, :n_state], hl_p[None, :, :, n_state:],
            ckv_s.reshape(1, db, ds, d_kv), kr_s.reshape(1, db, ds, MLA_ROPE_DIM),
            hl_s[None, :, :, :n_state], hl_s[None, :, :, n_state:])
```

```python
import functools
import math

import jax
import jax.numpy as jnp
from jax import lax
from jax.experimental import pallas as pl
from jax.experimental.pallas import tpu as pltpu

F32 = jnp.float32
BF16 = jnp.bfloat16

N_META = 16
SSM_GROUP = 16
MLA_HEADS = 8
MLA_NOPE_DIM = 64
MLA_ROPE_DIM = 32
ROPE_HALF = MLA_ROPE_DIM // 2
ROPE_THETA = 10000.0
RMS_EPS = 1e-6
ATTN_SCALE = (MLA_NOPE_DIM + MLA_ROPE_DIM) ** -0.5

LANES = 128
SUBLANES = 8
VMEM_LIMIT = 56 * 1024 * 1024

ROW_TILE = 512
FFN_ROW_TILE = 1024
FFN_EPILOGUE_ROWS = 128
FF_TILE = 256
ATTN_BQ = 256
ATTN_BK = 512
ATTN_COLS = 512
PAGES_PER_STEP = 64
SAMPLE_PARTS = 2
SAMPLE_RING = 2
SSM_CHUNK = 16


def _rms(x, g):
    return (x * lax.rsqrt(jnp.mean(x * x, axis=-1, keepdims=True) + RMS_EPS)) * g


def _row_tile(n):
    return ROW_TILE if n % ROW_TILE == 0 else n


def _cparams(sem):
    return pltpu.CompilerParams(dimension_semantics=sem, vmem_limit_bytes=VMEM_LIMIT)


def _ffn_body(emit_norm, n_tiles, ep_rows, x_ref, xp_ref, gpre_ref, wg_ref, wu_ref, wd_ref, gpost_ref, gnext_ref,
              *refs):
    if emit_norm:
        y_ref, hn_ref, h_scr, acc_scr = refs
    else:
        y_ref, h_scr, acc_scr = refs
        hn_ref = None
    i = pl.program_id(0)
    j = pl.program_id(1)
    n_ep = x_ref.shape[0] // ep_rows

    def matmul_step(p):
        h = h_scr[...]
        g = jnp.dot(h, wg_ref[...], preferred_element_type=F32)
        u = jnp.dot(h, wu_ref[...], preferred_element_type=F32)
        a = (g * jax.nn.sigmoid(g)) * u
        acc_scr[p] += jnp.dot(a.astype(BF16), wd_ref[...], preferred_element_type=F32)

    def epilogue_chunk(q):
        rows = pl.ds(pl.multiple_of(j * ep_rows, ep_rows), ep_rows)
        y = xp_ref[rows, :] + 0.5 * _rms(acc_scr[q, rows, :], gpost_ref[...])
        y_ref[rows, :] = y
        if emit_norm:
            hn_ref[rows, :] = _rms(y, gnext_ref[...]).astype(BF16)

    has_tile = i < n_tiles
    has_prev = jnp.logical_and(i >= 1, j < n_ep)
    for p in (0, 1):
        @pl.when(lax.rem(i, 2) == p)
        def _(p=p):
            @pl.when(jnp.logical_and(has_tile, j == 0))
            def _():
                h_scr[...] = _rms(x_ref[...], gpre_ref[...]).astype(BF16)
                acc_scr[p] = jnp.zeros(acc_scr.shape[1:], F32)

            @pl.when(jnp.logical_and(has_tile, has_prev))
            def _():
                epilogue_chunk(1 - p)
                matmul_step(p)

            @pl.when(jnp.logical_and(has_tile, jnp.logical_not(has_prev)))
            def _():
                matmul_step(p)

            @pl.when(jnp.logical_and(jnp.logical_not(has_tile), has_prev))
            def _():
                epilogue_chunk(1 - p)


def _ffn_weights(wg, wu, wd):
    d, d_ff = wg.shape
    tf = FF_TILE
    assert d_ff % tf == 0
    chunk = lambda w: w.astype(BF16).reshape(d, d_ff // tf, tf).transpose(1, 0, 2)
    return chunk(wg), chunk(wu), wd.astype(BF16).reshape(d_ff // tf, tf, d)


def _ffn(x, g_pre, weights, g_post, g_next, emit_norm, name):
    wg, wu, wd = weights
    n, d = x.shape
    nj, _, tf = wg.shape
    tm = FFN_ROW_TILE if n % FFN_ROW_TILE == 0 else n
    n_tiles = n // tm
    ep_rows = min(tm, FFN_EPILOGUE_ROWS)
    assert tm % ep_rows == 0 and tm // ep_rows <= nj
    cur = lambda i, j: (jnp.minimum(i, n_tiles - 1), 0)
    prev = lambda i, j: (jnp.maximum(i - 1, 0), 0)
    vec = lambda i, j: (0, 0)
    chunk = lambda i, j: (jnp.where(i < n_tiles, j, nj - 1), 0, 0)
    out_shape = [jax.ShapeDtypeStruct((n, d), F32)]
    out_specs = [pl.BlockSpec((tm, d), prev)]
    if emit_norm:
        out_shape.append(jax.ShapeDtypeStruct((n, d), BF16))
        out_specs.append(pl.BlockSpec((tm, d), prev))
    res = pl.pallas_call(
        functools.partial(_ffn_body, emit_norm, n_tiles, ep_rows),
        grid=(n_tiles + 1, nj),
        in_specs=[
            pl.BlockSpec((tm, d), cur),
            pl.BlockSpec((tm, d), prev),
            pl.BlockSpec((1, d), vec),
            pl.BlockSpec((None, d, tf), chunk),
            pl.BlockSpec((None, d, tf), chunk),
            pl.BlockSpec((None, tf, d), chunk),
            pl.BlockSpec((1, d), vec),
            pl.BlockSpec((1, d), vec),
        ],
        out_specs=out_specs,
        out_shape=out_shape,
        scratch_shapes=[pltpu.VMEM((tm, d), BF16), pltpu.VMEM((2, tm, d), F32)],
        compiler_params=_cparams(("arbitrary", "arbitrary")),
        name=name,
    )(x, x, g_pre, wg, wu, wd, g_post, g_next)
    return res if emit_norm else (res[0], None)


def _proj_body(d_u, d_q, d_kv, t_chunk, h_ref, win_ref, gq_ref, wuq_ref, wukt_ref, gkv_ref, wuvt_ref,
               cos_ref, sin_ref,
               u_ref, ckv_ref, ckvb_ref, vt_ref, kr_ref, krb_ref, qc_ref, qr_ref, *maybe_urows):
    proj = jnp.dot(h_ref[...], win_ref[...], preferred_element_type=F32)
    off_q, off_kv, off_kr = d_u, d_u + d_q, d_u + d_q + d_kv
    u_ref[...] = proj[:, :off_q]
    if t_chunk:
        urows_ref, slab_scr = maybe_urows
        n_rows = u_ref.shape[0] // t_chunk
        per_slab = LANES // SSM_GROUP
        lane_blk = lax.broadcasted_iota(jnp.int32, (n_rows, LANES), 1) // SSM_GROUP
        for k in range(d_u // LANES):
            slab_scr[k] = proj[:, k * LANES:(k + 1) * LANES]
        for k in range(d_u // LANES):
            rolled = []
            for s in range(t_chunk):
                x = slab_scr[k, pl.ds(s, n_rows, stride=t_chunk), :]
                shift = (s % per_slab) * SSM_GROUP
                rolled.append(pltpu.roll(x, shift, axis=1) if shift else x)
            for gg in range(per_slab):
                tiles = []
                for t in range(t_chunk // per_slab):
                    tile = rolled[t * per_slab]
                    for s1 in range(1, per_slab):
                        tile = jnp.where(lane_blk == (gg + s1) % per_slab, rolled[t * per_slab + s1], tile)
                    tiles.append(tile)
                urows_ref[k * per_slab + gg] = jnp.concatenate(tiles, axis=-1)
    cq = _rms(proj[:, off_q:off_kv], gq_ref[...]).astype(BF16)
    q = jnp.dot(cq, wuq_ref[...], preferred_element_type=F32) * ATTN_SCALE
    n_nope = MLA_HEADS * MLA_NOPE_DIM
    for hd in range(MLA_HEADS):
        qn = q[:, hd * MLA_NOPE_DIM:(hd + 1) * MLA_NOPE_DIM].astype(BF16)
        qc_ref[hd] = jnp.dot(qn, wukt_ref[hd], preferred_element_type=F32).astype(BF16)
    cos = cos_ref[...]
    sin = sin_ref[...]
    r1 = q[:, n_nope:n_nope + LANES]
    r2 = q[:, n_nope + LANES:n_nope + 2 * LANES]
    o1 = r1 * cos - r2 * sin
    o2 = r2 * cos + r1 * sin
    for hd in range(MLA_HEADS):
        sl = slice(hd * ROPE_HALF, (hd + 1) * ROPE_HALF)
        qr_ref[hd] = jnp.concatenate([o1[:, sl], o2[:, sl]], axis=-1).astype(BF16)
    ckv = _rms(proj[:, off_kv:off_kr], gkv_ref[...])
    ckv_ref[...] = ckv
    ckvb = ckv.astype(BF16)
    ckvb_ref[...] = ckvb
    vt_ref[...] = lax.dot_general(wuvt_ref[...], ckvb, (((1,), (1,)), ((), ())),
                                  preferred_element_type=F32).astype(BF16)
    x1 = proj[:, off_kr:off_kr + ROPE_HALF]
    x2 = proj[:, off_kr + ROPE_HALF:off_kr + MLA_ROPE_DIM]
    c16 = cos[:, :ROPE_HALF]
    s16 = sin[:, :ROPE_HALF]
    kr = jnp.concatenate([x1 * c16 - x2 * s16, x2 * c16 + x1 * s16], axis=-1)
    kr_ref[...] = kr
    krb_ref[...] = kr.astype(BF16)


def _proj(hn, w_in, g_q, w_uq, w_ukt, g_kv, w_uvt, cos, sin, dims, t_chunk, name):
    n, d = hn.shape
    d_u, d_q, d_kv = dims
    tm = _row_tile(n)
    row = lambda i: (i, 0)
    full2 = lambda i: (0, 0)
    full3 = lambda i: (0, 0, 0)
    hrow = lambda i: (0, i, 0)
    extra_specs, extra_shapes, scratch = [], [], []
    if t_chunk:
        assert tm % (t_chunk * SUBLANES) == 0 and d_u % LANES == 0
        n_groups = d_u // SSM_GROUP
        extra_specs = [pl.BlockSpec((n_groups, tm // t_chunk, t_chunk * SSM_GROUP), hrow)]
        extra_shapes = [jax.ShapeDtypeStruct((n_groups, n // t_chunk, t_chunk * SSM_GROUP), F32)]
        scratch = [pltpu.VMEM((d_u // LANES, tm, LANES), F32)]
    return pl.pallas_call(
        functools.partial(_proj_body, d_u, d_q, d_kv, t_chunk),
        grid=(n // tm,),
        in_specs=[
            pl.BlockSpec((tm, d), row),
            pl.BlockSpec(w_in.shape, full2),
            pl.BlockSpec(g_q.shape, full2),
            pl.BlockSpec(w_uq.shape, full2),
            pl.BlockSpec(w_ukt.shape, full3),
            pl.BlockSpec(g_kv.shape, full2),
            pl.BlockSpec(w_uvt.shape, full2),
            pl.BlockSpec((tm, LANES), row),
            pl.BlockSpec((tm, LANES), row),
        ],
        out_specs=[
            pl.BlockSpec((tm, d_u), row),
            pl.BlockSpec((tm, d_kv), row),
            pl.BlockSpec((tm, d_kv), row),
            pl.BlockSpec((w_uvt.shape[0], tm), lambda i: (0, i)),
            pl.BlockSpec((tm, MLA_ROPE_DIM), row),
            pl.BlockSpec((tm, MLA_ROPE_DIM), row),
            pl.BlockSpec((MLA_HEADS, tm, d_kv), hrow),
            pl.BlockSpec((MLA_HEADS, tm, MLA_ROPE_DIM), hrow),
        ] + extra_specs,
        out_shape=[
            jax.ShapeDtypeStruct((n, d_u), F32),
            jax.ShapeDtypeStruct((n, d_kv), F32),
            jax.ShapeDtypeStruct((n, d_kv), BF16),
            jax.ShapeDtypeStruct((w_uvt.shape[0], n), BF16),
            jax.ShapeDtypeStruct((n, MLA_ROPE_DIM), F32),
            jax.ShapeDtypeStruct((n, MLA_ROPE_DIM), BF16),
            jax.ShapeDtypeStruct((MLA_HEADS, n, d_kv), BF16),
            jax.ShapeDtypeStruct((MLA_HEADS, n, MLA_ROPE_DIM), BF16),
        ] + extra_shapes,
        scratch_shapes=scratch,
        compiler_params=_cparams(("parallel",)),
        name=name,
    )(hn, w_in, g_q, w_uq, w_ukt, g_kv, w_uvt, cos, sin)


def _scores(qc, qr, kc, kr):
    nt = (((1,), (1,)), ((), ()))
    return (lax.dot_general(qc, kc, nt, preferred_element_type=F32)
            + lax.dot_general(qr, kr, nt, preferred_element_type=F32))


def _softmax_step(s, kc, m_scr, l_scr, acc_scr):
    m_old = m_scr[...]
    m_new = jnp.maximum(m_old, jnp.max(s, axis=-1, keepdims=True))
    alpha = jnp.exp(m_old - m_new)
    p = jnp.exp(s - m_new)
    l_scr[...] = alpha * l_scr[...] + jnp.sum(p, axis=-1, keepdims=True)
    acc_scr[...] = alpha * acc_scr[...] + jnp.dot(p.astype(BF16), kc, preferred_element_type=F32)
    m_scr[...] = m_new


def _attn_finish(rows_per_head, wuv_ref, g_ref, l_scr, acc_scr):
    o = (acc_scr[...] / l_scr[...]).astype(BF16)
    outs = []
    for hd in range(MLA_HEADS):
        oh = o[hd * rows_per_head:(hd + 1) * rows_per_head]
        outs.append(jnp.dot(oh, wuv_ref[hd], preferred_element_type=F32))
    return _rms(jnp.concatenate(outs, axis=-1), g_ref[...])


def _attn_cols_step(parts, m_blk, cols, heads, bq, m_scr, l_scr, acc_scr):
    m_old = m_scr[:, cols]
    m_new = jnp.maximum(m_old, m_blk)
    alpha = jnp.exp(m_old - m_new)
    ps = [jnp.exp(s - m_new) for s, _ in parts]
    l_new = alpha * l_scr[:, cols]
    for p in ps:
        l_new = l_new + jnp.sum(p, axis=0, keepdims=True)
    l_scr[:, cols] = l_new
    m_scr[:, cols] = m_new
    v_dim = acc_scr.shape[0] // MLA_HEADS
    for n, hd in enumerate(heads):
        hc = slice(n * bq, (n + 1) * bq)
        rows = slice(hd * v_dim, (hd + 1) * v_dim)
        acc = alpha[:, hc] * acc_scr[rows, :]
        for p, (_, vt) in zip(ps, parts):
            acc = acc + jnp.dot(vt[rows, :], p[:, hc].astype(BF16), preferred_element_type=F32)
        acc_scr[rows, :] = acc


def _attn_prompt_body(bq, bk, ncol, qc_ref, qr_ref, kc_ref, vt_ref, kr_ref, kmc_ref, vmt_ref, kmr_ref,
                      g_ref, o_ref, m_scr, l_scr, acc_scr, s_scr, mb_scr):
    i = pl.program_id(1)
    nt = (((1,), (1,)), ((), ()))
    heads_per = ncol // bq
    n_groups = MLA_HEADS // heads_per
    groups = [slice(gi * ncol, (gi + 1) * ncol) for gi in range(n_groups)]

    def scores(gi, kc, kr):
        hs = slice(gi * heads_per, (gi + 1) * heads_per)
        qc = qc_ref[hs].reshape(ncol, qc_ref.shape[-1])
        qr = qr_ref[hs].reshape(ncol, qr_ref.shape[-1])
        return (lax.dot_general(kc, qc, nt, preferred_element_type=F32)
                + lax.dot_general(kr, qr, nt, preferred_element_type=F32))

    def produce(j, slot):
        start = pl.multiple_of(j * bk, bk)
        kc = kc_ref[0, pl.ds(start, bk), :]
        kr = kr_ref[0, pl.ds(start, bk), :]
        for gi, cols in enumerate(groups):
            s = scores(gi, kc, kr)
            s_scr[slot, :, cols] = s
            mb_scr[slot, :, cols] = jnp.max(s, axis=0, keepdims=True)

    def consume(j, slot, diagonal):
        start = pl.multiple_of(j * bk, bk)
        vt = vt_ref[:, pl.ds(start, bk)]
        for gi, cols in enumerate(groups):
            s = s_scr[slot, :, cols]
            if diagonal:
                k_pos = start + lax.broadcasted_iota(jnp.int32, s.shape, 0)
                q_pos = i * bq + (lax.broadcasted_iota(jnp.int32, s.shape, 1) & (bq - 1))
                s = jnp.where(k_pos <= q_pos, s, -jnp.inf)
                sm = scores(gi, kmc_ref[...], kmr_ref[...])
                sm = jnp.where(lax.broadcasted_iota(jnp.int32, sm.shape, 0) < N_META, sm, -jnp.inf)
                m_blk = jnp.maximum(jnp.max(s, axis=0, keepdims=True), jnp.max(sm, axis=0, keepdims=True))
                parts = [(s, vt), (sm, vmt_ref[...])]
            else:
                m_blk = mb_scr[slot, :, cols]
                parts = [(s, vt)]
            heads = range(gi * heads_per, (gi + 1) * heads_per)
            _attn_cols_step(parts, m_blk, cols, heads, bq, m_scr, l_scr, acc_scr)

    n_full = (i * bq) // bk
    m_scr[...] = jnp.full_like(m_scr, -jnp.inf)
    l_scr[...] = jnp.zeros_like(l_scr)
    acc_scr[...] = jnp.zeros_like(acc_scr)
    produce(0, 0)

    def pair(k, carry):
        j = 2 * k
        produce(j + 1, 1)
        consume(j, 0, False)
        produce(j + 2, 0)
        consume(j + 1, 1, False)
        return carry

    lax.fori_loop(0, n_full // 2, pair, 0)
    odd = (n_full & 1) == 1

    @pl.when(odd)
    def _():
        produce(n_full, 1)
        consume(n_full - 1, 0, False)
        consume(n_full, 1, True)

    @pl.when(jnp.logical_not(odd))
    def _():
        consume(n_full, 0, True)

    v_dim = acc_scr.shape[0] // MLA_HEADS
    outs = [acc_scr[hd * v_dim:(hd + 1) * v_dim, :] / l_scr[:, hd * bq:(hd + 1) * bq] for hd in range(MLA_HEADS)]
    y = jnp.transpose(jnp.concatenate(outs, axis=0))
    o_ref[...] = _rms(y, g_ref[...]).astype(o_ref.dtype)


def _attn_prompt(qc, qr, kc, vt, kr, kmc, vmt, kmr, g_mla):
    nb, seq, d_kv = kc.shape
    bq, bk, ncol = ATTN_BQ, ATTN_BK, ATTN_COLS
    assert seq % bk == 0 and bk % bq == 0 and bq & (bq - 1) == 0 and N_META >= 1
    assert ncol % bq == 0 and (MLA_HEADS * bq) % ncol == 0
    nq = seq // bq
    rows = MLA_HEADS * bq
    d_out = vt.shape[0]
    qmap = lambda b, i: (0, b * nq + i, 0)
    kmap = lambda b, i: (b, 0, 0)
    c2 = lambda b, i: (0, 0)
    return pl.pallas_call(
        functools.partial(_attn_prompt_body, bq, bk, ncol),
        grid=(nb, nq),
        in_specs=[
            pl.BlockSpec((MLA_HEADS, bq, d_kv), qmap),
            pl.BlockSpec((MLA_HEADS, bq, MLA_ROPE_DIM), qmap),
            pl.BlockSpec((1, seq, d_kv), kmap),
            pl.BlockSpec((d_out, seq), lambda b, i: (0, b)),
            pl.BlockSpec((1, seq, MLA_ROPE_DIM), kmap),
            pl.BlockSpec(kmc.shape, c2),
            pl.BlockSpec(vmt.shape, c2),
            pl.BlockSpec(kmr.shape, c2),
            pl.BlockSpec(g_mla.shape, c2),
        ],
        out_specs=pl.BlockSpec((bq, d_out), lambda b, i: (b * nq + i, 0)),
        out_shape=jax.ShapeDtypeStruct((nb * seq, d_out), BF16),
        scratch_shapes=[pltpu.VMEM((1, rows), F32), pltpu.VMEM((1, rows), F32),
                        pltpu.VMEM((d_out, bq), F32),
                        pltpu.VMEM((2, bk, rows), F32), pltpu.VMEM((2, 1, rows), F32)],
        compiler_params=_cparams(("parallel", "arbitrary")),
        name="attn_prompt",
    )(qc, qr, kc, vt, kr, kmc, vmt, kmr, g_mla)


def _attn_sample_body(npg, page, ds, n_part, pt_ref, qc_ref, qr_ref, cn_ref, rn_ref, wuv_ref, g_ref,
                      cache_c, cache_r, o_ref, pc_buf, pr_buf, sems, kc_scr, krt_scr, s_scr, mb_scr,
                      m_scr, l_scr, acc_scr):
    b = pl.program_id(0)
    j = pl.program_id(1)
    n_steps = pl.num_programs(1)
    step = b * n_steps + j
    n_total = pl.num_programs(0) * n_steps
    ring = pc_buf.shape[0]
    slot = lax.rem(step, ring)
    qc = qc_ref[0]
    qr = qr_ref[0]
    nt = (((1,), (1,)), ((), ()))
    per = npg // n_part

    def page_copies(bb, jj, sl):
        out = []
        for pg in range(npg):
            idx = pt_ref[bb, jj * npg + pg]
            out.append(pltpu.make_async_copy(cache_c.at[idx], pc_buf.at[sl, pg], sems.at[0, sl]))
            out.append(pltpu.make_async_copy(cache_r.at[idx], pr_buf.at[sl, pg], sems.at[1, sl]))
        return out

    for ahead in range(ring - 1):
        @pl.when(jnp.logical_and(step == 0, ahead < n_total))
        def _(ahead=ahead):
            for cp in page_copies(jnp.int32(ahead) // n_steps, lax.rem(jnp.int32(ahead), n_steps), ahead):
                cp.start()

    nxt = step + (ring - 1)

    @pl.when(nxt < n_total)
    def _():
        for cp in page_copies(nxt // n_steps, lax.rem(nxt, n_steps), lax.rem(nxt, ring)):
            cp.start()

    for cp in page_copies(b, j, slot):
        cp.wait()
    pc_refs = [pc_buf.at[slot, pg] for pg in range(npg)]
    pr_refs = [pr_buf.at[slot, pg] for pg in range(npg)]

    @pl.when(j == 0)
    def _():
        m_scr[...] = jnp.full_like(m_scr, -jnp.inf)
        l_scr[...] = jnp.zeros_like(l_scr)
        acc_scr[...] = jnp.zeros_like(acc_scr)

    def keys(part):
        return slice(part * per * page, (part + 1) * per * page)

    def produce(part):
        for pg in range(part * per, (part + 1) * per):
            kc_scr[pg * page:(pg + 1) * page, :] = pc_refs[pg][...].astype(BF16)
            krt_scr[:, pg * page:(pg + 1) * page] = pr_refs[pg][...].astype(BF16)
        s = (lax.dot_general(qc, kc_scr[keys(part), :], nt, preferred_element_type=F32)
             + jnp.dot(qr, krt_scr[:, keys(part)], preferred_element_type=F32))
        s_scr[part] = s
        mb_scr[part] = jnp.max(s, axis=-1, keepdims=True)

    def consume(part):
        s = s_scr[part]
        m_old = m_scr[...]
        m_new = jnp.maximum(m_old, mb_scr[part])
        alpha = jnp.exp(m_old - m_new)
        p = jnp.exp(s - m_new)
        l_scr[...] = alpha * l_scr[...] + jnp.sum(p, axis=-1, keepdims=True)
        acc_scr[...] = alpha * acc_scr[...] + jnp.dot(p.astype(BF16), kc_scr[keys(part), :],
                                                      preferred_element_type=F32)
        m_scr[...] = m_new

    produce(0)
    for part in range(n_part):
        if part + 1 < n_part:
            produce(part + 1)
        consume(part)

    @pl.when(j == pl.num_programs(1) - 1)
    def _():
        pad = LANES - ds
        kn = jnp.concatenate([cn_ref[0], jnp.zeros((pad, cn_ref.shape[-1]), F32)], axis=0).astype(BF16)
        rn = jnp.concatenate([rn_ref[0], jnp.zeros((pad, rn_ref.shape[-1]), F32)], axis=0).astype(BF16)
        s = _scores(qc, qr, kn, rn)
        t_q = lax.broadcasted_iota(jnp.int32, s.shape, 0) & (ds - 1)
        t_k = lax.broadcasted_iota(jnp.int32, s.shape, 1)
        s = jnp.where(t_k <= t_q, s, -jnp.inf)
        _softmax_step(s, kn, m_scr, l_scr, acc_scr)
        o_ref[0] = _attn_finish(ds, wuv_ref, g_ref, l_scr, acc_scr)


def _attn_sample(page_table, qc, qr, c_new, r_new, cache_c, cache_r, w_uv, g_mla):
    db, rows, d_kv = qc.shape
    ds = c_new.shape[1]
    n_pages = page_table.shape[1]
    page = cache_c.shape[1]
    npg = math.gcd(PAGES_PER_STEP, n_pages)
    n_part = math.gcd(SAMPLE_PARTS, npg)
    assert ds & (ds - 1) == 0 and ds <= LANES
    assert cache_r.shape[1:] == (MLA_ROPE_DIM, page)
    d_out = w_uv.shape[0] * w_uv.shape[2]
    bmap = lambda b, j, pt: (b, 0, 0)
    c2 = lambda b, j, pt: (0, 0)
    c3 = lambda b, j, pt: (0, 0, 0)
    in_specs = [
        pl.BlockSpec((1, rows, d_kv), bmap),
        pl.BlockSpec((1, rows, MLA_ROPE_DIM), bmap),
        pl.BlockSpec((1, ds, d_kv), bmap),
        pl.BlockSpec((1, ds, MLA_ROPE_DIM), bmap),
        pl.BlockSpec(w_uv.shape, c3),
        pl.BlockSpec(g_mla.shape, c2),
        pl.BlockSpec(memory_space=pl.ANY),
        pl.BlockSpec(memory_space=pl.ANY),
    ]
    part_keys = (npg // n_part) * page
    grid_spec = pltpu.PrefetchScalarGridSpec(
        num_scalar_prefetch=1,
        grid=(db, n_pages // npg),
        in_specs=in_specs,
        out_specs=pl.BlockSpec((1, ds, d_out), bmap),
        scratch_shapes=[pltpu.VMEM((SAMPLE_RING, npg, page, d_kv), F32),
                        pltpu.VMEM((SAMPLE_RING, npg, MLA_ROPE_DIM, page), F32),
                        pltpu.SemaphoreType.DMA((2, SAMPLE_RING)),
                        pltpu.VMEM((npg * page, d_kv), BF16), pltpu.VMEM((MLA_ROPE_DIM, npg * page), BF16),
                        pltpu.VMEM((n_part, rows, part_keys), F32), pltpu.VMEM((n_part, rows, 1), F32),
                        pltpu.VMEM((rows, 1), F32), pltpu.VMEM((rows, 1), F32), pltpu.VMEM((rows, d_kv), F32)],
    )
    return pl.pallas_call(
        functools.partial(_attn_sample_body, npg, page, ds, n_part),
        grid_spec=grid_spec,
        out_shape=jax.ShapeDtypeStruct((db, ds, d_out), F32),
        compiler_params=_cparams(("arbitrary", "arbitrary")),
        name="attn_sample",
    )(page_table, qc, qr, c_new, r_new, w_uv, g_mla, cache_c, cache_r)


def _cmul_add(cur, sh, a_r, a_i, half):
    return cur + a_r * sh + a_i * pltpu.roll(sh, half, axis=1)


def _ssm_prompt_body(nb, n_chunks, n_levels, pre, u_ref, um_ref, wy_ref, wd_ref, wc_ref, dv_ref, ar_ref, ai_ref,
                     y_ref, hl_ref, scr, e_scr):
    u = u_ref[0]
    ub = u.astype(BF16)
    half = wd_ref.shape[-1] // 2
    wd = wd_ref[0]
    d = jnp.dot(ub, wd, preferred_element_type=F32)
    h_meta = jnp.dot(um_ref[0].astype(BF16), wd, preferred_element_type=F32)[0:1]
    first = lax.broadcasted_iota(jnp.int32, (n_chunks, 1), 0) == 0
    for b in range(nb):
        scr[b, 0:pre, :] = jnp.zeros((pre, scr.shape[-1]), F32)
        scr[b, pre:pre + n_chunks, :] = d[b * n_chunks:(b + 1) * n_chunks]
    for b in range(nb):
        shifted = scr[b, pre - 1:pre - 1 + n_chunks, :]
        scr[b, pre:pre + n_chunks, :] = shifted + jnp.where(first, h_meta, 0.0)
    for k in range(n_levels):
        s = 1 << k
        for b in range(nb):
            cur = scr[b, pre:pre + n_chunks, :]
            sh = scr[b, pre - s:pre - s + n_chunks, :]
            scr[b, pre:pre + n_chunks, :] = _cmul_add(cur, sh, ar_ref[0, k:k + 1, :], ai_ref[0, k:k + 1, :], half)
    for b in range(nb):
        e_scr[b * n_chunks:(b + 1) * n_chunks, :] = scr[b, pre:pre + n_chunks, :]
    e = e_scr[...]
    y_ref[0] = (jnp.dot(ub, wy_ref[0], preferred_element_type=F32)
                + jnp.dot(e.astype(BF16), wc_ref[0], preferred_element_type=F32)
                + u * dv_ref[0])
    h_after = _cmul_add(d, e, ar_ref[0, 0:1, :], ai_ref[0, 0:1, :], half)
    for b in range(nb):
        last = (b + 1) * n_chunks - 1
        hl_ref[0, b:b + 1, :] = h_after[last:last + 1]


def _ssm_sample_body(u_ref, h_ref, wy_ref, wd_ref, wc_ref, dv_ref, ar_ref, ai_ref, y_ref, hl_ref):
    u = u_ref[0]
    ub = u.astype(BF16)
    e = h_ref[0]
    half = e.shape[-1] // 2
    d = jnp.dot(ub, wd_ref[0], preferred_element_type=F32)
    y_ref[0] = (jnp.dot(ub, wy_ref[0], preferred_element_type=F32)
                + jnp.dot(e.astype(BF16), wc_ref[0], preferred_element_type=F32)
                + u * dv_ref[0])
    hl_ref[0] = _cmul_add(d, e, ar_ref[0, 0:1, :], ai_ref[0, 0:1, :], half)


def _chunk_steps(n_groups, t_chunk, lane_order):
    j = jnp.arange(t_chunk, dtype=jnp.int32)[None, :]
    if not lane_order:
        return jnp.broadcast_to(j, (n_groups, t_chunk))
    per_slab = LANES // SSM_GROUP
    assert t_chunk % per_slab == 0
    gg = (jnp.arange(n_groups, dtype=jnp.int32) % per_slab)[:, None]
    return (j // per_slab) * per_slab + (j % per_slab - gg) % per_slab


def _ssm_weights(a_re, a_im, log_dt, b_re, b_im, c_re, c_im, d_skip, t_chunk, n_levels, lane_order=False):
    hi = lax.Precision.HIGHEST
    a = lax.complex(a_re.astype(F32), a_im.astype(F32))
    dt = jnp.exp(log_dt.astype(F32))[:, None]
    a_dt = a * dt
    a_bar = jnp.exp(a_dt)
    b_bar = ((a_bar - 1.0) / a)[..., None] * lax.complex(b_re.astype(F32), b_im.astype(F32))
    c = lax.complex(c_re.astype(F32), c_im.astype(F32))
    g, p_state, ch = b_bar.shape
    steps = _chunk_steps(g, t_chunk, lane_order)
    k = jnp.arange(t_chunk + 1, dtype=F32)
    a_pow = jnp.exp(a_dt[:, None, :] * k[None, :, None])
    pick = lambda idx: jnp.take_along_axis(a_pow, idx[:, :, None], axis=1)
    kern = jnp.einsum('gcp,gkp,gpd->gkcd', c, a_pow[:, :t_chunk], b_bar, precision=hi).real
    lag = steps[:, None, :] - steps[:, :, None]
    sel = jnp.clip(lag, 0, t_chunk - 1).reshape(g, t_chunk * t_chunk, 1, 1)
    wy = jnp.take_along_axis(kern, sel, axis=1).reshape(g, t_chunk, t_chunk, ch, ch)
    wy = jnp.where((lag >= 0)[:, :, :, None, None], wy, 0.0)
    wy = wy.transpose(0, 1, 4, 2, 3).reshape(g, t_chunk * ch, t_chunk * ch)
    wd = pick(t_chunk - 1 - steps)[:, :, None, :] * b_bar.transpose(0, 2, 1)[:, None]
    wd = wd.reshape(g, t_chunk * ch, p_state)
    wd = jnp.concatenate([wd.real, wd.imag], axis=-1)
    gm = c.transpose(0, 2, 1)[:, :, None, :] * pick(steps + 1).transpose(0, 2, 1)[:, :, :, None]
    gm = gm.reshape(g, p_state, t_chunk * ch)
    wc = jnp.concatenate([gm.real, -gm.imag], axis=1)
    dv = jnp.tile(d_skip.astype(F32).reshape(g, 1, ch), (1, 1, t_chunk))
    lev = (t_chunk * (2.0 ** jnp.arange(n_levels, dtype=F32)))
    a_lev = jnp.exp(a_dt[:, None, :] * lev[None, :, None])
    a_r = jnp.concatenate([a_lev.real, a_lev.real], axis=-1)
    a_i = jnp.concatenate([-a_lev.imag, a_lev.imag], axis=-1)
    return wy.astype(BF16), wd.astype(BF16), wc.astype(BF16), dv, a_r, a_i


def _ssm_prompt(u_rows, um_rows, nb, ops):
    wy, wd, wc, dv, a_r, a_i = ops
    g, r, tc = u_rows.shape
    n_chunks = r // nb
    n_levels = a_r.shape[1]
    assert (1 << n_levels) >= n_chunks and n_chunks % SUBLANES == 0
    st = wd.shape[-1]
    pre = -(-(1 << (n_levels - 1)) // SUBLANES) * SUBLANES
    gmap = lambda i: (i, 0, 0)
    return pl.pallas_call(
        functools.partial(_ssm_prompt_body, nb, n_chunks, n_levels, pre),
        grid=(g,),
        in_specs=[pl.BlockSpec((1, r, tc), gmap), pl.BlockSpec((1,) + um_rows.shape[1:], gmap),
                  pl.BlockSpec((1,) + wy.shape[1:], gmap),
                  pl.BlockSpec((1,) + wd.shape[1:], gmap), pl.BlockSpec((1,) + wc.shape[1:], gmap),
                  pl.BlockSpec((1,) + dv.shape[1:], gmap), pl.BlockSpec((1,) + a_r.shape[1:], gmap),
                  pl.BlockSpec((1,) + a_i.shape[1:], gmap)],
        out_specs=[pl.BlockSpec((1, r, tc), gmap), pl.BlockSpec((1, nb, st), gmap)],
        out_shape=[jax.ShapeDtypeStruct((g, r, tc), F32), jax.ShapeDtypeStruct((g, nb, st), F32)],
        scratch_shapes=[pltpu.VMEM((nb, pre + n_chunks, st), F32), pltpu.VMEM((r, st), F32)],
        compiler_params=_cparams(("parallel",)),
        name="ssm_prompt",
    )(u_rows, um_rows, wy, wd, wc, dv, a_r, a_i)


def _ssm_sample(u_rows, h_rows, ops):
    wy, wd, wc, dv, a_r, a_i = ops
    g, r, tc = u_rows.shape
    st = wd.shape[-1]
    gmap = lambda i: (i, 0, 0)
    return pl.pallas_call(
        _ssm_sample_body,
        grid=(g,),
        in_specs=[pl.BlockSpec((1, r, tc), gmap), pl.BlockSpec((1, r, st), gmap),
                  pl.BlockSpec((1,) + wy.shape[1:], gmap), pl.BlockSpec((1,) + wd.shape[1:], gmap),
                  pl.BlockSpec((1,) + wc.shape[1:], gmap), pl.BlockSpec((1,) + dv.shape[1:], gmap),
                  pl.BlockSpec((1,) + a_r.shape[1:], gmap), pl.BlockSpec((1,) + a_i.shape[1:], gmap)],
        out_specs=[pl.BlockSpec((1, r, tc), gmap), pl.BlockSpec((1, r, st), gmap)],
        out_shape=[jax.ShapeDtypeStruct((g, r, tc), F32), jax.ShapeDtypeStruct((g, r, st), F32)],
        compiler_params=_cparams(("parallel",)),
        name="ssm_sample",
    )(u_rows, h_rows, wy, wd, wc, dv, a_r, a_i)


def _mix_out_body(t_chunk, ys_ref, ya_ref, x_ref, wglu_ref, bglu_ref, gs_ref, wo_ref, gpost_ref, o_ref,
                  *maybe_scr):
    if t_chunk:
        (slab_scr,) = maybe_scr
        n_slabs = slab_scr.shape[0]
        n_rows = slab_scr.shape[1] // t_chunk
        per_slab = LANES // SSM_GROUP
        lane_blk = lax.broadcasted_iota(jnp.int32, (n_rows, LANES), 1) // SSM_GROUP
        for k in range(n_slabs):
            for t in range(t_chunk // per_slab):
                tiles = [ys_ref[k * per_slab + gg, :, t * LANES:(t + 1) * LANES] for gg in range(per_slab)]
                for s1 in range(per_slab):
                    w = tiles[(-s1) % per_slab]
                    for q in range(1, per_slab):
                        w = jnp.where(lane_blk == q, tiles[(q - s1) % per_slab], w)
                    if s1:
                        w = pltpu.roll(w, (per_slab - s1) * SSM_GROUP, axis=1)
                    slab_scr[k, pl.ds(t * per_slab + s1, n_rows, stride=t_chunk), :] = w
        ys = jnp.concatenate([slab_scr[k] for k in range(n_slabs)], axis=-1)
    else:
        ys = ys_ref[...]
    z = jax.nn.gelu(ys)
    gate = jax.nn.sigmoid(jnp.dot(z.astype(BF16), wglu_ref[...], preferred_element_type=F32) + bglu_ref[...])
    ns = _rms(z * gate, gs_ref[...]).astype(BF16)
    w = ns.shape[-1]
    y = (jnp.dot(ns, wo_ref[:w, :], preferred_element_type=F32)
         + jnp.dot(ya_ref[...].astype(BF16), wo_ref[w:, :], preferred_element_type=F32))
    o_ref[...] = x_ref[...] + _rms(y, gpost_ref[...])


def _mix_out(ys, ya, x, w_glu, b_glu, g_ssm, w_o, g_post, t_chunk, name):
    n, d = x.shape
    tm = _row_tile(n)
    row = lambda i: (i, 0)
    c2 = lambda i: (0, 0)
    if t_chunk:
        assert tm % (t_chunk * SUBLANES) == 0
        w = ys.shape[0] * SSM_GROUP
        assert w % LANES == 0
        ys_spec = pl.BlockSpec((ys.shape[0], tm // t_chunk, ys.shape[2]), lambda i: (0, i, 0))
        scratch = [pltpu.VMEM((w // LANES, tm, LANES), F32)]
    else:
        ys_spec = pl.BlockSpec((tm, ys.shape[1]), row)
        scratch = []
    return pl.pallas_call(
        functools.partial(_mix_out_body, t_chunk),
        grid=(n // tm,),
        in_specs=[ys_spec, pl.BlockSpec((tm, ya.shape[1]), row), pl.BlockSpec((tm, d), row),
                  pl.BlockSpec(w_glu.shape, c2), pl.BlockSpec(b_glu.shape, c2), pl.BlockSpec(g_ssm.shape, c2),
                  pl.BlockSpec(w_o.shape, c2), pl.BlockSpec(g_post.shape, c2)],
        out_specs=pl.BlockSpec((tm, d), row),
        out_shape=jax.ShapeDtypeStruct((n, d), F32),
        scratch_shapes=scratch,
        compiler_params=_cparams(("parallel",)),
        name=name,
    )(ys, ya, x, w_glu, b_glu, g_ssm, w_o, g_post)


def _rope_tables(pos):
    inv = ROPE_THETA ** (-jnp.arange(ROPE_HALF, dtype=F32) / ROPE_HALF)
    ang = pos.astype(F32)[:, None] * inv[None, :]
    reps = LANES // ROPE_HALF
    return jnp.tile(jnp.cos(ang), (1, reps)), jnp.tile(jnp.sin(ang), (1, reps))


def _regroup_uq_columns(w_uq):
    w = w_uq.reshape(w_uq.shape[0], MLA_HEADS, MLA_NOPE_DIM + MLA_ROPE_DIM)
    parts = (w[:, :, :MLA_NOPE_DIM], w[:, :, MLA_NOPE_DIM:MLA_NOPE_DIM + ROPE_HALF], w[:, :, MLA_NOPE_DIM + ROPE_HALF:])
    return jnp.concatenate([p.reshape(w_uq.shape[0], -1) for p in parts], axis=1)


def kernel(x_prompt, x_sample, cache_kv_latent, cache_k_rope, state_ssm_re, state_ssm_im, page_table, meta_tokens, g_ff1_pre, w_ff1_gate, w_ff1_up, w_ff1_down, g_ff1_post, g_mix_pre, w_in, ssm_a_re, ssm_a_im, ssm_log_dt, ssm_b_re, ssm_b_im, ssm_c_re, ssm_c_im, ssm_d, w_glu, b_glu, g_q_norm, w_uq, g_kv_norm, w_uk, w_uv, g_ssm_out, g_mla_out, w_o, g_mix_post, g_ff2_pre, w_ff2_gate, w_ff2_up, w_ff2_down, g_ff2_post):
    depth = w_in.shape[0]
    assert depth == 1, "single-layer step"
    bp, seq, d_model = x_prompt.shape
    db, ds, _ = x_sample.shape
    n_pages = page_table.shape[1]
    page = cache_kv_latent.shape[2]
    past_len = n_pages * page
    d_kv = cache_kv_latent.shape[3]
    d_q = w_uq.shape[1]
    n_groups, n_state = ssm_a_re.shape[1], ssm_a_re.shape[2]
    d_u = n_groups * SSM_GROUP
    dims = (d_u, d_q, d_kv)
    l = 0
    row = lambda v: v[l].reshape(1, -1).astype(F32)

    ff1_w = _ffn_weights(w_ff1_gate[l], w_ff1_up[l], w_ff1_down[l])
    ff2_w = _ffn_weights(w_ff2_gate[l], w_ff2_up[l], w_ff2_down[l])
    w_in_b = w_in[l].astype(BF16)
    w_uq_b = _regroup_uq_columns(w_uq[l]).astype(BF16)
    w_ukt = jnp.transpose(w_uk[l], (1, 2, 0)).astype(BF16)
    w_uv_b = jnp.transpose(w_uv[l], (1, 0, 2)).astype(BF16)
    w_uvt_b = w_uv[l].reshape(d_kv, -1).T.astype(BF16)
    w_glu_b = w_glu[l].astype(BF16)
    w_o_b = w_o[l].astype(BF16)

    xs = [x_prompt.reshape(bp * seq, d_model), x_sample.reshape(db * ds, d_model), meta_tokens.astype(F32)]
    names = ["prompt", "sample", "meta"]
    pos = [N_META + jnp.tile(jnp.arange(seq), bp), past_len + jnp.tile(jnp.arange(ds), db), jnp.arange(N_META)]

    t_p = SSM_CHUNK
    assert N_META == t_p and seq % t_p == 0
    x1, pr = [], []
    for x, nm, ps, t_rows in zip(xs, names, pos, (t_p, 0, 0)):
        y, hn = _ffn(x, row(g_ff1_pre), ff1_w, row(g_ff1_post), row(g_mix_pre), True, "ffn1_" + nm)
        cos, sin = _rope_tables(ps)
        x1.append(y)
        pr.append(_proj(hn, w_in_b, row(g_q_norm), w_uq_b, w_ukt, row(g_kv_norm), w_uvt_b, cos, sin, dims, t_rows,
                        "proj_" + nm))
    (_, ckv_p, ckvb_p, vt_p, kr_p, krb_p, qc_p, qr_p, u_rows) = pr[0]
    (u_s, ckv_s, _, _, kr_s, _, qc_s, qr_s) = pr[1]
    (u_m, ckv_m, ckvb_m, vt_m, kr_m, krb_m, _, _) = pr[2]

    n_levels = max(1, (seq // t_p - 1).bit_length())
    ssm_w = (ssm_a_re[l], ssm_a_im[l], ssm_log_dt[l], ssm_b_re[l], ssm_b_im[l], ssm_c_re[l], ssm_c_im[l], ssm_d[l])
    ops_p = _ssm_weights(*ssm_w, t_p, n_levels, True)
    ops_s = _ssm_weights(*ssm_w, ds, 1)
    tc = t_p * SSM_GROUP
    um_rows = u_m.reshape(t_p, n_groups, SSM_GROUP).transpose(1, 0, 2)
    um_rows = jnp.take_along_axis(um_rows, _chunk_steps(n_groups, t_p, True)[:, :, None], axis=1)
    um_rows = um_rows.reshape(n_groups, 1, tc)
    um_rows = jnp.pad(um_rows, ((0, 0), (0, SUBLANES - 1), (0, 0)))
    ys_p, hl_p = _ssm_prompt(u_rows, um_rows, bp, ops_p)
    hl_p = hl_p.transpose(1, 0, 2)

    us_rows = u_s.reshape(db, ds, n_groups, SSM_GROUP).transpose(2, 0, 1, 3).reshape(n_groups, db, ds * SSM_GROUP)
    h0_rows = jnp.concatenate([state_ssm_re[l], state_ssm_im[l]], axis=-1).astype(F32).transpose(1, 0, 2)
    ysr, hl_s = _ssm_sample(us_rows, h0_rows, ops_s)
    ys_s = ysr.reshape(n_groups, db, ds, SSM_GROUP).transpose(1, 2, 0, 3).reshape(db * ds, d_u)
    hl_s = hl_s.transpose(1, 0, 2)

    g_mla = row(g_mla_out)
    pad_m = LANES - N_META
    kmc = jnp.pad(ckvb_m, ((0, pad_m), (0, 0)))
    kmr = jnp.pad(krb_m, ((0, pad_m), (0, 0)))
    vmt = jnp.pad(vt_m, ((0, 0), (0, pad_m)))
    ya_p = _attn_prompt(qc_p, qr_p, ckvb_p.reshape(bp, seq, d_kv), vt_p,
                        krb_p.reshape(bp, seq, MLA_ROPE_DIM), kmc, vmt, kmr, g_mla)
    to_seq = lambda q: q.reshape(MLA_HEADS, db, ds, q.shape[-1]).transpose(1, 0, 2, 3).reshape(db, MLA_HEADS * ds, q.shape[-1])
    ya_s = _attn_sample(page_table, to_seq(qc_s), to_seq(qr_s), ckv_s.reshape(db, ds, d_kv),
                        kr_s.reshape(db, ds, MLA_ROPE_DIM), cache_kv_latent[l],
                        jnp.swapaxes(cache_k_rope[l], 1, 2), w_uv_b, g_mla)
    ya_s = ya_s.reshape(db * ds, -1)

    outs = []
    for x, ys, ya, t_rows, nm in ((x1[0], ys_p, ya_p, t_p, "prompt"), (x1[1], ys_s, ya_s, 0, "sample")):
        x2 = _mix_out(ys, ya, x, w_glu_b, row(b_glu), row(g_ssm_out), w_o_b, row(g_mix_post), t_rows,
                      "mix_out_" + nm)
        y, _ = _ffn(x2, row(g_ff2_pre), ff2_w, row(g_ff2_post), row(g_ff2_post), False, "ffn2_" + nm)
        outs.append(y)

    y_prompt = outs[0].reshape(bp, seq, d_model)
    y_sample = outs[1].reshape(db, ds, d_model)
    meta_b = lambda v: jnp.broadcast_to(v[None], (bp,) + v.shape)
    new_ckv_p = jnp.concatenate([meta_b(ckv_m), ckv_p.reshape(bp, seq, d_kv)], axis=1)[None]
    new_kr_p = jnp.concatenate([meta_b(kr_m), kr_p.reshape(bp, seq, MLA_ROPE_DIM)], axis=1)[None]
    return (y_prompt, y_sample, new_ckv_p, new_kr_p,
            hl_p[None, :, :, :n_state], hl_p[None, :, :, n_state:],
            ckv_s.reshape(1, db, ds, d_kv), kr_s.reshape(1, db, ds, MLA_ROPE_DIM),
            hl_s[None, :, :, :n_state], hl_s[None, :, :, n_state:])
```

```python
import functools
import math

import jax
import jax.numpy as jnp
from jax import lax
from jax.experimental import pallas as pl
from jax.experimental.pallas import tpu as pltpu

F32 = jnp.float32
BF16 = jnp.bfloat16

N_META = 16
SSM_GROUP = 16
MLA_HEADS = 8
MLA_NOPE_DIM = 64
MLA_ROPE_DIM = 32
ROPE_HALF = MLA_ROPE_DIM // 2
ROPE_THETA = 10000.0
RMS_EPS = 1e-6
ATTN_SCALE = (MLA_NOPE_DIM + MLA_ROPE_DIM) ** -0.5

LANES = 128
SUBLANES = 8
VMEM_LIMIT = 56 * 1024 * 1024

ROW_TILE = 512
FFN_ROW_TILE = 1024
FF_TILE = 256
ATTN_BQ = 256
ATTN_BK = 512
ATTN_COLS = 512
PAGES_PER_STEP = 64
SAMPLE_PARTS = 2
SAMPLE_RING = 2
SSM_CHUNK = 16


def _rms(x, g):
    return (x * lax.rsqrt(jnp.mean(x * x, axis=-1, keepdims=True) + RMS_EPS)) * g


def _row_tile(n):
    return ROW_TILE if n % ROW_TILE == 0 else n


def _cparams(sem):
    return pltpu.CompilerParams(dimension_semantics=sem, vmem_limit_bytes=VMEM_LIMIT)


def _ffn_body(emit_norm, x_ref, gpre_ref, wg_ref, wu_ref, wd_ref, gpost_ref, gnext_ref, *refs):
    if emit_norm:
        y_ref, hn_ref, h_scr, acc_scr = refs
    else:
        y_ref, h_scr, acc_scr = refs
        hn_ref = None
    j = pl.program_id(1)

    @pl.when(j == 0)
    def _():
        h_scr[...] = _rms(x_ref[...], gpre_ref[...]).astype(BF16)
        acc_scr[...] = jnp.zeros_like(acc_scr)

    h = h_scr[...]
    g = jnp.dot(h, wg_ref[...], preferred_element_type=F32)
    u = jnp.dot(h, wu_ref[...], preferred_element_type=F32)
    a = (g * jax.nn.sigmoid(g)) * u
    acc_scr[...] += jnp.dot(a.astype(BF16), wd_ref[...], preferred_element_type=F32)

    @pl.when(j == pl.num_programs(1) - 1)
    def _():
        y = x_ref[...] + 0.5 * _rms(acc_scr[...], gpost_ref[...])
        y_ref[...] = y
        if emit_norm:
            hn_ref[...] = _rms(y, gnext_ref[...]).astype(BF16)


def _ffn_weights(wg, wu, wd):
    d, d_ff = wg.shape
    tf = FF_TILE
    assert d_ff % tf == 0
    chunk = lambda w: w.astype(BF16).reshape(d, d_ff // tf, tf).transpose(1, 0, 2)
    return chunk(wg), chunk(wu), wd.astype(BF16).reshape(d_ff // tf, tf, d)


def _ffn(x, g_pre, weights, g_post, g_next, emit_norm, name):
    wg, wu, wd = weights
    n, d = x.shape
    nj, _, tf = wg.shape
    tm = FFN_ROW_TILE if n % FFN_ROW_TILE == 0 else n
    row = lambda i, j: (i, 0)
    vec = lambda i, j: (0, 0)
    chunk = lambda i, j: (j, 0, 0)
    out_shape = [jax.ShapeDtypeStruct((n, d), F32)]
    out_specs = [pl.BlockSpec((tm, d), row)]
    if emit_norm:
        out_shape.append(jax.ShapeDtypeStruct((n, d), BF16))
        out_specs.append(pl.BlockSpec((tm, d), row))
    res = pl.pallas_call(
        functools.partial(_ffn_body, emit_norm),
        grid=(n // tm, nj),
        in_specs=[
            pl.BlockSpec((tm, d), row),
            pl.BlockSpec((1, d), vec),
            pl.BlockSpec((None, d, tf), chunk),
            pl.BlockSpec((None, d, tf), chunk),
            pl.BlockSpec((None, tf, d), chunk),
            pl.BlockSpec((1, d), vec),
            pl.BlockSpec((1, d), vec),
        ],
        out_specs=out_specs,
        out_shape=out_shape,
        scratch_shapes=[pltpu.VMEM((tm, d), BF16), pltpu.VMEM((tm, d), F32)],
        compiler_params=_cparams(("parallel", "arbitrary")),
        name=name,
    )(x, g_pre, wg, wu, wd, g_post, g_next)
    return res if emit_norm else (res[0], None)


def _proj_body(d_u, d_q, d_kv, t_chunk, h_ref, win_ref, gq_ref, wuq_ref, wukt_ref, gkv_ref, wuvt_ref,
               cos_ref, sin_ref,
               u_ref, ckv_ref, ckvb_ref, vt_ref, kr_ref, krb_ref, qc_ref, qr_ref, *maybe_urows):
    proj = jnp.dot(h_ref[...], win_ref[...], preferred_element_type=F32)
    off_q, off_kv, off_kr = d_u, d_u + d_q, d_u + d_q + d_kv
    u_ref[...] = proj[:, :off_q]
    if t_chunk:
        urows_ref, slab_scr = maybe_urows
        n_rows = u_ref.shape[0] // t_chunk
        per_slab = LANES // SSM_GROUP
        lane_blk = lax.broadcasted_iota(jnp.int32, (n_rows, LANES), 1) // SSM_GROUP
        for k in range(d_u // LANES):
            slab_scr[k] = proj[:, k * LANES:(k + 1) * LANES]
        for k in range(d_u // LANES):
            rolled = []
            for s in range(t_chunk):
                x = slab_scr[k, pl.ds(s, n_rows, stride=t_chunk), :]
                shift = (s % per_slab) * SSM_GROUP
                rolled.append(pltpu.roll(x, shift, axis=1) if shift else x)
            for gg in range(per_slab):
                tiles = []
                for t in range(t_chunk // per_slab):
                    tile = rolled[t * per_slab]
                    for s1 in range(1, per_slab):
                        tile = jnp.where(lane_blk == (gg + s1) % per_slab, rolled[t * per_slab + s1], tile)
                    tiles.append(tile)
                urows_ref[k * per_slab + gg] = jnp.concatenate(tiles, axis=-1)
    cq = _rms(proj[:, off_q:off_kv], gq_ref[...]).astype(BF16)
    q = jnp.dot(cq, wuq_ref[...], preferred_element_type=F32) * ATTN_SCALE
    n_nope = MLA_HEADS * MLA_NOPE_DIM
    for hd in range(MLA_HEADS):
        qn = q[:, hd * MLA_NOPE_DIM:(hd + 1) * MLA_NOPE_DIM].astype(BF16)
        qc_ref[hd] = jnp.dot(qn, wukt_ref[hd], preferred_element_type=F32).astype(BF16)
    cos = cos_ref[...]
    sin = sin_ref[...]
    r1 = q[:, n_nope:n_nope + LANES]
    r2 = q[:, n_nope + LANES:n_nope + 2 * LANES]
    o1 = r1 * cos - r2 * sin
    o2 = r2 * cos + r1 * sin
    for hd in range(MLA_HEADS):
        sl = slice(hd * ROPE_HALF, (hd + 1) * ROPE_HALF)
        qr_ref[hd] = jnp.concatenate([o1[:, sl], o2[:, sl]], axis=-1).astype(BF16)
    ckv = _rms(proj[:, off_kv:off_kr], gkv_ref[...])
    ckv_ref[...] = ckv
    ckvb = ckv.astype(BF16)
    ckvb_ref[...] = ckvb
    vt_ref[...] = lax.dot_general(wuvt_ref[...], ckvb, (((1,), (1,)), ((), ())),
                                  preferred_element_type=F32).astype(BF16)
    x1 = proj[:, off_kr:off_kr + ROPE_HALF]
    x2 = proj[:, off_kr + ROPE_HALF:off_kr + MLA_ROPE_DIM]
    c16 = cos[:, :ROPE_HALF]
    s16 = sin[:, :ROPE_HALF]
    kr = jnp.concatenate([x1 * c16 - x2 * s16, x2 * c16 + x1 * s16], axis=-1)
    kr_ref[...] = kr
    krb_ref[...] = kr.astype(BF16)


def _proj(hn, w_in, g_q, w_uq, w_ukt, g_kv, w_uvt, cos, sin, dims, t_chunk, name):
    n, d = hn.shape
    d_u, d_q, d_kv = dims
    tm = _row_tile(n)
    row = lambda i: (i, 0)
    full2 = lambda i: (0, 0)
    full3 = lambda i: (0, 0, 0)
    hrow = lambda i: (0, i, 0)
    extra_specs, extra_shapes, scratch = [], [], []
    if t_chunk:
        assert tm % (t_chunk * SUBLANES) == 0 and d_u % LANES == 0
        n_groups = d_u // SSM_GROUP
        extra_specs = [pl.BlockSpec((n_groups, tm // t_chunk, t_chunk * SSM_GROUP), hrow)]
        extra_shapes = [jax.ShapeDtypeStruct((n_groups, n // t_chunk, t_chunk * SSM_GROUP), F32)]
        scratch = [pltpu.VMEM((d_u // LANES, tm, LANES), F32)]
    return pl.pallas_call(
        functools.partial(_proj_body, d_u, d_q, d_kv, t_chunk),
        grid=(n // tm,),
        in_specs=[
            pl.BlockSpec((tm, d), row),
            pl.BlockSpec(w_in.shape, full2),
            pl.BlockSpec(g_q.shape, full2),
            pl.BlockSpec(w_uq.shape, full2),
            pl.BlockSpec(w_ukt.shape, full3),
            pl.BlockSpec(g_kv.shape, full2),
            pl.BlockSpec(w_uvt.shape, full2),
            pl.BlockSpec((tm, LANES), row),
            pl.BlockSpec((tm, LANES), row),
        ],
        out_specs=[
            pl.BlockSpec((tm, d_u), row),
            pl.BlockSpec((tm, d_kv), row),
            pl.BlockSpec((tm, d_kv), row),
            pl.BlockSpec((w_uvt.shape[0], tm), lambda i: (0, i)),
            pl.BlockSpec((tm, MLA_ROPE_DIM), row),
            pl.BlockSpec((tm, MLA_ROPE_DIM), row),
            pl.BlockSpec((MLA_HEADS, tm, d_kv), hrow),
            pl.BlockSpec((MLA_HEADS, tm, MLA_ROPE_DIM), hrow),
        ] + extra_specs,
        out_shape=[
            jax.ShapeDtypeStruct((n, d_u), F32),
            jax.ShapeDtypeStruct((n, d_kv), F32),
            jax.ShapeDtypeStruct((n, d_kv), BF16),
            jax.ShapeDtypeStruct((w_uvt.shape[0], n), BF16),
            jax.ShapeDtypeStruct((n, MLA_ROPE_DIM), F32),
            jax.ShapeDtypeStruct((n, MLA_ROPE_DIM), BF16),
            jax.ShapeDtypeStruct((MLA_HEADS, n, d_kv), BF16),
            jax.ShapeDtypeStruct((MLA_HEADS, n, MLA_ROPE_DIM), BF16),
        ] + extra_shapes,
        scratch_shapes=scratch,
        compiler_params=_cparams(("parallel",)),
        name=name,
    )(hn, w_in, g_q, w_uq, w_ukt, g_kv, w_uvt, cos, sin)


def _scores(qc, qr, kc, kr):
    nt = (((1,), (1,)), ((), ()))
    return (lax.dot_general(qc, kc, nt, preferred_element_type=F32)
            + lax.dot_general(qr, kr, nt, preferred_element_type=F32))


def _softmax_step(s, kc, m_scr, l_scr, acc_scr):
    m_old = m_scr[...]
    m_new = jnp.maximum(m_old, jnp.max(s, axis=-1, keepdims=True))
    alpha = jnp.exp(m_old - m_new)
    p = jnp.exp(s - m_new)
    l_scr[...] = alpha * l_scr[...] + jnp.sum(p, axis=-1, keepdims=True)
    acc_scr[...] = alpha * acc_scr[...] + jnp.dot(p.astype(BF16), kc, preferred_element_type=F32)
    m_scr[...] = m_new


def _attn_finish(rows_per_head, wuv_ref, g_ref, l_scr, acc_scr):
    o = (acc_scr[...] / l_scr[...]).astype(BF16)
    outs = []
    for hd in range(MLA_HEADS):
        oh = o[hd * rows_per_head:(hd + 1) * rows_per_head]
        outs.append(jnp.dot(oh, wuv_ref[hd], preferred_element_type=F32))
    return _rms(jnp.concatenate(outs, axis=-1), g_ref[...])


def _attn_cols_step(parts, m_blk, cols, heads, bq, m_scr, l_scr, acc_scr):
    m_old = m_scr[:, cols]
    m_new = jnp.maximum(m_old, m_blk)
    alpha = jnp.exp(m_old - m_new)
    ps = [jnp.exp(s - m_new) for s, _ in parts]
    l_new = alpha * l_scr[:, cols]
    for p in ps:
        l_new = l_new + jnp.sum(p, axis=0, keepdims=True)
    l_scr[:, cols] = l_new
    m_scr[:, cols] = m_new
    v_dim = acc_scr.shape[0] // MLA_HEADS
    for n, hd in enumerate(heads):
        hc = slice(n * bq, (n + 1) * bq)
        rows = slice(hd * v_dim, (hd + 1) * v_dim)
        acc = alpha[:, hc] * acc_scr[rows, :]
        for p, (_, vt) in zip(ps, parts):
            acc = acc + jnp.dot(vt[rows, :], p[:, hc].astype(BF16), preferred_element_type=F32)
        acc_scr[rows, :] = acc


def _attn_prompt_body(bq, bk, ncol, qc_ref, qr_ref, kc_ref, vt_ref, kr_ref, kmc_ref, vmt_ref, kmr_ref,
                      g_ref, o_ref, m_scr, l_scr, acc_scr, s_scr, mb_scr):
    i = pl.program_id(1)
    nt = (((1,), (1,)), ((), ()))
    heads_per = ncol // bq
    n_groups = MLA_HEADS // heads_per
    groups = [slice(gi * ncol, (gi + 1) * ncol) for gi in range(n_groups)]

    def scores(gi, kc, kr):
        hs = slice(gi * heads_per, (gi + 1) * heads_per)
        qc = qc_ref[hs].reshape(ncol, qc_ref.shape[-1])
        qr = qr_ref[hs].reshape(ncol, qr_ref.shape[-1])
        return (lax.dot_general(kc, qc, nt, preferred_element_type=F32)
                + lax.dot_general(kr, qr, nt, preferred_element_type=F32))

    def produce(j, slot):
        start = pl.multiple_of(j * bk, bk)
        kc = kc_ref[0, pl.ds(start, bk), :]
        kr = kr_ref[0, pl.ds(start, bk), :]
        for gi, cols in enumerate(groups):
            s = scores(gi, kc, kr)
            s_scr[slot, :, cols] = s
            mb_scr[slot, :, cols] = jnp.max(s, axis=0, keepdims=True)

    def consume(j, slot, diagonal):
        start = pl.multiple_of(j * bk, bk)
        vt = vt_ref[:, pl.ds(start, bk)]
        for gi, cols in enumerate(groups):
            s = s_scr[slot, :, cols]
            if diagonal:
                k_pos = start + lax.broadcasted_iota(jnp.int32, s.shape, 0)
                q_pos = i * bq + (lax.broadcasted_iota(jnp.int32, s.shape, 1) & (bq - 1))
                s = jnp.where(k_pos <= q_pos, s, -jnp.inf)
                sm = scores(gi, kmc_ref[...], kmr_ref[...])
                sm = jnp.where(lax.broadcasted_iota(jnp.int32, sm.shape, 0) < N_META, sm, -jnp.inf)
                m_blk = jnp.maximum(jnp.max(s, axis=0, keepdims=True), jnp.max(sm, axis=0, keepdims=True))
                parts = [(s, vt), (sm, vmt_ref[...])]
            else:
                m_blk = mb_scr[slot, :, cols]
                parts = [(s, vt)]
            heads = range(gi * heads_per, (gi + 1) * heads_per)
            _attn_cols_step(parts, m_blk, cols, heads, bq, m_scr, l_scr, acc_scr)

    n_full = (i * bq) // bk
    m_scr[...] = jnp.full_like(m_scr, -jnp.inf)
    l_scr[...] = jnp.zeros_like(l_scr)
    acc_scr[...] = jnp.zeros_like(acc_scr)
    produce(0, 0)

    def pair(k, carry):
        j = 2 * k
        produce(j + 1, 1)
        consume(j, 0, False)
        produce(j + 2, 0)
        consume(j + 1, 1, False)
        return carry

    lax.fori_loop(0, n_full // 2, pair, 0)
    odd = (n_full & 1) == 1

    @pl.when(odd)
    def _():
        produce(n_full, 1)
        consume(n_full - 1, 0, False)
        consume(n_full, 1, True)

    @pl.when(jnp.logical_not(odd))
    def _():
        consume(n_full, 0, True)

    v_dim = acc_scr.shape[0] // MLA_HEADS
    outs = [acc_scr[hd * v_dim:(hd + 1) * v_dim, :] / l_scr[:, hd * bq:(hd + 1) * bq] for hd in range(MLA_HEADS)]
    y = jnp.transpose(jnp.concatenate(outs, axis=0))
    o_ref[...] = _rms(y, g_ref[...]).astype(o_ref.dtype)


def _attn_prompt(qc, qr, kc, vt, kr, kmc, vmt, kmr, g_mla):
    nb, seq, d_kv = kc.shape
    bq, bk, ncol = ATTN_BQ, ATTN_BK, ATTN_COLS
    assert seq % bk == 0 and bk % bq == 0 and bq & (bq - 1) == 0 and N_META >= 1
    assert ncol % bq == 0 and (MLA_HEADS * bq) % ncol == 0
    nq = seq // bq
    rows = MLA_HEADS * bq
    d_out = vt.shape[0]
    qmap = lambda b, i: (0, b * nq + i, 0)
    kmap = lambda b, i: (b, 0, 0)
    c2 = lambda b, i: (0, 0)
    return pl.pallas_call(
        functools.partial(_attn_prompt_body, bq, bk, ncol),
        grid=(nb, nq),
        in_specs=[
            pl.BlockSpec((MLA_HEADS, bq, d_kv), qmap),
            pl.BlockSpec((MLA_HEADS, bq, MLA_ROPE_DIM), qmap),
            pl.BlockSpec((1, seq, d_kv), kmap),
            pl.BlockSpec((d_out, seq), lambda b, i: (0, b)),
            pl.BlockSpec((1, seq, MLA_ROPE_DIM), kmap),
            pl.BlockSpec(kmc.shape, c2),
            pl.BlockSpec(vmt.shape, c2),
            pl.BlockSpec(kmr.shape, c2),
            pl.BlockSpec(g_mla.shape, c2),
        ],
        out_specs=pl.BlockSpec((bq, d_out), lambda b, i: (b * nq + i, 0)),
        out_shape=jax.ShapeDtypeStruct((nb * seq, d_out), BF16),
        scratch_shapes=[pltpu.VMEM((1, rows), F32), pltpu.VMEM((1, rows), F32),
                        pltpu.VMEM((d_out, bq), F32),
                        pltpu.VMEM((2, bk, rows), F32), pltpu.VMEM((2, 1, rows), F32)],
        compiler_params=_cparams(("parallel", "arbitrary")),
        name="attn_prompt",
    )(qc, qr, kc, vt, kr, kmc, vmt, kmr, g_mla)


def _attn_sample_body(npg, page, ds, n_part, pt_ref, qc_ref, qr_ref, cn_ref, rn_ref, wuv_ref, g_ref,
                      cache_c, cache_r, o_ref, pc_buf, pr_buf, sems, kc_scr, krt_scr, s_scr, mb_scr,
                      m_scr, l_scr, acc_scr):
    b = pl.program_id(0)
    j = pl.program_id(1)
    n_steps = pl.num_programs(1)
    step = b * n_steps + j
    n_total = pl.num_programs(0) * n_steps
    ring = pc_buf.shape[0]
    slot = lax.rem(step, ring)
    qc = qc_ref[0]
    qr = qr_ref[0]
    nt = (((1,), (1,)), ((), ()))
    per = npg // n_part

    def page_copies(bb, jj, sl):
        out = []
        for pg in range(npg):
            idx = pt_ref[bb, jj * npg + pg]
            out.append(pltpu.make_async_copy(cache_c.at[idx], pc_buf.at[sl, pg], sems.at[0, sl]))
            out.append(pltpu.make_async_copy(cache_r.at[idx], pr_buf.at[sl, pg], sems.at[1, sl]))
        return out

    for ahead in range(ring - 1):
        @pl.when(jnp.logical_and(step == 0, ahead < n_total))
        def _(ahead=ahead):
            for cp in page_copies(jnp.int32(ahead) // n_steps, lax.rem(jnp.int32(ahead), n_steps), ahead):
                cp.start()

    nxt = step + (ring - 1)

    @pl.when(nxt < n_total)
    def _():
        for cp in page_copies(nxt // n_steps, lax.rem(nxt, n_steps), lax.rem(nxt, ring)):
            cp.start()

    for cp in page_copies(b, j, slot):
        cp.wait()
    pc_refs = [pc_buf.at[slot, pg] for pg in range(npg)]
    pr_refs = [pr_buf.at[slot, pg] for pg in range(npg)]

    @pl.when(j == 0)
    def _():
        m_scr[...] = jnp.full_like(m_scr, -jnp.inf)
        l_scr[...] = jnp.zeros_like(l_scr)
        acc_scr[...] = jnp.zeros_like(acc_scr)

    def keys(part):
        return slice(part * per * page, (part + 1) * per * page)

    def produce(part):
        for pg in range(part * per, (part + 1) * per):
            kc_scr[pg * page:(pg + 1) * page, :] = pc_refs[pg][...].astype(BF16)
            krt_scr[:, pg * page:(pg + 1) * page] = pr_refs[pg][...].astype(BF16)
        s = (lax.dot_general(qc, kc_scr[keys(part), :], nt, preferred_element_type=F32)
             + jnp.dot(qr, krt_scr[:, keys(part)], preferred_element_type=F32))
        s_scr[part] = s
        mb_scr[part] = jnp.max(s, axis=-1, keepdims=True)

    def consume(part):
        s = s_scr[part]
        m_old = m_scr[...]
        m_new = jnp.maximum(m_old, mb_scr[part])
        alpha = jnp.exp(m_old - m_new)
        p = jnp.exp(s - m_new)
        l_scr[...] = alpha * l_scr[...] + jnp.sum(p, axis=-1, keepdims=True)
        acc_scr[...] = alpha * acc_scr[...] + jnp.dot(p.astype(BF16), kc_scr[keys(part), :],
                                                      preferred_element_type=F32)
        m_scr[...] = m_new

    produce(0)
    for part in range(n_part):
        if part + 1 < n_part:
            produce(part + 1)
        consume(part)

    @pl.when(j == pl.num_programs(1) - 1)
    def _():
        pad = LANES - ds
        kn = jnp.concatenate([cn_ref[0], jnp.zeros((pad, cn_ref.shape[-1]), F32)], axis=0).astype(BF16)
        rn = jnp.concatenate([rn_ref[0], jnp.zeros((pad, rn_ref.shape[-1]), F32)], axis=0).astype(BF16)
        s = _scores(qc, qr, kn, rn)
        t_q = lax.broadcasted_iota(jnp.int32, s.shape, 0) & (ds - 1)
        t_k = lax.broadcasted_iota(jnp.int32, s.shape, 1)
        s = jnp.where(t_k <= t_q, s, -jnp.inf)
        _softmax_step(s, kn, m_scr, l_scr, acc_scr)
        o_ref[0] = _attn_finish(ds, wuv_ref, g_ref, l_scr, acc_scr)


def _attn_sample(page_table, qc, qr, c_new, r_new, cache_c, cache_r, w_uv, g_mla):
    db, rows, d_kv = qc.shape
    ds = c_new.shape[1]
    n_pages = page_table.shape[1]
    page = cache_c.shape[1]
    npg = math.gcd(PAGES_PER_STEP, n_pages)
    n_part = math.gcd(SAMPLE_PARTS, npg)
    assert ds & (ds - 1) == 0 and ds <= LANES
    assert cache_r.shape[1:] == (MLA_ROPE_DIM, page)
    d_out = w_uv.shape[0] * w_uv.shape[2]
    bmap = lambda b, j, pt: (b, 0, 0)
    c2 = lambda b, j, pt: (0, 0)
    c3 = lambda b, j, pt: (0, 0, 0)
    in_specs = [
        pl.BlockSpec((1, rows, d_kv), bmap),
        pl.BlockSpec((1, rows, MLA_ROPE_DIM), bmap),
        pl.BlockSpec((1, ds, d_kv), bmap),
        pl.BlockSpec((1, ds, MLA_ROPE_DIM), bmap),
        pl.BlockSpec(w_uv.shape, c3),
        pl.BlockSpec(g_mla.shape, c2),
        pl.BlockSpec(memory_space=pl.ANY),
        pl.BlockSpec(memory_space=pl.ANY),
    ]
    part_keys = (npg // n_part) * page
    grid_spec = pltpu.PrefetchScalarGridSpec(
        num_scalar_prefetch=1,
        grid=(db, n_pages // npg),
        in_specs=in_specs,
        out_specs=pl.BlockSpec((1, ds, d_out), bmap),
        scratch_shapes=[pltpu.VMEM((SAMPLE_RING, npg, page, d_kv), F32),
                        pltpu.VMEM((SAMPLE_RING, npg, MLA_ROPE_DIM, page), F32),
                        pltpu.SemaphoreType.DMA((2, SAMPLE_RING)),
                        pltpu.VMEM((npg * page, d_kv), BF16), pltpu.VMEM((MLA_ROPE_DIM, npg * page), BF16),
                        pltpu.VMEM((n_part, rows, part_keys), F32), pltpu.VMEM((n_part, rows, 1), F32),
                        pltpu.VMEM((rows, 1), F32), pltpu.VMEM((rows, 1), F32), pltpu.VMEM((rows, d_kv), F32)],
    )
    return pl.pallas_call(
        functools.partial(_attn_sample_body, npg, page, ds, n_part),
        grid_spec=grid_spec,
        out_shape=jax.ShapeDtypeStruct((db, ds, d_out), F32),
        compiler_params=_cparams(("arbitrary", "arbitrary")),
        name="attn_sample",
    )(page_table, qc, qr, c_new, r_new, w_uv, g_mla, cache_c, cache_r)


def _cmul_add(cur, sh, a_r, a_i, half):
    return cur + a_r * sh + a_i * pltpu.roll(sh, half, axis=1)


def _ssm_prompt_body(nb, n_chunks, n_levels, pre, u_ref, um_ref, wy_ref, wd_ref, wc_ref, dv_ref, ar_ref, ai_ref,
                     y_ref, hl_ref, scr, e_scr):
    u = u_ref[0]
    ub = u.astype(BF16)
    half = wd_ref.shape[-1] // 2
    wd = wd_ref[0]
    d = jnp.dot(ub, wd, preferred_element_type=F32)
    h_meta = jnp.dot(um_ref[0].astype(BF16), wd, preferred_element_type=F32)[0:1]
    first = lax.broadcasted_iota(jnp.int32, (n_chunks, 1), 0) == 0
    for b in range(nb):
        scr[b, 0:pre, :] = jnp.zeros((pre, scr.shape[-1]), F32)
        scr[b, pre:pre + n_chunks, :] = d[b * n_chunks:(b + 1) * n_chunks]
    for b in range(nb):
        shifted = scr[b, pre - 1:pre - 1 + n_chunks, :]
        scr[b, pre:pre + n_chunks, :] = shifted + jnp.where(first, h_meta, 0.0)
    for k in range(n_levels):
        s = 1 << k
        for b in range(nb):
            cur = scr[b, pre:pre + n_chunks, :]
            sh = scr[b, pre - s:pre - s + n_chunks, :]
            scr[b, pre:pre + n_chunks, :] = _cmul_add(cur, sh, ar_ref[0, k:k + 1, :], ai_ref[0, k:k + 1, :], half)
    for b in range(nb):
        e_scr[b * n_chunks:(b + 1) * n_chunks, :] = scr[b, pre:pre + n_chunks, :]
    e = e_scr[...]
    y_ref[0] = (jnp.dot(ub, wy_ref[0], preferred_element_type=F32)
                + jnp.dot(e.astype(BF16), wc_ref[0], preferred_element_type=F32)
                + u * dv_ref[0])
    h_after = _cmul_add(d, e, ar_ref[0, 0:1, :], ai_ref[0, 0:1, :], half)
    for b in range(nb):
        last = (b + 1) * n_chunks - 1
        hl_ref[0, b:b + 1, :] = h_after[last:last + 1]


def _ssm_sample_body(u_ref, h_ref, wy_ref, wd_ref, wc_ref, dv_ref, ar_ref, ai_ref, y_ref, hl_ref):
    u = u_ref[0]
    ub = u.astype(BF16)
    e = h_ref[0]
    half = e.shape[-1] // 2
    d = jnp.dot(ub, wd_ref[0], preferred_element_type=F32)
    y_ref[0] = (jnp.dot(ub, wy_ref[0], preferred_element_type=F32)
                + jnp.dot(e.astype(BF16), wc_ref[0], preferred_element_type=F32)
                + u * dv_ref[0])
    hl_ref[0] = _cmul_add(d, e, ar_ref[0, 0:1, :], ai_ref[0, 0:1, :], half)


def _chunk_steps(n_groups, t_chunk, lane_order):
    j = jnp.arange(t_chunk, dtype=jnp.int32)[None, :]
    if not lane_order:
        return jnp.broadcast_to(j, (n_groups, t_chunk))
    per_slab = LANES // SSM_GROUP
    assert t_chunk % per_slab == 0
    gg = (jnp.arange(n_groups, dtype=jnp.int32) % per_slab)[:, None]
    return (j // per_slab) * per_slab + (j % per_slab - gg) % per_slab


def _ssm_weights(a_re, a_im, log_dt, b_re, b_im, c_re, c_im, d_skip, t_chunk, n_levels, lane_order=False):
    hi = lax.Precision.HIGHEST
    a = lax.complex(a_re.astype(F32), a_im.astype(F32))
    dt = jnp.exp(log_dt.astype(F32))[:, None]
    a_dt = a * dt
    a_bar = jnp.exp(a_dt)
    b_bar = ((a_bar - 1.0) / a)[..., None] * lax.complex(b_re.astype(F32), b_im.astype(F32))
    c = lax.complex(c_re.astype(F32), c_im.astype(F32))
    g, p_state, ch = b_bar.shape
    steps = _chunk_steps(g, t_chunk, lane_order)
    k = jnp.arange(t_chunk + 1, dtype=F32)
    a_pow = jnp.exp(a_dt[:, None, :] * k[None, :, None])
    pick = lambda idx: jnp.take_along_axis(a_pow, idx[:, :, None], axis=1)
    kern = jnp.einsum('gcp,gkp,gpd->gkcd', c, a_pow[:, :t_chunk], b_bar, precision=hi).real
    per = LANES // SSM_GROUP
    assert g % per == 0
    lag = steps[:per, None, :] - steps[:per, :, None]
    kern = kern.reshape(g // per, per, t_chunk, ch * ch).transpose(1, 2, 0, 3).reshape(per, t_chunk, -1)
    sel = jnp.clip(lag, 0, t_chunk - 1).reshape(per, t_chunk * t_chunk, 1)
    wy = jnp.take_along_axis(kern, sel, axis=1)
    wy = jnp.where((lag >= 0).reshape(per, t_chunk * t_chunk, 1), wy, 0.0)
    wy = wy.reshape(per, t_chunk, t_chunk, g // per, ch, ch)
    wy = wy.transpose(3, 0, 1, 5, 2, 4).reshape(g, t_chunk * ch, t_chunk * ch)
    wd = pick(t_chunk - 1 - steps)[:, :, None, :] * b_bar.transpose(0, 2, 1)[:, None]
    wd = wd.reshape(g, t_chunk * ch, p_state)
    wd = jnp.concatenate([wd.real, wd.imag], axis=-1)
    gm = c.transpose(0, 2, 1)[:, :, None, :] * pick(steps + 1).transpose(0, 2, 1)[:, :, :, None]
    gm = gm.reshape(g, p_state, t_chunk * ch)
    wc = jnp.concatenate([gm.real, -gm.imag], axis=1)
    dv = jnp.tile(d_skip.astype(F32).reshape(g, 1, ch), (1, 1, t_chunk))
    lev = (t_chunk * (2.0 ** jnp.arange(n_levels, dtype=F32)))
    a_lev = jnp.exp(a_dt[:, None, :] * lev[None, :, None])
    a_r = jnp.concatenate([a_lev.real, a_lev.real], axis=-1)
    a_i = jnp.concatenate([-a_lev.imag, a_lev.imag], axis=-1)
    return wy.astype(BF16), wd.astype(BF16), wc.astype(BF16), dv, a_r, a_i


def _ssm_prompt(u_rows, um_rows, nb, ops):
    wy, wd, wc, dv, a_r, a_i = ops
    g, r, tc = u_rows.shape
    n_chunks = r // nb
    n_levels = a_r.shape[1]
    assert (1 << n_levels) >= n_chunks and n_chunks % SUBLANES == 0
    st = wd.shape[-1]
    pre = -(-(1 << (n_levels - 1)) // SUBLANES) * SUBLANES
    gmap = lambda i: (i, 0, 0)
    return pl.pallas_call(
        functools.partial(_ssm_prompt_body, nb, n_chunks, n_levels, pre),
        grid=(g,),
        in_specs=[pl.BlockSpec((1, r, tc), gmap), pl.BlockSpec((1,) + um_rows.shape[1:], gmap),
                  pl.BlockSpec((1,) + wy.shape[1:], gmap),
                  pl.BlockSpec((1,) + wd.shape[1:], gmap), pl.BlockSpec((1,) + wc.shape[1:], gmap),
                  pl.BlockSpec((1,) + dv.shape[1:], gmap), pl.BlockSpec((1,) + a_r.shape[1:], gmap),
                  pl.BlockSpec((1,) + a_i.shape[1:], gmap)],
        out_specs=[pl.BlockSpec((1, r, tc), gmap), pl.BlockSpec((1, nb, st), gmap)],
        out_shape=[jax.ShapeDtypeStruct((g, r, tc), F32), jax.ShapeDtypeStruct((g, nb, st), F32)],
        scratch_shapes=[pltpu.VMEM((nb, pre + n_chunks, st), F32), pltpu.VMEM((r, st), F32)],
        compiler_params=_cparams(("parallel",)),
        name="ssm_prompt",
    )(u_rows, um_rows, wy, wd, wc, dv, a_r, a_i)


def _ssm_sample(u_rows, h_rows, ops):
    wy, wd, wc, dv, a_r, a_i = ops
    g, r, tc = u_rows.shape
    st = wd.shape[-1]
    gmap = lambda i: (i, 0, 0)
    return pl.pallas_call(
        _ssm_sample_body,
        grid=(g,),
        in_specs=[pl.BlockSpec((1, r, tc), gmap), pl.BlockSpec((1, r, st), gmap),
                  pl.BlockSpec((1,) + wy.shape[1:], gmap), pl.BlockSpec((1,) + wd.shape[1:], gmap),
                  pl.BlockSpec((1,) + wc.shape[1:], gmap), pl.BlockSpec((1,) + dv.shape[1:], gmap),
                  pl.BlockSpec((1,) + a_r.shape[1:], gmap), pl.BlockSpec((1,) + a_i.shape[1:], gmap)],
        out_specs=[pl.BlockSpec((1, r, tc), gmap), pl.BlockSpec((1, r, st), gmap)],
        out_shape=[jax.ShapeDtypeStruct((g, r, tc), F32), jax.ShapeDtypeStruct((g, r, st), F32)],
        compiler_params=_cparams(("parallel",)),
        name="ssm_sample",
    )(u_rows, h_rows, wy, wd, wc, dv, a_r, a_i)


def _mix_out_body(t_chunk, ys_ref, ya_ref, x_ref, wglu_ref, bglu_ref, gs_ref, wo_ref, gpost_ref, o_ref,
                  *maybe_scr):
    if t_chunk:
        (slab_scr,) = maybe_scr
        n_slabs = slab_scr.shape[0]
        n_rows = slab_scr.shape[1] // t_chunk
        per_slab = LANES // SSM_GROUP
        lane_blk = lax.broadcasted_iota(jnp.int32, (n_rows, LANES), 1) // SSM_GROUP
        for k in range(n_slabs):
            for t in range(t_chunk // per_slab):
                tiles = [ys_ref[k * per_slab + gg, :, t * LANES:(t + 1) * LANES] for gg in range(per_slab)]
                for s1 in range(per_slab):
                    w = tiles[(-s1) % per_slab]
                    for q in range(1, per_slab):
                        w = jnp.where(lane_blk == q, tiles[(q - s1) % per_slab], w)
                    if s1:
                        w = pltpu.roll(w, (per_slab - s1) * SSM_GROUP, axis=1)
                    slab_scr[k, pl.ds(t * per_slab + s1, n_rows, stride=t_chunk), :] = w
        ys = jnp.concatenate([slab_scr[k] for k in range(n_slabs)], axis=-1)
    else:
        ys = ys_ref[...]
    z = jax.nn.gelu(ys)
    gate = jax.nn.sigmoid(jnp.dot(z.astype(BF16), wglu_ref[...], preferred_element_type=F32) + bglu_ref[...])
    ns = _rms(z * gate, gs_ref[...]).astype(BF16)
    w = ns.shape[-1]
    y = (jnp.dot(ns, wo_ref[:w, :], preferred_element_type=F32)
         + jnp.dot(ya_ref[...].astype(BF16), wo_ref[w:, :], preferred_element_type=F32))
    o_ref[...] = x_ref[...] + _rms(y, gpost_ref[...])


def _mix_out(ys, ya, x, w_glu, b_glu, g_ssm, w_o, g_post, t_chunk, name):
    n, d = x.shape
    tm = _row_tile(n)
    row = lambda i: (i, 0)
    c2 = lambda i: (0, 0)
    if t_chunk:
        assert tm % (t_chunk * SUBLANES) == 0
        w = ys.shape[0] * SSM_GROUP
        assert w % LANES == 0
        ys_spec = pl.BlockSpec((ys.shape[0], tm // t_chunk, ys.shape[2]), lambda i: (0, i, 0))
        scratch = [pltpu.VMEM((w // LANES, tm, LANES), F32)]
    else:
        ys_spec = pl.BlockSpec((tm, ys.shape[1]), row)
        scratch = []
    return pl.pallas_call(
        functools.partial(_mix_out_body, t_chunk),
        grid=(n // tm,),
        in_specs=[ys_spec, pl.BlockSpec((tm, ya.shape[1]), row), pl.BlockSpec((tm, d), row),
                  pl.BlockSpec(w_glu.shape, c2), pl.BlockSpec(b_glu.shape, c2), pl.BlockSpec(g_ssm.shape, c2),
                  pl.BlockSpec(w_o.shape, c2), pl.BlockSpec(g_post.shape, c2)],
        out_specs=pl.BlockSpec((tm, d), row),
        out_shape=jax.ShapeDtypeStruct((n, d), F32),
        scratch_shapes=scratch,
        compiler_params=_cparams(("parallel",)),
        name=name,
    )(ys, ya, x, w_glu, b_glu, g_ssm, w_o, g_post)


def _rope_tables(pos):
    inv = ROPE_THETA ** (-jnp.arange(ROPE_HALF, dtype=F32) / ROPE_HALF)
    ang = pos.astype(F32)[:, None] * inv[None, :]
    reps = LANES // ROPE_HALF
    return jnp.tile(jnp.cos(ang), (1, reps)), jnp.tile(jnp.sin(ang), (1, reps))


def _regroup_uq_columns(w_uq):
    w = w_uq.reshape(w_uq.shape[0], MLA_HEADS, MLA_NOPE_DIM + MLA_ROPE_DIM)
    parts = (w[:, :, :MLA_NOPE_DIM], w[:, :, MLA_NOPE_DIM:MLA_NOPE_DIM + ROPE_HALF], w[:, :, MLA_NOPE_DIM + ROPE_HALF:])
    return jnp.concatenate([p.reshape(w_uq.shape[0], -1) for p in parts], axis=1)


def kernel(x_prompt, x_sample, cache_kv_latent, cache_k_rope, state_ssm_re, state_ssm_im, page_table, meta_tokens, g_ff1_pre, w_ff1_gate, w_ff1_up, w_ff1_down, g_ff1_post, g_mix_pre, w_in, ssm_a_re, ssm_a_im, ssm_log_dt, ssm_b_re, ssm_b_im, ssm_c_re, ssm_c_im, ssm_d, w_glu, b_glu, g_q_norm, w_uq, g_kv_norm, w_uk, w_uv, g_ssm_out, g_mla_out, w_o, g_mix_post, g_ff2_pre, w_ff2_gate, w_ff2_up, w_ff2_down, g_ff2_post):
    depth = w_in.shape[0]
    assert depth == 1, "single-layer step"
    bp, seq, d_model = x_prompt.shape
    db, ds, _ = x_sample.shape
    n_pages = page_table.shape[1]
    page = cache_kv_latent.shape[2]
    past_len = n_pages * page
    d_kv = cache_kv_latent.shape[3]
    d_q = w_uq.shape[1]
    n_groups, n_state = ssm_a_re.shape[1], ssm_a_re.shape[2]
    d_u = n_groups * SSM_GROUP
    dims = (d_u, d_q, d_kv)
    l = 0
    row = lambda v: v[l].reshape(1, -1).astype(F32)

    ff1_w = _ffn_weights(w_ff1_gate[l], w_ff1_up[l], w_ff1_down[l])
    ff2_w = _ffn_weights(w_ff2_gate[l], w_ff2_up[l], w_ff2_down[l])
    w_in_b = w_in[l].astype(BF16)
    w_uq_b = _regroup_uq_columns(w_uq[l]).astype(BF16)
    w_ukt = jnp.transpose(w_uk[l], (1, 2, 0)).astype(BF16)
    w_uv_b = jnp.transpose(w_uv[l], (1, 0, 2)).astype(BF16)
    w_uvt_b = w_uv[l].reshape(d_kv, -1).T.astype(BF16)
    w_glu_b = w_glu[l].astype(BF16)
    w_o_b = w_o[l].astype(BF16)

    xs = [x_prompt.reshape(bp * seq, d_model), x_sample.reshape(db * ds, d_model), meta_tokens.astype(F32)]
    names = ["prompt", "sample", "meta"]
    pos = [N_META + jnp.tile(jnp.arange(seq), bp), past_len + jnp.tile(jnp.arange(ds), db), jnp.arange(N_META)]

    t_p = SSM_CHUNK
    assert N_META == t_p and seq % t_p == 0
    x1, pr = [], []
    for x, nm, ps, t_rows in zip(xs, names, pos, (t_p, 0, 0)):
        y, hn = _ffn(x, row(g_ff1_pre), ff1_w, row(g_ff1_post), row(g_mix_pre), True, "ffn1_" + nm)
        cos, sin = _rope_tables(ps)
        x1.append(y)
        pr.append(_proj(hn, w_in_b, row(g_q_norm), w_uq_b, w_ukt, row(g_kv_norm), w_uvt_b, cos, sin, dims, t_rows,
                        "proj_" + nm))
    (_, ckv_p, ckvb_p, vt_p, kr_p, krb_p, qc_p, qr_p, u_rows) = pr[0]
    (u_s, ckv_s, _, _, kr_s, _, qc_s, qr_s) = pr[1]
    (u_m, ckv_m, ckvb_m, vt_m, kr_m, krb_m, _, _) = pr[2]

    n_levels = max(1, (seq // t_p - 1).bit_length())
    ssm_w = (ssm_a_re[l], ssm_a_im[l], ssm_log_dt[l], ssm_b_re[l], ssm_b_im[l], ssm_c_re[l], ssm_c_im[l], ssm_d[l])
    ops_p = _ssm_weights(*ssm_w, t_p, n_levels, True)
    ops_s = _ssm_weights(*ssm_w, ds, 1)
    tc = t_p * SSM_GROUP
    um_rows = u_m.reshape(t_p, n_groups, SSM_GROUP).transpose(1, 0, 2)
    um_rows = jnp.take_along_axis(um_rows, _chunk_steps(n_groups, t_p, True)[:, :, None], axis=1)
    um_rows = um_rows.reshape(n_groups, 1, tc)
    um_rows = jnp.pad(um_rows, ((0, 0), (0, SUBLANES - 1), (0, 0)))
    ys_p, hl_p = _ssm_prompt(u_rows, um_rows, bp, ops_p)
    hl_p = hl_p.transpose(1, 0, 2)

    us_rows = u_s.reshape(db, ds, n_groups, SSM_GROUP).transpose(2, 0, 1, 3).reshape(n_groups, db, ds * SSM_GROUP)
    h0_rows = jnp.concatenate([state_ssm_re[l], state_ssm_im[l]], axis=-1).astype(F32).transpose(1, 0, 2)
    ysr, hl_s = _ssm_sample(us_rows, h0_rows, ops_s)
    ys_s = ysr.reshape(n_groups, db, ds, SSM_GROUP).transpose(1, 2, 0, 3).reshape(db * ds, d_u)
    hl_s = hl_s.transpose(1, 0, 2)

    g_mla = row(g_mla_out)
    pad_m = LANES - N_META
    kmc = jnp.pad(ckvb_m, ((0, pad_m), (0, 0)))
    kmr = jnp.pad(krb_m, ((0, pad_m), (0, 0)))
    vmt = jnp.pad(vt_m, ((0, 0), (0, pad_m)))
    ya_p = _attn_prompt(qc_p, qr_p, ckvb_p.reshape(bp, seq, d_kv), vt_p,
                        krb_p.reshape(bp, seq, MLA_ROPE_DIM), kmc, vmt, kmr, g_mla)
    to_seq = lambda q: q.reshape(MLA_HEADS, db, ds, q.shape[-1]).transpose(1, 0, 2, 3).reshape(db, MLA_HEADS * ds, q.shape[-1])
    ya_s = _attn_sample(page_table, to_seq(qc_s), to_seq(qr_s), ckv_s.reshape(db, ds, d_kv),
                        kr_s.reshape(db, ds, MLA_ROPE_DIM), cache_kv_latent[l],
                        jnp.swapaxes(cache_k_rope[l], 1, 2), w_uv_b, g_mla)
    ya_s = ya_s.reshape(db * ds, -1)

    outs = []
    for x, ys, ya, t_rows, nm in ((x1[0], ys_p, ya_p, t_p, "prompt"), (x1[1], ys_s, ya_s, 0, "sample")):
        x2 = _mix_out(ys, ya, x, w_glu_b, row(b_glu), row(g_ssm_out), w_o_b, row(g_mix_post), t_rows,
                      "mix_out_" + nm)
        y, _ = _ffn(x2, row(g_ff2_pre), ff2_w, row(g_ff2_post), row(g_ff2_post), False, "ffn2_" + nm)
        outs.append(y)

    y_prompt = outs[0].reshape(bp, seq, d_model)
    y_sample = outs[1].reshape(db, ds, d_model)
    meta_b = lambda v: jnp.broadcast_to(v[None], (bp,) + v.shape)
    new_ckv_p = jnp.concatenate([meta_b(ckv_m), ckv_p.reshape(bp, seq, d_kv)], axis=1)[None]
    new_kr_p = jnp.concatenate([meta_b(kr_m), kr_p.reshape(bp, seq, MLA_ROPE_DIM)], axis=1)[None]
    return (y_prompt, y_sample, new_ckv_p, new_kr_p,
            hl_p[None, :, :, :n_state], hl_p[None, :, :, n_state:],
            ckv_s.reshape(1, db, ds, d_kv), kr_s.reshape(1, db, ds, MLA_ROPE_DIM),
            hl_s[None, :, :, :n_state], hl_s[None, :, :, n_state:])
```

```python
import functools
import math

import jax
import jax.numpy as jnp
from jax import lax
from jax.experimental import pallas as pl
from jax.experimental.pallas import tpu as pltpu

F32 = jnp.float32
BF16 = jnp.bfloat16

N_META = 16
SSM_GROUP = 16
MLA_HEADS = 8
MLA_NOPE_DIM = 64
MLA_ROPE_DIM = 32
ROPE_HALF = MLA_ROPE_DIM // 2
ROPE_THETA = 10000.0
RMS_EPS = 1e-6
ATTN_SCALE = (MLA_NOPE_DIM + MLA_ROPE_DIM) ** -0.5

LANES = 128
SUBLANES = 8
VMEM_LIMIT = 56 * 1024 * 1024

ROW_TILE = 512
FFN_ROW_TILE = 1024
FF_TILE = 256
ATTN_BQ = 256
ATTN_BK = 512
ATTN_COLS = 512
PAGES_PER_STEP = 64
SAMPLE_PARTS = 2
SAMPLE_RING = 2
SSM_CHUNK = 16


def _rms(x, g):
    return (x * lax.rsqrt(jnp.mean(x * x, axis=-1, keepdims=True) + RMS_EPS)) * g


def _row_tile(n):
    return ROW_TILE if n % ROW_TILE == 0 else n


def _cparams(sem):
    return pltpu.CompilerParams(dimension_semantics=sem, vmem_limit_bytes=VMEM_LIMIT)


def _ffn_body(emit_norm, x_ref, gpre_ref, wg_ref, wu_ref, wd_ref, gpost_ref, gnext_ref, *refs):
    if emit_norm:
        y_ref, hn_ref, h_scr, acc_scr = refs
    else:
        y_ref, h_scr, acc_scr = refs
        hn_ref = None
    j = pl.program_id(1)

    @pl.when(j == 0)
    def _():
        h_scr[...] = _rms(x_ref[...], gpre_ref[...]).astype(BF16)
        acc_scr[...] = jnp.zeros_like(acc_scr)

    h = h_scr[...]
    g = jnp.dot(h, wg_ref[...], preferred_element_type=F32)
    u = jnp.dot(h, wu_ref[...], preferred_element_type=F32)
    a = (g * jax.nn.sigmoid(g)) * u
    acc_scr[...] += jnp.dot(a.astype(BF16), wd_ref[...], preferred_element_type=F32)

    @pl.when(j == pl.num_programs(1) - 1)
    def _():
        y = x_ref[...] + 0.5 * _rms(acc_scr[...], gpost_ref[...])
        y_ref[...] = y
        if emit_norm:
            hn_ref[...] = _rms(y, gnext_ref[...]).astype(BF16)


def _ffn_weights(wg, wu, wd):
    d, d_ff = wg.shape
    tf = FF_TILE
    assert d_ff % tf == 0
    chunk = lambda w: w.astype(BF16).reshape(d, d_ff // tf, tf).transpose(1, 0, 2)
    return chunk(wg), chunk(wu), wd.astype(BF16).reshape(d_ff // tf, tf, d)


def _ffn(x, g_pre, weights, g_post, g_next, emit_norm, name):
    wg, wu, wd = weights
    n, d = x.shape
    nj, _, tf = wg.shape
    tm = FFN_ROW_TILE if n % FFN_ROW_TILE == 0 else n
    row = lambda i, j: (i, 0)
    vec = lambda i, j: (0, 0)
    chunk = lambda i, j: (j, 0, 0)
    out_shape = [jax.ShapeDtypeStruct((n, d), F32)]
    out_specs = [pl.BlockSpec((tm, d), row)]
    if emit_norm:
        out_shape.append(jax.ShapeDtypeStruct((n, d), BF16))
        out_specs.append(pl.BlockSpec((tm, d), row))
    res = pl.pallas_call(
        functools.partial(_ffn_body, emit_norm),
        grid=(n // tm, nj),
        in_specs=[
            pl.BlockSpec((tm, d), row),
            pl.BlockSpec((1, d), vec),
            pl.BlockSpec((None, d, tf), chunk),
            pl.BlockSpec((None, d, tf), chunk),
            pl.BlockSpec((None, tf, d), chunk),
            pl.BlockSpec((1, d), vec),
            pl.BlockSpec((1, d), vec),
        ],
        out_specs=out_specs,
        out_shape=out_shape,
        scratch_shapes=[pltpu.VMEM((tm, d), BF16), pltpu.VMEM((tm, d), F32)],
        compiler_params=_cparams(("parallel", "arbitrary")),
        name=name,
    )(x, g_pre, wg, wu, wd, g_post, g_next)
    return res if emit_norm else (res[0], None)


def _proj_body(d_u, d_q, d_kv, t_chunk, h_ref, win_ref, gq_ref, wuq_ref, wukt_ref, gkv_ref, wuvt_ref,
               cos_ref, sin_ref,
               u_ref, ckv_ref, ckvb_ref, vt_ref, kr_ref, krb_ref, qc_ref, qr_ref, *maybe_urows):
    proj = jnp.dot(h_ref[...], win_ref[...], preferred_element_type=F32)
    off_q, off_kv, off_kr = d_u, d_u + d_q, d_u + d_q + d_kv
    u_ref[...] = proj[:, :off_q]
    if t_chunk:
        urows_ref, slab_scr = maybe_urows
        n_rows = u_ref.shape[0] // t_chunk
        per_slab = LANES // SSM_GROUP
        lane_blk = lax.broadcasted_iota(jnp.int32, (n_rows, LANES), 1) // SSM_GROUP
        for k in range(d_u // LANES):
            slab_scr[k] = proj[:, k * LANES:(k + 1) * LANES]
        for k in range(d_u // LANES):
            rolled = []
            for s in range(t_chunk):
                x = slab_scr[k, pl.ds(s, n_rows, stride=t_chunk), :]
                shift = (s % per_slab) * SSM_GROUP
                rolled.append(pltpu.roll(x, shift, axis=1) if shift else x)
            for gg in range(per_slab):
                tiles = []
                for t in range(t_chunk // per_slab):
                    tile = rolled[t * per_slab]
                    for s1 in range(1, per_slab):
                        tile = jnp.where(lane_blk == (gg + s1) % per_slab, rolled[t * per_slab + s1], tile)
                    tiles.append(tile)
                urows_ref[k * per_slab + gg] = jnp.concatenate(tiles, axis=-1)
    cq = _rms(proj[:, off_q:off_kv], gq_ref[...]).astype(BF16)
    q = jnp.dot(cq, wuq_ref[...], preferred_element_type=F32) * ATTN_SCALE
    n_nope = MLA_HEADS * MLA_NOPE_DIM
    for hd in range(MLA_HEADS):
        qn = q[:, hd * MLA_NOPE_DIM:(hd + 1) * MLA_NOPE_DIM].astype(BF16)
        qc_ref[hd] = jnp.dot(qn, wukt_ref[hd], preferred_element_type=F32).astype(BF16)
    cos = cos_ref[...]
    sin = sin_ref[...]
    r1 = q[:, n_nope:n_nope + LANES]
    r2 = q[:, n_nope + LANES:n_nope + 2 * LANES]
    o1 = r1 * cos - r2 * sin
    o2 = r2 * cos + r1 * sin
    for hd in range(MLA_HEADS):
        sl = slice(hd * ROPE_HALF, (hd + 1) * ROPE_HALF)
        qr_ref[hd] = jnp.concatenate([o1[:, sl], o2[:, sl]], axis=-1).astype(BF16)
    ckv = _rms(proj[:, off_kv:off_kr], gkv_ref[...])
    ckv_ref[...] = ckv
    ckvb = ckv.astype(BF16)
    ckvb_ref[...] = ckvb
    vt_ref[...] = lax.dot_general(wuvt_ref[...], ckvb, (((1,), (1,)), ((), ())),
                                  preferred_element_type=F32).astype(BF16)
    x1 = proj[:, off_kr:off_kr + ROPE_HALF]
    x2 = proj[:, off_kr + ROPE_HALF:off_kr + MLA_ROPE_DIM]
    c16 = cos[:, :ROPE_HALF]
    s16 = sin[:, :ROPE_HALF]
    kr = jnp.concatenate([x1 * c16 - x2 * s16, x2 * c16 + x1 * s16], axis=-1)
    kr_ref[...] = kr
    krb_ref[...] = kr.astype(BF16)


def _proj(hn, w_in, g_q, w_uq, w_ukt, g_kv, w_uvt, cos, sin, dims, t_chunk, name):
    n, d = hn.shape
    d_u, d_q, d_kv = dims
    tm = _row_tile(n)
    row = lambda i: (i, 0)
    full2 = lambda i: (0, 0)
    full3 = lambda i: (0, 0, 0)
    hrow = lambda i: (0, i, 0)
    extra_specs, extra_shapes, scratch = [], [], []
    if t_chunk:
        assert tm % (t_chunk * SUBLANES) == 0 and d_u % LANES == 0
        n_groups = d_u // SSM_GROUP
        extra_specs = [pl.BlockSpec((n_groups, tm // t_chunk, t_chunk * SSM_GROUP), hrow)]
        extra_shapes = [jax.ShapeDtypeStruct((n_groups, n // t_chunk, t_chunk * SSM_GROUP), F32)]
        scratch = [pltpu.VMEM((d_u // LANES, tm, LANES), F32)]
    return pl.pallas_call(
        functools.partial(_proj_body, d_u, d_q, d_kv, t_chunk),
        grid=(n // tm,),
        in_specs=[
            pl.BlockSpec((tm, d), row),
            pl.BlockSpec(w_in.shape, full2),
            pl.BlockSpec(g_q.shape, full2),
            pl.BlockSpec(w_uq.shape, full2),
            pl.BlockSpec(w_ukt.shape, full3),
            pl.BlockSpec(g_kv.shape, full2),
            pl.BlockSpec(w_uvt.shape, full2),
            pl.BlockSpec((tm, LANES), row),
            pl.BlockSpec((tm, LANES), row),
        ],
        out_specs=[
            pl.BlockSpec((tm, d_u), row),
            pl.BlockSpec((tm, d_kv), row),
            pl.BlockSpec((tm, d_kv), row),
            pl.BlockSpec((w_uvt.shape[0], tm), lambda i: (0, i)),
            pl.BlockSpec((tm, MLA_ROPE_DIM), row),
            pl.BlockSpec((tm, MLA_ROPE_DIM), row),
            pl.BlockSpec((MLA_HEADS, tm, d_kv), hrow),
            pl.BlockSpec((MLA_HEADS, tm, MLA_ROPE_DIM), hrow),
        ] + extra_specs,
        out_shape=[
            jax.ShapeDtypeStruct((n, d_u), F32),
            jax.ShapeDtypeStruct((n, d_kv), F32),
            jax.ShapeDtypeStruct((n, d_kv), BF16),
            jax.ShapeDtypeStruct((w_uvt.shape[0], n), BF16),
            jax.ShapeDtypeStruct((n, MLA_ROPE_DIM), F32),
            jax.ShapeDtypeStruct((n, MLA_ROPE_DIM), BF16),
            jax.ShapeDtypeStruct((MLA_HEADS, n, d_kv), BF16),
            jax.ShapeDtypeStruct((MLA_HEADS, n, MLA_ROPE_DIM), BF16),
        ] + extra_shapes,
        scratch_shapes=scratch,
        compiler_params=_cparams(("parallel",)),
        name=name,
    )(hn, w_in, g_q, w_uq, w_ukt, g_kv, w_uvt, cos, sin)


def _scores(qc, qr, kc, kr):
    nt = (((1,), (1,)), ((), ()))
    return (lax.dot_general(qc, kc, nt, preferred_element_type=F32)
            + lax.dot_general(qr, kr, nt, preferred_element_type=F32))


def _softmax_step(s, kc, m_scr, l_scr, acc_scr):
    m_old = m_scr[...]
    m_new = jnp.maximum(m_old, jnp.max(s, axis=-1, keepdims=True))
    alpha = jnp.exp(m_old - m_new)
    p = jnp.exp(s - m_new)
    l_scr[...] = alpha * l_scr[...] + jnp.sum(p, axis=-1, keepdims=True)
    acc_scr[...] = alpha * acc_scr[...] + jnp.dot(p.astype(BF16), kc, preferred_element_type=F32)
    m_scr[...] = m_new


def _attn_finish(rows_per_head, wuv_ref, g_ref, l_scr, acc_scr):
    o = (acc_scr[...] / l_scr[...]).astype(BF16)
    outs = []
    for hd in range(MLA_HEADS):
        oh = o[hd * rows_per_head:(hd + 1) * rows_per_head]
        outs.append(jnp.dot(oh, wuv_ref[hd], preferred_element_type=F32))
    return _rms(jnp.concatenate(outs, axis=-1), g_ref[...])


def _attn_cols_step(parts, m_blk, cols, heads, bq, m_scr, l_scr, acc_scr):
    m_old = m_scr[:, cols]
    m_new = jnp.maximum(m_old, m_blk)
    alpha = jnp.exp(m_old - m_new)
    ps = [jnp.exp(s - m_new) for s, _ in parts]
    l_new = alpha * l_scr[:, cols]
    for p in ps:
        l_new = l_new + jnp.sum(p, axis=0, keepdims=True)
    l_scr[:, cols] = l_new
    m_scr[:, cols] = m_new
    v_dim = acc_scr.shape[0] // MLA_HEADS
    for n, hd in enumerate(heads):
        hc = slice(n * bq, (n + 1) * bq)
        rows = slice(hd * v_dim, (hd + 1) * v_dim)
        acc = alpha[:, hc] * acc_scr[rows, :]
        for p, (_, vt) in zip(ps, parts):
            acc = acc + jnp.dot(vt[rows, :], p[:, hc].astype(BF16), preferred_element_type=F32)
        acc_scr[rows, :] = acc


def _attn_prompt_body(bq, bk, ncol, qc_ref, qr_ref, qcn_ref, qrn_ref, kc_ref, vt_ref, kr_ref, kmc_ref, vmt_ref,
                      kmr_ref, g_ref, o_ref, m_scr, l_scr, acc_scr, s_scr, mb_scr):
    i = pl.program_id(1)
    nt = (((1,), (1,)), ((), ()))
    heads_per = ncol // bq
    n_groups = MLA_HEADS // heads_per
    groups = [slice(gi * ncol, (gi + 1) * ncol) for gi in range(n_groups)]
    own_q = (qc_ref, qr_ref)
    next_q = (qcn_ref, qrn_ref)

    def scores(gi, kc, kr, q_refs=own_q):
        hs = slice(gi * heads_per, (gi + 1) * heads_per)
        qc = q_refs[0][hs].reshape(ncol, qc_ref.shape[-1])
        qr = q_refs[1][hs].reshape(ncol, qr_ref.shape[-1])
        return (lax.dot_general(kc, qc, nt, preferred_element_type=F32)
                + lax.dot_general(kr, qr, nt, preferred_element_type=F32))

    def produce(j, slot, q_refs=own_q):
        start = pl.multiple_of(j * bk, bk)
        kc = kc_ref[0, pl.ds(start, bk), :]
        kr = kr_ref[0, pl.ds(start, bk), :]
        for gi, cols in enumerate(groups):
            s = scores(gi, kc, kr, q_refs)
            s_scr[slot, :, cols] = s
            mb_scr[slot, :, cols] = jnp.max(s, axis=0, keepdims=True)

    def meta_scores():
        out = []
        for gi in range(n_groups):
            sm = scores(gi, kmc_ref[...], kmr_ref[...])
            out.append(jnp.where(lax.broadcasted_iota(jnp.int32, sm.shape, 0) < N_META, sm, -jnp.inf))
        return out

    def consume(j, slot, diagonal, meta=None):
        start = pl.multiple_of(j * bk, bk)
        vt = vt_ref[:, pl.ds(start, bk)]
        for gi, cols in enumerate(groups):
            s = s_scr[slot, :, cols]
            if diagonal:
                k_pos = start + lax.broadcasted_iota(jnp.int32, s.shape, 0)
                q_pos = i * bq + (lax.broadcasted_iota(jnp.int32, s.shape, 1) & (bq - 1))
                s = jnp.where(k_pos <= q_pos, s, -jnp.inf)
                sm = meta[gi]
                m_blk = jnp.maximum(jnp.max(s, axis=0, keepdims=True), jnp.max(sm, axis=0, keepdims=True))
                parts = [(s, vt), (sm, vmt_ref[...])]
            else:
                m_blk = mb_scr[slot, :, cols]
                parts = [(s, vt)]
            heads = range(gi * heads_per, (gi + 1) * heads_per)
            _attn_cols_step(parts, m_blk, cols, heads, bq, m_scr, l_scr, acc_scr)

    ratio = bk // bq
    n_full = i // ratio
    odd = (n_full & 1) == 1
    m_scr[...] = jnp.full_like(m_scr, -jnp.inf)
    l_scr[...] = jnp.zeros_like(l_scr)
    acc_scr[...] = jnp.zeros_like(acc_scr)

    base = ((i // (2 * ratio)) * ratio + jnp.minimum(lax.rem(i, 2 * ratio), ratio)) & 1

    @pl.when(i == 0)
    def _():
        produce(0, 0)

    def run(b0):
        def pair(k, carry):
            j = 2 * k
            produce(j + 1, 1 - b0)
            consume(j, b0, False)
            produce(j + 2, b0)
            consume(j + 1, 1 - b0, False)
            return carry

        lax.fori_loop(0, n_full // 2, pair, 0)

        @pl.when(odd)
        def _():
            produce(n_full, 1 - b0)
            consume(n_full - 1, b0, False)
            meta = meta_scores()
            produce(0, b0, next_q)
            consume(n_full, 1 - b0, True, meta)

        @pl.when(jnp.logical_not(odd))
        def _():
            meta = meta_scores()
            produce(0, 1 - b0, next_q)
            consume(n_full, b0, True, meta)

    for b0 in (0, 1):
        pl.when(base == b0)(functools.partial(run, b0))

    v_dim = acc_scr.shape[0] // MLA_HEADS
    outs = [acc_scr[hd * v_dim:(hd + 1) * v_dim, :] / l_scr[:, hd * bq:(hd + 1) * bq] for hd in range(MLA_HEADS)]
    y = jnp.transpose(jnp.concatenate(outs, axis=0))
    o_ref[...] = _rms(y, g_ref[...]).astype(o_ref.dtype)


def _attn_prompt(qc, qr, kc, vt, kr, kmc, vmt, kmr, g_mla):
    nb, seq, d_kv = kc.shape
    bq, bk, ncol = ATTN_BQ, ATTN_BK, ATTN_COLS
    assert seq % bk == 0 and bk % bq == 0 and bq & (bq - 1) == 0 and N_META >= 1
    assert ncol % bq == 0 and (MLA_HEADS * bq) % ncol == 0
    nq = seq // bq
    rows = MLA_HEADS * bq
    d_out = vt.shape[0]
    qmap = lambda b, i: (0, b * nq + i, 0)
    qnext = lambda b, i: (0, b * nq + jnp.minimum(i + 1, nq - 1), 0)
    kmap = lambda b, i: (b, 0, 0)
    c2 = lambda b, i: (0, 0)
    return pl.pallas_call(
        functools.partial(_attn_prompt_body, bq, bk, ncol),
        grid=(nb, nq),
        in_specs=[
            pl.BlockSpec((MLA_HEADS, bq, d_kv), qmap),
            pl.BlockSpec((MLA_HEADS, bq, MLA_ROPE_DIM), qmap),
            pl.BlockSpec((MLA_HEADS, bq, d_kv), qnext),
            pl.BlockSpec((MLA_HEADS, bq, MLA_ROPE_DIM), qnext),
            pl.BlockSpec((1, seq, d_kv), kmap),
            pl.BlockSpec((d_out, seq), lambda b, i: (0, b)),
            pl.BlockSpec((1, seq, MLA_ROPE_DIM), kmap),
            pl.BlockSpec(kmc.shape, c2),
            pl.BlockSpec(vmt.shape, c2),
            pl.BlockSpec(kmr.shape, c2),
            pl.BlockSpec(g_mla.shape, c2),
        ],
        out_specs=pl.BlockSpec((bq, d_out), lambda b, i: (b * nq + i, 0)),
        out_shape=jax.ShapeDtypeStruct((nb * seq, d_out), BF16),
        scratch_shapes=[pltpu.VMEM((1, rows), F32), pltpu.VMEM((1, rows), F32),
                        pltpu.VMEM((d_out, bq), F32),
                        pltpu.VMEM((2, bk, rows), F32), pltpu.VMEM((2, 1, rows), F32)],
        compiler_params=_cparams(("parallel", "arbitrary")),
        name="attn_prompt",
    )(qc, qr, qc, qr, kc, vt, kr, kmc, vmt, kmr, g_mla)


def _attn_sample_body(npg, page, ds, n_part, pt_ref, qc_ref, qr_ref, cn_ref, rn_ref, wuv_ref, g_ref,
                      cache_c, cache_r, o_ref, pc_buf, pr_buf, sems, kc_scr, krt_scr, s_scr, mb_scr,
                      m_scr, l_scr, acc_scr):
    b = pl.program_id(0)
    j = pl.program_id(1)
    n_steps = pl.num_programs(1)
    step = b * n_steps + j
    n_total = pl.num_programs(0) * n_steps
    ring = pc_buf.shape[0]
    slot = lax.rem(step, ring)
    qc = qc_ref[0]
    qr = qr_ref[0]
    nt = (((1,), (1,)), ((), ()))
    per = npg // n_part

    def page_copies(bb, jj, sl):
        out = []
        for pg in range(npg):
            idx = pt_ref[bb, jj * npg + pg]
            out.append(pltpu.make_async_copy(cache_c.at[idx], pc_buf.at[sl, pg], sems.at[0, sl]))
            out.append(pltpu.make_async_copy(cache_r.at[idx], pr_buf.at[sl, pg], sems.at[1, sl]))
        return out

    for ahead in range(ring - 1):
        @pl.when(jnp.logical_and(step == 0, ahead < n_total))
        def _(ahead=ahead):
            for cp in page_copies(jnp.int32(ahead) // n_steps, lax.rem(jnp.int32(ahead), n_steps), ahead):
                cp.start()

    nxt = step + (ring - 1)

    @pl.when(nxt < n_total)
    def _():
        for cp in page_copies(nxt // n_steps, lax.rem(nxt, n_steps), lax.rem(nxt, ring)):
            cp.start()

    for cp in page_copies(b, j, slot):
        cp.wait()
    pc_refs = [pc_buf.at[slot, pg] for pg in range(npg)]
    pr_refs = [pr_buf.at[slot, pg] for pg in range(npg)]

    @pl.when(j == 0)
    def _():
        m_scr[...] = jnp.full_like(m_scr, -jnp.inf)
        l_scr[...] = jnp.zeros_like(l_scr)
        acc_scr[...] = jnp.zeros_like(acc_scr)

    def keys(part):
        return slice(part * per * page, (part + 1) * per * page)

    def produce(part):
        for pg in range(part * per, (part + 1) * per):
            kc_scr[pg * page:(pg + 1) * page, :] = pc_refs[pg][...].astype(BF16)
            krt_scr[:, pg * page:(pg + 1) * page] = pr_refs[pg][...].astype(BF16)
        s = (lax.dot_general(qc, kc_scr[keys(part), :], nt, preferred_element_type=F32)
             + jnp.dot(qr, krt_scr[:, keys(part)], preferred_element_type=F32))
        s_scr[part] = s
        mb_scr[part] = jnp.max(s, axis=-1, keepdims=True)

    def consume(part):
        s = s_scr[part]
        m_old = m_scr[...]
        m_new = jnp.maximum(m_old, mb_scr[part])
        alpha = jnp.exp(m_old - m_new)
        p = jnp.exp(s - m_new)
        l_scr[...] = alpha * l_scr[...] + jnp.sum(p, axis=-1, keepdims=True)
        acc_scr[...] = alpha * acc_scr[...] + jnp.dot(p.astype(BF16), kc_scr[keys(part), :],
                                                      preferred_element_type=F32)
        m_scr[...] = m_new

    produce(0)
    for part in range(n_part):
        if part + 1 < n_part:
            produce(part + 1)
        consume(part)

    @pl.when(j == pl.num_programs(1) - 1)
    def _():
        pad = LANES - ds
        kn = jnp.concatenate([cn_ref[0], jnp.zeros((pad, cn_ref.shape[-1]), F32)], axis=0).astype(BF16)
        rn = jnp.concatenate([rn_ref[0], jnp.zeros((pad, rn_ref.shape[-1]), F32)], axis=0).astype(BF16)
        s = _scores(qc, qr, kn, rn)
        t_q = lax.broadcasted_iota(jnp.int32, s.shape, 0) & (ds - 1)
        t_k = lax.broadcasted_iota(jnp.int32, s.shape, 1)
        s = jnp.where(t_k <= t_q, s, -jnp.inf)
        _softmax_step(s, kn, m_scr, l_scr, acc_scr)
        o_ref[0] = _attn_finish(ds, wuv_ref, g_ref, l_scr, acc_scr)


def _attn_sample(page_table, qc, qr, c_new, r_new, cache_c, cache_r, w_uv, g_mla):
    db, rows, d_kv = qc.shape
    ds = c_new.shape[1]
    n_pages = page_table.shape[1]
    page = cache_c.shape[1]
    npg = math.gcd(PAGES_PER_STEP, n_pages)
    n_part = math.gcd(SAMPLE_PARTS, npg)
    assert ds & (ds - 1) == 0 and ds <= LANES
    assert cache_r.shape[1:] == (MLA_ROPE_DIM, page)
    d_out = w_uv.shape[0] * w_uv.shape[2]
    bmap = lambda b, j, pt: (b, 0, 0)
    c2 = lambda b, j, pt: (0, 0)
    c3 = lambda b, j, pt: (0, 0, 0)
    in_specs = [
        pl.BlockSpec((1, rows, d_kv), bmap),
        pl.BlockSpec((1, rows, MLA_ROPE_DIM), bmap),
        pl.BlockSpec((1, ds, d_kv), bmap),
        pl.BlockSpec((1, ds, MLA_ROPE_DIM), bmap),
        pl.BlockSpec(w_uv.shape, c3),
        pl.BlockSpec(g_mla.shape, c2),
        pl.BlockSpec(memory_space=pl.ANY),
        pl.BlockSpec(memory_space=pl.ANY),
    ]
    part_keys = (npg // n_part) * page
    grid_spec = pltpu.PrefetchScalarGridSpec(
        num_scalar_prefetch=1,
        grid=(db, n_pages // npg),
        in_specs=in_specs,
        out_specs=pl.BlockSpec((1, ds, d_out), bmap),
        scratch_shapes=[pltpu.VMEM((SAMPLE_RING, npg, page, d_kv), F32),
                        pltpu.VMEM((SAMPLE_RING, npg, MLA_ROPE_DIM, page), F32),
                        pltpu.SemaphoreType.DMA((2, SAMPLE_RING)),
                        pltpu.VMEM((npg * page, d_kv), BF16), pltpu.VMEM((MLA_ROPE_DIM, npg * page), BF16),
                        pltpu.VMEM((n_part, rows, part_keys), F32), pltpu.VMEM((n_part, rows, 1), F32),
                        pltpu.VMEM((rows, 1), F32), pltpu.VMEM((rows, 1), F32), pltpu.VMEM((rows, d_kv), F32)],
    )
    return pl.pallas_call(
        functools.partial(_attn_sample_body, npg, page, ds, n_part),
        grid_spec=grid_spec,
        out_shape=jax.ShapeDtypeStruct((db, ds, d_out), F32),
        compiler_params=_cparams(("arbitrary", "arbitrary")),
        name="attn_sample",
    )(page_table, qc, qr, c_new, r_new, w_uv, g_mla, cache_c, cache_r)


def _cmul_add(cur, sh, a_r, a_i, half):
    return cur + a_r * sh + a_i * pltpu.roll(sh, half, axis=1)


def _ssm_prompt_body(nb, n_chunks, n_levels, pre, u_ref, um_ref, wy_ref, wd_ref, wc_ref, dv_ref, ar_ref, ai_ref,
                     y_ref, hl_ref, scr, e_scr):
    u = u_ref[0]
    ub = u.astype(BF16)
    half = wd_ref.shape[-1] // 2
    wd = wd_ref[0]
    d = jnp.dot(ub, wd, preferred_element_type=F32)
    h_meta = jnp.dot(um_ref[0].astype(BF16), wd, preferred_element_type=F32)[0:1]
    first = lax.broadcasted_iota(jnp.int32, (n_chunks, 1), 0) == 0
    for b in range(nb):
        scr[b, 0:pre, :] = jnp.zeros((pre, scr.shape[-1]), F32)
        scr[b, pre:pre + n_chunks, :] = d[b * n_chunks:(b + 1) * n_chunks]
    for b in range(nb):
        shifted = scr[b, pre - 1:pre - 1 + n_chunks, :]
        scr[b, pre:pre + n_chunks, :] = shifted + jnp.where(first, h_meta, 0.0)
    for k in range(n_levels):
        s = 1 << k
        for b in range(nb):
            cur = scr[b, pre:pre + n_chunks, :]
            sh = scr[b, pre - s:pre - s + n_chunks, :]
            scr[b, pre:pre + n_chunks, :] = _cmul_add(cur, sh, ar_ref[0, k:k + 1, :], ai_ref[0, k:k + 1, :], half)
    for b in range(nb):
        e_scr[b * n_chunks:(b + 1) * n_chunks, :] = scr[b, pre:pre + n_chunks, :]
    e = e_scr[...]
    y_ref[0] = (jnp.dot(ub, wy_ref[0], preferred_element_type=F32)
                + jnp.dot(e.astype(BF16), wc_ref[0], preferred_element_type=F32)
                + u * dv_ref[0])
    h_after = _cmul_add(d, e, ar_ref[0, 0:1, :], ai_ref[0, 0:1, :], half)
    for b in range(nb):
        last = (b + 1) * n_chunks - 1
        hl_ref[0, b:b + 1, :] = h_after[last:last + 1]


def _ssm_sample_body(u_ref, h_ref, wy_ref, wd_ref, wc_ref, dv_ref, ar_ref, ai_ref, y_ref, hl_ref):
    u = u_ref[0]
    ub = u.astype(BF16)
    e = h_ref[0]
    half = e.shape[-1] // 2
    d = jnp.dot(ub, wd_ref[0], preferred_element_type=F32)
    y_ref[0] = (jnp.dot(ub, wy_ref[0], preferred_element_type=F32)
                + jnp.dot(e.astype(BF16), wc_ref[0], preferred_element_type=F32)
                + u * dv_ref[0])
    hl_ref[0] = _cmul_add(d, e, ar_ref[0, 0:1, :], ai_ref[0, 0:1, :], half)


def _chunk_steps(n_groups, t_chunk, lane_order):
    j = jnp.arange(t_chunk, dtype=jnp.int32)[None, :]
    if not lane_order:
        return jnp.broadcast_to(j, (n_groups, t_chunk))
    per_slab = LANES // SSM_GROUP
    assert t_chunk % per_slab == 0
    gg = (jnp.arange(n_groups, dtype=jnp.int32) % per_slab)[:, None]
    return (j // per_slab) * per_slab + (j % per_slab - gg) % per_slab


def _ssm_weights(a_re, a_im, log_dt, b_re, b_im, c_re, c_im, d_skip, t_chunk, n_levels, lane_order=False):
    hi = lax.Precision.HIGHEST
    a = lax.complex(a_re.astype(F32), a_im.astype(F32))
    dt = jnp.exp(log_dt.astype(F32))[:, None]
    a_dt = a * dt
    a_bar = jnp.exp(a_dt)
    b_bar = ((a_bar - 1.0) / a)[..., None] * lax.complex(b_re.astype(F32), b_im.astype(F32))
    c = lax.complex(c_re.astype(F32), c_im.astype(F32))
    g, p_state, ch = b_bar.shape
    steps = _chunk_steps(g, t_chunk, lane_order)
    k = jnp.arange(t_chunk + 1, dtype=F32)
    a_pow = jnp.exp(a_dt[:, None, :] * k[None, :, None])
    pick = lambda idx: jnp.take_along_axis(a_pow, idx[:, :, None], axis=1)
    kern = jnp.einsum('gcp,gkp,gpd->gkcd', c, a_pow[:, :t_chunk], b_bar, precision=hi).real
    per = LANES // SSM_GROUP
    assert g % per == 0
    lag = steps[:per, None, :] - steps[:per, :, None]
    kern = kern.reshape(g // per, per, t_chunk, ch * ch).transpose(1, 2, 0, 3).reshape(per, t_chunk, -1)
    sel = jnp.clip(lag, 0, t_chunk - 1).reshape(per, t_chunk * t_chunk, 1)
    wy = jnp.take_along_axis(kern, sel, axis=1)
    wy = jnp.where((lag >= 0).reshape(per, t_chunk * t_chunk, 1), wy, 0.0)
    wy = wy.reshape(per, t_chunk, t_chunk, g // per, ch, ch)
    wy = wy.transpose(3, 0, 1, 5, 2, 4).reshape(g, t_chunk * ch, t_chunk * ch)
    wd = pick(t_chunk - 1 - steps)[:, :, None, :] * b_bar.transpose(0, 2, 1)[:, None]
    wd = wd.reshape(g, t_chunk * ch, p_state)
    wd = jnp.concatenate([wd.real, wd.imag], axis=-1)
    gm = c.transpose(0, 2, 1)[:, :, None, :] * pick(steps + 1).transpose(0, 2, 1)[:, :, :, None]
    gm = gm.reshape(g, p_state, t_chunk * ch)
    wc = jnp.concatenate([gm.real, -gm.imag], axis=1)
    dv = jnp.tile(d_skip.astype(F32).reshape(g, 1, ch), (1, 1, t_chunk))
    lev = (t_chunk * (2.0 ** jnp.arange(n_levels, dtype=F32)))
    a_lev = jnp.exp(a_dt[:, None, :] * lev[None, :, None])
    a_r = jnp.concatenate([a_lev.real, a_lev.real], axis=-1)
    a_i = jnp.concatenate([-a_lev.imag, a_lev.imag], axis=-1)
    return wy.astype(BF16), wd.astype(BF16), wc.astype(BF16), dv, a_r, a_i


def _ssm_prompt(u_rows, um_rows, nb, ops):
    wy, wd, wc, dv, a_r, a_i = ops
    g, r, tc = u_rows.shape
    n_chunks = r // nb
    n_levels = a_r.shape[1]
    assert (1 << n_levels) >= n_chunks and n_chunks % SUBLANES == 0
    st = wd.shape[-1]
    pre = -(-(1 << (n_levels - 1)) // SUBLANES) * SUBLANES
    gmap = lambda i: (i, 0, 0)
    return pl.pallas_call(
        functools.partial(_ssm_prompt_body, nb, n_chunks, n_levels, pre),
        grid=(g,),
        in_specs=[pl.BlockSpec((1, r, tc), gmap), pl.BlockSpec((1,) + um_rows.shape[1:], gmap),
                  pl.BlockSpec((1,) + wy.shape[1:], gmap),
                  pl.BlockSpec((1,) + wd.shape[1:], gmap), pl.BlockSpec((1,) + wc.shape[1:], gmap),
                  pl.BlockSpec((1,) + dv.shape[1:], gmap), pl.BlockSpec((1,) + a_r.shape[1:], gmap),
                  pl.BlockSpec((1,) + a_i.shape[1:], gmap)],
        out_specs=[pl.BlockSpec((1, r, tc), gmap), pl.BlockSpec((1, nb, st), gmap)],
        out_shape=[jax.ShapeDtypeStruct((g, r, tc), F32), jax.ShapeDtypeStruct((g, nb, st), F32)],
        scratch_shapes=[pltpu.VMEM((nb, pre + n_chunks, st), F32), pltpu.VMEM((r, st), F32)],
        compiler_params=_cparams(("parallel",)),
        name="ssm_prompt",
    )(u_rows, um_rows, wy, wd, wc, dv, a_r, a_i)


def _ssm_sample(u_rows, h_rows, ops):
    wy, wd, wc, dv, a_r, a_i = ops
    g, r, tc = u_rows.shape
    st = wd.shape[-1]
    gmap = lambda i: (i, 0, 0)
    return pl.pallas_call(
        _ssm_sample_body,
        grid=(g,),
        in_specs=[pl.BlockSpec((1, r, tc), gmap), pl.BlockSpec((1, r, st), gmap),
                  pl.BlockSpec((1,) + wy.shape[1:], gmap), pl.BlockSpec((1,) + wd.shape[1:], gmap),
                  pl.BlockSpec((1,) + wc.shape[1:], gmap), pl.BlockSpec((1,) + dv.shape[1:], gmap),
                  pl.BlockSpec((1,) + a_r.shape[1:], gmap), pl.BlockSpec((1,) + a_i.shape[1:], gmap)],
        out_specs=[pl.BlockSpec((1, r, tc), gmap), pl.BlockSpec((1, r, st), gmap)],
        out_shape=[jax.ShapeDtypeStruct((g, r, tc), F32), jax.ShapeDtypeStruct((g, r, st), F32)],
        compiler_params=_cparams(("parallel",)),
        name="ssm_sample",
    )(u_rows, h_rows, wy, wd, wc, dv, a_r, a_i)


def _mix_out_body(t_chunk, ys_ref, ya_ref, x_ref, wglu_ref, bglu_ref, gs_ref, wo_ref, gpost_ref, o_ref,
                  *maybe_scr):
    if t_chunk:
        (slab_scr,) = maybe_scr
        n_slabs = slab_scr.shape[0]
        n_rows = slab_scr.shape[1] // t_chunk
        per_slab = LANES // SSM_GROUP
        lane_blk = lax.broadcasted_iota(jnp.int32, (n_rows, LANES), 1) // SSM_GROUP
        for k in range(n_slabs):
            for t in range(t_chunk // per_slab):
                tiles = [ys_ref[k * per_slab + gg, :, t * LANES:(t + 1) * LANES] for gg in range(per_slab)]
                for s1 in range(per_slab):
                    w = tiles[(-s1) % per_slab]
                    for q in range(1, per_slab):
                        w = jnp.where(lane_blk == q, tiles[(q - s1) % per_slab], w)
                    if s1:
                        w = pltpu.roll(w, (per_slab - s1) * SSM_GROUP, axis=1)
                    slab_scr[k, pl.ds(t * per_slab + s1, n_rows, stride=t_chunk), :] = w
        ys = jnp.concatenate([slab_scr[k] for k in range(n_slabs)], axis=-1)
    else:
        ys = ys_ref[...]
    z = jax.nn.gelu(ys)
    gate = jax.nn.sigmoid(jnp.dot(z.astype(BF16), wglu_ref[...], preferred_element_type=F32) + bglu_ref[...])
    ns = _rms(z * gate, gs_ref[...]).astype(BF16)
    w = ns.shape[-1]
    y = (jnp.dot(ns, wo_ref[:w, :], preferred_element_type=F32)
         + jnp.dot(ya_ref[...].astype(BF16), wo_ref[w:, :], preferred_element_type=F32))
    o_ref[...] = x_ref[...] + _rms(y, gpost_ref[...])


def _mix_out(ys, ya, x, w_glu, b_glu, g_ssm, w_o, g_post, t_chunk, name):
    n, d = x.shape
    tm = _row_tile(n)
    row = lambda i: (i, 0)
    c2 = lambda i: (0, 0)
    if t_chunk:
        assert tm % (t_chunk * SUBLANES) == 0
        w = ys.shape[0] * SSM_GROUP
        assert w % LANES == 0
        ys_spec = pl.BlockSpec((ys.shape[0], tm // t_chunk, ys.shape[2]), lambda i: (0, i, 0))
        scratch = [pltpu.VMEM((w // LANES, tm, LANES), F32)]
    else:
        ys_spec = pl.BlockSpec((tm, ys.shape[1]), row)
        scratch = []
    return pl.pallas_call(
        functools.partial(_mix_out_body, t_chunk),
        grid=(n // tm,),
        in_specs=[ys_spec, pl.BlockSpec((tm, ya.shape[1]), row), pl.BlockSpec((tm, d), row),
                  pl.BlockSpec(w_glu.shape, c2), pl.BlockSpec(b_glu.shape, c2), pl.BlockSpec(g_ssm.shape, c2),
                  pl.BlockSpec(w_o.shape, c2), pl.BlockSpec(g_post.shape, c2)],
        out_specs=pl.BlockSpec((tm, d), row),
        out_shape=jax.ShapeDtypeStruct((n, d), F32),
        scratch_shapes=scratch,
        compiler_params=_cparams(("parallel",)),
        name=name,
    )(ys, ya, x, w_glu, b_glu, g_ssm, w_o, g_post)


def _rope_tables(pos):
    inv = ROPE_THETA ** (-jnp.arange(ROPE_HALF, dtype=F32) / ROPE_HALF)
    ang = pos.astype(F32)[:, None] * inv[None, :]
    reps = LANES // ROPE_HALF
    return jnp.tile(jnp.cos(ang), (1, reps)), jnp.tile(jnp.sin(ang), (1, reps))


def _regroup_uq_columns(w_uq):
    w = w_uq.reshape(w_uq.shape[0], MLA_HEADS, MLA_NOPE_DIM + MLA_ROPE_DIM)
    parts = (w[:, :, :MLA_NOPE_DIM], w[:, :, MLA_NOPE_DIM:MLA_NOPE_DIM + ROPE_HALF], w[:, :, MLA_NOPE_DIM + ROPE_HALF:])
    return jnp.concatenate([p.reshape(w_uq.shape[0], -1) for p in parts], axis=1)


def kernel(x_prompt, x_sample, cache_kv_latent, cache_k_rope, state_ssm_re, state_ssm_im, page_table, meta_tokens, g_ff1_pre, w_ff1_gate, w_ff1_up, w_ff1_down, g_ff1_post, g_mix_pre, w_in, ssm_a_re, ssm_a_im, ssm_log_dt, ssm_b_re, ssm_b_im, ssm_c_re, ssm_c_im, ssm_d, w_glu, b_glu, g_q_norm, w_uq, g_kv_norm, w_uk, w_uv, g_ssm_out, g_mla_out, w_o, g_mix_post, g_ff2_pre, w_ff2_gate, w_ff2_up, w_ff2_down, g_ff2_post):
    depth = w_in.shape[0]
    assert depth == 1, "single-layer step"
    bp, seq, d_model = x_prompt.shape
    db, ds, _ = x_sample.shape
    n_pages = page_table.shape[1]
    page = cache_kv_latent.shape[2]
    past_len = n_pages * page
    d_kv = cache_kv_latent.shape[3]
    d_q = w_uq.shape[1]
    n_groups, n_state = ssm_a_re.shape[1], ssm_a_re.shape[2]
    d_u = n_groups * SSM_GROUP
    dims = (d_u, d_q, d_kv)
    l = 0
    row = lambda v: v[l].reshape(1, -1).astype(F32)

    ff1_w = _ffn_weights(w_ff1_gate[l], w_ff1_up[l], w_ff1_down[l])
    ff2_w = _ffn_weights(w_ff2_gate[l], w_ff2_up[l], w_ff2_down[l])
    w_in_b = w_in[l].astype(BF16)
    w_uq_b = _regroup_uq_columns(w_uq[l]).astype(BF16)
    w_ukt = jnp.transpose(w_uk[l], (1, 2, 0)).astype(BF16)
    w_uv_b = jnp.transpose(w_uv[l], (1, 0, 2)).astype(BF16)
    w_uvt_b = w_uv[l].reshape(d_kv, -1).T.astype(BF16)
    w_glu_b = w_glu[l].astype(BF16)
    w_o_b = w_o[l].astype(BF16)

    xs = [x_prompt.reshape(bp * seq, d_model), x_sample.reshape(db * ds, d_model), meta_tokens.astype(F32)]
    names = ["prompt", "sample", "meta"]
    pos = [N_META + jnp.tile(jnp.arange(seq), bp), past_len + jnp.tile(jnp.arange(ds), db), jnp.arange(N_META)]

    t_p = SSM_CHUNK
    assert N_META == t_p and seq % t_p == 0
    x1, pr = [], []
    for x, nm, ps, t_rows in zip(xs, names, pos, (t_p, 0, 0)):
        y, hn = _ffn(x, row(g_ff1_pre), ff1_w, row(g_ff1_post), row(g_mix_pre), True, "ffn1_" + nm)
        cos, sin = _rope_tables(ps)
        x1.append(y)
        pr.append(_proj(hn, w_in_b, row(g_q_norm), w_uq_b, w_ukt, row(g_kv_norm), w_uvt_b, cos, sin, dims, t_rows,
                        "proj_" + nm))
    (_, ckv_p, ckvb_p, vt_p, kr_p, krb_p, qc_p, qr_p, u_rows) = pr[0]
    (u_s, ckv_s, _, _, kr_s, _, qc_s, qr_s) = pr[1]
    (u_m, ckv_m, ckvb_m, vt_m, kr_m, krb_m, _, _) = pr[2]

    n_levels = max(1, (seq // t_p - 1).bit_length())
    ssm_w = (ssm_a_re[l], ssm_a_im[l], ssm_log_dt[l], ssm_b_re[l], ssm_b_im[l], ssm_c_re[l], ssm_c_im[l], ssm_d[l])
    ops_p = _ssm_weights(*ssm_w, t_p, n_levels, True)
    ops_s = _ssm_weights(*ssm_w, ds, 1)
    tc = t_p * SSM_GROUP
    um_rows = u_m.reshape(t_p, n_groups, SSM_GROUP).transpose(1, 0, 2)
    um_rows = jnp.take_along_axis(um_rows, _chunk_steps(n_groups, t_p, True)[:, :, None], axis=1)
    um_rows = um_rows.reshape(n_groups, 1, tc)
    um_rows = jnp.pad(um_rows, ((0, 0), (0, SUBLANES - 1), (0, 0)))
    ys_p, hl_p = _ssm_prompt(u_rows, um_rows, bp, ops_p)
    hl_p = hl_p.transpose(1, 0, 2)

    us_rows = u_s.reshape(db, ds, n_groups, SSM_GROUP).transpose(2, 0, 1, 3).reshape(n_groups, db, ds * SSM_GROUP)
    h0_rows = jnp.concatenate([state_ssm_re[l], state_ssm_im[l]], axis=-1).astype(F32).transpose(1, 0, 2)
    ysr, hl_s = _ssm_sample(us_rows, h0_rows, ops_s)
    ys_s = ysr.reshape(n_groups, db, ds, SSM_GROUP).transpose(1, 2, 0, 3).reshape(db * ds, d_u)
    hl_s = hl_s.transpose(1, 0, 2)

    g_mla = row(g_mla_out)
    pad_m = LANES - N_META
    kmc = jnp.pad(ckvb_m, ((0, pad_m), (0, 0)))
    kmr = jnp.pad(krb_m, ((0, pad_m), (0, 0)))
    vmt = jnp.pad(vt_m, ((0, 0), (0, pad_m)))
    ya_p = _attn_prompt(qc_p, qr_p, ckvb_p.reshape(bp, seq, d_kv), vt_p,
                        krb_p.reshape(bp, seq, MLA_ROPE_DIM), kmc, vmt, kmr, g_mla)
    to_seq = lambda q: q.reshape(MLA_HEADS, db, ds, q.shape[-1]).transpose(1, 0, 2, 3).reshape(db, MLA_HEADS * ds, q.shape[-1])
    ya_s = _attn_sample(page_table, to_seq(qc_s), to_seq(qr_s), ckv_s.reshape(db, ds, d_kv),
                        kr_s.reshape(db, ds, MLA_ROPE_DIM), cache_kv_latent[l],
                        jnp.swapaxes(cache_k_rope[l], 1, 2), w_uv_b, g_mla)
    ya_s = ya_s.reshape(db * ds, -1)

    outs = []
    for x, ys, ya, t_rows, nm in ((x1[0], ys_p, ya_p, t_p, "prompt"), (x1[1], ys_s, ya_s, 0, "sample")):
        x2 = _mix_out(ys, ya, x, w_glu_b, row(b_glu), row(g_ssm_out), w_o_b, row(g_mix_post), t_rows,
                      "mix_out_" + nm)
        y, _ = _ffn(x2, row(g_ff2_pre), ff2_w, row(g_ff2_post), row(g_ff2_post), False, "ffn2_" + nm)
        outs.append(y)

    y_prompt = outs[0].reshape(bp, seq, d_model)
    y_sample = outs[1].reshape(db, ds, d_model)
    meta_b = lambda v: jnp.broadcast_to(v[None], (bp,) + v.shape)
    new_ckv_p = jnp.concatenate([meta_b(ckv_m), ckv_p.reshape(bp, seq, d_kv)], axis=1)[None]
    new_kr_p = jnp.concatenate([meta_b(kr_m), kr_p.reshape(bp, seq, MLA_ROPE_DIM)], axis=1)[None]
    return (y_prompt, y_sample, new_ckv_p, new_kr_p,
            hl_p[None, :, :, :n_state], hl_p[None, :, :, n_state:],
            ckv_s.reshape(1, db, ds, d_kv), kr_s.reshape(1, db, ds, MLA_ROPE_DIM),
            hl_s[None, :, :, :n_state], hl_s[None, :, :, n_state:])
```

```python
import functools
import math

import jax
import jax.numpy as jnp
from jax import lax
from jax.experimental import pallas as pl
from jax.experimental.pallas import tpu as pltpu

F32 = jnp.float32
BF16 = jnp.bfloat16

N_META = 16
SSM_GROUP = 16
MLA_HEADS = 8
MLA_NOPE_DIM = 64
MLA_ROPE_DIM = 32
ROPE_HALF = MLA_ROPE_DIM // 2
ROPE_THETA = 10000.0
RMS_EPS = 1e-6
ATTN_SCALE = (MLA_NOPE_DIM + MLA_ROPE_DIM) ** -0.5

LANES = 128
SUBLANES = 8
VMEM_LIMIT = 56 * 1024 * 1024

ROW_TILE = 512
FFN_ROW_TILE = 1024
FF_TILE = 256
ATTN_BQ = 256
ATTN_BK = 512
ATTN_COLS = 512
PAGES_PER_STEP = 128
SAMPLE_PARTS = 2
SAMPLE_RING = 2
SSM_CHUNK = 16


def _rms(x, g):
    return (x * lax.rsqrt(jnp.mean(x * x, axis=-1, keepdims=True) + RMS_EPS)) * g


def _row_tile(n):
    return ROW_TILE if n % ROW_TILE == 0 else n


def _cparams(sem):
    return pltpu.CompilerParams(dimension_semantics=sem, vmem_limit_bytes=VMEM_LIMIT)


def _ffn_body(emit_norm, x_ref, gpre_ref, wg_ref, wu_ref, wd_ref, gpost_ref, gnext_ref, *refs):
    if emit_norm:
        y_ref, hn_ref, h_scr, acc_scr = refs
    else:
        y_ref, h_scr, acc_scr = refs
        hn_ref = None
    j = pl.program_id(1)

    @pl.when(j == 0)
    def _():
        h_scr[...] = _rms(x_ref[...], gpre_ref[...]).astype(BF16)
        acc_scr[...] = jnp.zeros_like(acc_scr)

    h = h_scr[...]
    g = jnp.dot(h, wg_ref[...], preferred_element_type=F32)
    u = jnp.dot(h, wu_ref[...], preferred_element_type=F32)
    a = (g * jax.nn.sigmoid(g)) * u
    acc_scr[...] += jnp.dot(a.astype(BF16), wd_ref[...], preferred_element_type=F32)

    @pl.when(j == pl.num_programs(1) - 1)
    def _():
        y = x_ref[...] + 0.5 * _rms(acc_scr[...], gpost_ref[...])
        y_ref[...] = y
        if emit_norm:
            hn_ref[...] = _rms(y, gnext_ref[...]).astype(BF16)


def _ffn_weights(wg, wu, wd):
    d, d_ff = wg.shape
    tf = FF_TILE
    assert d_ff % tf == 0
    chunk = lambda w: w.astype(BF16).reshape(d, d_ff // tf, tf).transpose(1, 0, 2)
    return chunk(wg), chunk(wu), wd.astype(BF16).reshape(d_ff // tf, tf, d)


def _ffn(x, g_pre, weights, g_post, g_next, emit_norm, name):
    wg, wu, wd = weights
    n, d = x.shape
    nj, _, tf = wg.shape
    tm = FFN_ROW_TILE if n % FFN_ROW_TILE == 0 else n
    row = lambda i, j: (i, 0)
    vec = lambda i, j: (0, 0)
    chunk = lambda i, j: (j, 0, 0)
    out_shape = [jax.ShapeDtypeStruct((n, d), F32)]
    out_specs = [pl.BlockSpec((tm, d), row)]
    if emit_norm:
        out_shape.append(jax.ShapeDtypeStruct((n, d), BF16))
        out_specs.append(pl.BlockSpec((tm, d), row))
    res = pl.pallas_call(
        functools.partial(_ffn_body, emit_norm),
        grid=(n // tm, nj),
        in_specs=[
            pl.BlockSpec((tm, d), row),
            pl.BlockSpec((1, d), vec),
            pl.BlockSpec((None, d, tf), chunk),
            pl.BlockSpec((None, d, tf), chunk),
            pl.BlockSpec((None, tf, d), chunk),
            pl.BlockSpec((1, d), vec),
            pl.BlockSpec((1, d), vec),
        ],
        out_specs=out_specs,
        out_shape=out_shape,
        scratch_shapes=[pltpu.VMEM((tm, d), BF16), pltpu.VMEM((tm, d), F32)],
        compiler_params=_cparams(("parallel", "arbitrary")),
        name=name,
    )(x, g_pre, wg, wu, wd, g_post, g_next)
    return res if emit_norm else (res[0], None)


def _proj_body(d_u, d_q, d_kv, t_chunk, h_ref, win_ref, gq_ref, wuq_ref, wukt_ref, gkv_ref, wuvt_ref,
               cos_ref, sin_ref,
               u_ref, ckv_ref, ckvb_ref, vt_ref, kr_ref, krb_ref, qc_ref, qr_ref, *maybe_urows):
    proj = jnp.dot(h_ref[...], win_ref[...], preferred_element_type=F32)
    off_q, off_kv, off_kr = d_u, d_u + d_q, d_u + d_q + d_kv
    u_ref[...] = proj[:, :off_q]
    if t_chunk:
        urows_ref, slab_scr = maybe_urows
        n_rows = u_ref.shape[0] // t_chunk
        per_slab = LANES // SSM_GROUP
        lane_blk = lax.broadcasted_iota(jnp.int32, (n_rows, LANES), 1) // SSM_GROUP
        for k in range(d_u // LANES):
            slab_scr[k] = proj[:, k * LANES:(k + 1) * LANES]
        for k in range(d_u // LANES):
            rolled = []
            for s in range(t_chunk):
                x = slab_scr[k, pl.ds(s, n_rows, stride=t_chunk), :]
                shift = (s % per_slab) * SSM_GROUP
                rolled.append(pltpu.roll(x, shift, axis=1) if shift else x)
            for gg in range(per_slab):
                tiles = []
                for t in range(t_chunk // per_slab):
                    tile = rolled[t * per_slab]
                    for s1 in range(1, per_slab):
                        tile = jnp.where(lane_blk == (gg + s1) % per_slab, rolled[t * per_slab + s1], tile)
                    tiles.append(tile)
                urows_ref[k * per_slab + gg] = jnp.concatenate(tiles, axis=-1)
    cq = _rms(proj[:, off_q:off_kv], gq_ref[...]).astype(BF16)
    q = jnp.dot(cq, wuq_ref[...], preferred_element_type=F32) * ATTN_SCALE
    n_nope = MLA_HEADS * MLA_NOPE_DIM
    for hd in range(MLA_HEADS):
        qn = q[:, hd * MLA_NOPE_DIM:(hd + 1) * MLA_NOPE_DIM].astype(BF16)
        qc_ref[hd] = jnp.dot(qn, wukt_ref[hd], preferred_element_type=F32).astype(BF16)
    cos = cos_ref[...]
    sin = sin_ref[...]
    r1 = q[:, n_nope:n_nope + LANES]
    r2 = q[:, n_nope + LANES:n_nope + 2 * LANES]
    o1 = r1 * cos - r2 * sin
    o2 = r2 * cos + r1 * sin
    for hd in range(MLA_HEADS):
        sl = slice(hd * ROPE_HALF, (hd + 1) * ROPE_HALF)
        qr_ref[hd] = jnp.concatenate([o1[:, sl], o2[:, sl]], axis=-1).astype(BF16)
    ckv = _rms(proj[:, off_kv:off_kr], gkv_ref[...])
    ckv_ref[...] = ckv
    ckvb = ckv.astype(BF16)
    ckvb_ref[...] = ckvb
    vt_ref[...] = lax.dot_general(wuvt_ref[...], ckvb, (((1,), (1,)), ((), ())),
                                  preferred_element_type=F32).astype(BF16)
    x1 = proj[:, off_kr:off_kr + ROPE_HALF]
    x2 = proj[:, off_kr + ROPE_HALF:off_kr + MLA_ROPE_DIM]
    c16 = cos[:, :ROPE_HALF]
    s16 = sin[:, :ROPE_HALF]
    kr = jnp.concatenate([x1 * c16 - x2 * s16, x2 * c16 + x1 * s16], axis=-1)
    kr_ref[...] = kr
    krb_ref[...] = kr.astype(BF16)


def _proj(hn, w_in, g_q, w_uq, w_ukt, g_kv, w_uvt, cos, sin, dims, t_chunk, name):
    n, d = hn.shape
    d_u, d_q, d_kv = dims
    tm = _row_tile(n)
    row = lambda i: (i, 0)
    full2 = lambda i: (0, 0)
    full3 = lambda i: (0, 0, 0)
    hrow = lambda i: (0, i, 0)
    extra_specs, extra_shapes, scratch = [], [], []
    if t_chunk:
        assert tm % (t_chunk * SUBLANES) == 0 and d_u % LANES == 0
        n_groups = d_u // SSM_GROUP
        extra_specs = [pl.BlockSpec((n_groups, tm // t_chunk, t_chunk * SSM_GROUP), hrow)]
        extra_shapes = [jax.ShapeDtypeStruct((n_groups, n // t_chunk, t_chunk * SSM_GROUP), F32)]
        scratch = [pltpu.VMEM((d_u // LANES, tm, LANES), F32)]
    return pl.pallas_call(
        functools.partial(_proj_body, d_u, d_q, d_kv, t_chunk),
        grid=(n // tm,),
        in_specs=[
            pl.BlockSpec((tm, d), row),
            pl.BlockSpec(w_in.shape, full2),
            pl.BlockSpec(g_q.shape, full2),
            pl.BlockSpec(w_uq.shape, full2),
            pl.BlockSpec(w_ukt.shape, full3),
            pl.BlockSpec(g_kv.shape, full2),
            pl.BlockSpec(w_uvt.shape, full2),
            pl.BlockSpec((tm, LANES), row),
            pl.BlockSpec((tm, LANES), row),
        ],
        out_specs=[
            pl.BlockSpec((tm, d_u), row),
            pl.BlockSpec((tm, d_kv), row),
            pl.BlockSpec((tm, d_kv), row),
            pl.BlockSpec((w_uvt.shape[0], tm), lambda i: (0, i)),
            pl.BlockSpec((tm, MLA_ROPE_DIM), row),
            pl.BlockSpec((tm, MLA_ROPE_DIM), row),
            pl.BlockSpec((MLA_HEADS, tm, d_kv), hrow),
            pl.BlockSpec((MLA_HEADS, tm, MLA_ROPE_DIM), hrow),
        ] + extra_specs,
        out_shape=[
            jax.ShapeDtypeStruct((n, d_u), F32),
            jax.ShapeDtypeStruct((n, d_kv), F32),
            jax.ShapeDtypeStruct((n, d_kv), BF16),
            jax.ShapeDtypeStruct((w_uvt.shape[0], n), BF16),
            jax.ShapeDtypeStruct((n, MLA_ROPE_DIM), F32),
            jax.ShapeDtypeStruct((n, MLA_ROPE_DIM), BF16),
            jax.ShapeDtypeStruct((MLA_HEADS, n, d_kv), BF16),
            jax.ShapeDtypeStruct((MLA_HEADS, n, MLA_ROPE_DIM), BF16),
        ] + extra_shapes,
        scratch_shapes=scratch,
        compiler_params=_cparams(("parallel",)),
        name=name,
    )(hn, w_in, g_q, w_uq, w_ukt, g_kv, w_uvt, cos, sin)


def _scores(qc, qr, kc, kr):
    nt = (((1,), (1,)), ((), ()))
    return (lax.dot_general(qc, kc, nt, preferred_element_type=F32)
            + lax.dot_general(qr, kr, nt, preferred_element_type=F32))


def _softmax_step(s, kc, m_scr, l_scr, acc_scr):
    m_old = m_scr[...]
    m_new = jnp.maximum(m_old, jnp.max(s, axis=-1, keepdims=True))
    alpha = jnp.exp(m_old - m_new)
    p = jnp.exp(s - m_new)
    l_scr[...] = alpha * l_scr[...] + jnp.sum(p, axis=-1, keepdims=True)
    acc_scr[...] = alpha * acc_scr[...] + jnp.dot(p.astype(BF16), kc, preferred_element_type=F32)
    m_scr[...] = m_new


def _attn_finish(rows_per_head, wuv_ref, g_ref, l_scr, acc_scr):
    o = (acc_scr[...] / l_scr[...]).astype(BF16)
    outs = []
    for hd in range(MLA_HEADS):
        oh = o[hd * rows_per_head:(hd + 1) * rows_per_head]
        outs.append(jnp.dot(oh, wuv_ref[hd], preferred_element_type=F32))
    return _rms(jnp.concatenate(outs, axis=-1), g_ref[...])


def _attn_cols_step(parts, m_blk, cols, heads, bq, m_scr, l_scr, acc_scr):
    m_old = m_scr[:, cols]
    m_new = jnp.maximum(m_old, m_blk)
    alpha = jnp.exp(m_old - m_new)
    ps = [jnp.exp(s - m_new) for s, _ in parts]
    l_new = alpha * l_scr[:, cols]
    for p in ps:
        l_new = l_new + jnp.sum(p, axis=0, keepdims=True)
    l_scr[:, cols] = l_new
    m_scr[:, cols] = m_new
    v_dim = acc_scr.shape[0] // MLA_HEADS
    for n, hd in enumerate(heads):
        hc = slice(n * bq, (n + 1) * bq)
        rows = slice(hd * v_dim, (hd + 1) * v_dim)
        acc = alpha[:, hc] * acc_scr[rows, :]
        for p, (_, vt) in zip(ps, parts):
            acc = acc + jnp.dot(vt[rows, :], p[:, hc].astype(BF16), preferred_element_type=F32)
        acc_scr[rows, :] = acc


def _attn_prompt_body(bq, bk, ncol, qc_ref, qr_ref, qcn_ref, qrn_ref, kc_ref, vt_ref, kr_ref, kmc_ref, vmt_ref,
                      kmr_ref, g_ref, o_ref, m_scr, l_scr, acc_scr, s_scr, mb_scr):
    i = pl.program_id(1)
    nt = (((1,), (1,)), ((), ()))
    heads_per = ncol // bq
    n_groups = MLA_HEADS // heads_per
    groups = [slice(gi * ncol, (gi + 1) * ncol) for gi in range(n_groups)]
    own_q = (qc_ref, qr_ref)
    next_q = (qcn_ref, qrn_ref)

    def scores(gi, kc, kr, q_refs=own_q):
        hs = slice(gi * heads_per, (gi + 1) * heads_per)
        qc = q_refs[0][hs].reshape(ncol, qc_ref.shape[-1])
        qr = q_refs[1][hs].reshape(ncol, qr_ref.shape[-1])
        return (lax.dot_general(kc, qc, nt, preferred_element_type=F32)
                + lax.dot_general(kr, qr, nt, preferred_element_type=F32))

    def produce(j, slot, q_refs=own_q):
        start = pl.multiple_of(j * bk, bk)
        kc = kc_ref[0, pl.ds(start, bk), :]
        kr = kr_ref[0, pl.ds(start, bk), :]
        for gi, cols in enumerate(groups):
            s = scores(gi, kc, kr, q_refs)
            s_scr[slot, :, cols] = s
            mb_scr[slot, :, cols] = jnp.max(s, axis=0, keepdims=True)

    def meta_scores():
        out = []
        for gi in range(n_groups):
            sm = scores(gi, kmc_ref[...], kmr_ref[...])
            out.append(jnp.where(lax.broadcasted_iota(jnp.int32, sm.shape, 0) < N_META, sm, -jnp.inf))
        return out

    def consume(j, slot, diagonal, meta=None):
        start = pl.multiple_of(j * bk, bk)
        vt = vt_ref[:, pl.ds(start, bk)]
        for gi, cols in enumerate(groups):
            s = s_scr[slot, :, cols]
            if diagonal:
                k_pos = start + lax.broadcasted_iota(jnp.int32, s.shape, 0)
                q_pos = i * bq + (lax.broadcasted_iota(jnp.int32, s.shape, 1) & (bq - 1))
                s = jnp.where(k_pos <= q_pos, s, -jnp.inf)
                sm = meta[gi]
                m_blk = jnp.maximum(jnp.max(s, axis=0, keepdims=True), jnp.max(sm, axis=0, keepdims=True))
                parts = [(s, vt), (sm, vmt_ref[...])]
            else:
                m_blk = mb_scr[slot, :, cols]
                parts = [(s, vt)]
            heads = range(gi * heads_per, (gi + 1) * heads_per)
            _attn_cols_step(parts, m_blk, cols, heads, bq, m_scr, l_scr, acc_scr)

    ratio = bk // bq
    n_full = i // ratio
    odd = (n_full & 1) == 1
    m_scr[...] = jnp.full_like(m_scr, -jnp.inf)
    l_scr[...] = jnp.zeros_like(l_scr)
    acc_scr[...] = jnp.zeros_like(acc_scr)

    base = ((i // (2 * ratio)) * ratio + jnp.minimum(lax.rem(i, 2 * ratio), ratio)) & 1

    @pl.when(i == 0)
    def _():
        produce(0, 0)

    def run(b0):
        def pair(k, carry):
            j = 2 * k
            produce(j + 1, 1 - b0)
            consume(j, b0, False)
            produce(j + 2, b0)
            consume(j + 1, 1 - b0, False)
            return carry

        lax.fori_loop(0, n_full // 2, pair, 0)

        @pl.when(odd)
        def _():
            produce(n_full, 1 - b0)
            consume(n_full - 1, b0, False)
            meta = meta_scores()
            produce(0, b0, next_q)
            consume(n_full, 1 - b0, True, meta)

        @pl.when(jnp.logical_not(odd))
        def _():
            meta = meta_scores()
            produce(0, 1 - b0, next_q)
            consume(n_full, b0, True, meta)

    for b0 in (0, 1):
        pl.when(base == b0)(functools.partial(run, b0))

    v_dim = acc_scr.shape[0] // MLA_HEADS
    outs = [acc_scr[hd * v_dim:(hd + 1) * v_dim, :] / l_scr[:, hd * bq:(hd + 1) * bq] for hd in range(MLA_HEADS)]
    y = jnp.transpose(jnp.concatenate(outs, axis=0))
    o_ref[...] = _rms(y, g_ref[...]).astype(o_ref.dtype)


def _attn_prompt(qc, qr, kc, vt, kr, kmc, vmt, kmr, g_mla):
    nb, seq, d_kv = kc.shape
    bq, bk, ncol = ATTN_BQ, ATTN_BK, ATTN_COLS
    assert seq % bk == 0 and bk % bq == 0 and bq & (bq - 1) == 0 and N_META >= 1
    assert ncol % bq == 0 and (MLA_HEADS * bq) % ncol == 0
    nq = seq // bq
    rows = MLA_HEADS * bq
    d_out = vt.shape[0]
    qmap = lambda b, i: (0, b * nq + i, 0)
    qnext = lambda b, i: (0, b * nq + jnp.minimum(i + 1, nq - 1), 0)
    kmap = lambda b, i: (b, 0, 0)
    c2 = lambda b, i: (0, 0)
    return pl.pallas_call(
        functools.partial(_attn_prompt_body, bq, bk, ncol),
        grid=(nb, nq),
        in_specs=[
            pl.BlockSpec((MLA_HEADS, bq, d_kv), qmap),
            pl.BlockSpec((MLA_HEADS, bq, MLA_ROPE_DIM), qmap),
            pl.BlockSpec((MLA_HEADS, bq, d_kv), qnext),
            pl.BlockSpec((MLA_HEADS, bq, MLA_ROPE_DIM), qnext),
            pl.BlockSpec((1, seq, d_kv), kmap),
            pl.BlockSpec((d_out, seq), lambda b, i: (0, b)),
            pl.BlockSpec((1, seq, MLA_ROPE_DIM), kmap),
            pl.BlockSpec(kmc.shape, c2),
            pl.BlockSpec(vmt.shape, c2),
            pl.BlockSpec(kmr.shape, c2),
            pl.BlockSpec(g_mla.shape, c2),
        ],
        out_specs=pl.BlockSpec((bq, d_out), lambda b, i: (b * nq + i, 0)),
        out_shape=jax.ShapeDtypeStruct((nb * seq, d_out), BF16),
        scratch_shapes=[pltpu.VMEM((1, rows), F32), pltpu.VMEM((1, rows), F32),
                        pltpu.VMEM((d_out, bq), F32),
                        pltpu.VMEM((2, bk, rows), F32), pltpu.VMEM((2, 1, rows), F32)],
        compiler_params=_cparams(("parallel", "arbitrary")),
        name="attn_prompt",
    )(qc, qr, qc, qr, kc, vt, kr, kmc, vmt, kmr, g_mla)


def _attn_sample_body(npg, page, ds, n_part, pt_ref, qc_ref, qr_ref, cn_ref, rn_ref, wuv_ref, g_ref,
                      cache_c, cache_r, o_ref, pc_buf, pr_buf, sems, kc_scr, krt_scr, s_scr, mb_scr,
                      m_scr, l_scr, acc_scr):
    b = pl.program_id(0)
    j = pl.program_id(1)
    n_steps = pl.num_programs(1)
    step = b * n_steps + j
    n_total = pl.num_programs(0) * n_steps
    ring = pc_buf.shape[0]
    slot = lax.rem(step, ring)
    qc = qc_ref[0]
    qr = qr_ref[0]
    nt = (((1,), (1,)), ((), ()))
    per = npg // n_part

    def page_copies(bb, jj, sl):
        out = []
        for pg in range(npg):
            idx = pt_ref[bb, jj * npg + pg]
            out.append(pltpu.make_async_copy(cache_c.at[idx], pc_buf.at[sl, pg], sems.at[0, sl]))
            out.append(pltpu.make_async_copy(cache_r.at[idx], pr_buf.at[sl, pg], sems.at[1, sl]))
        return out

    for ahead in range(ring - 1):
        @pl.when(jnp.logical_and(step == 0, ahead < n_total))
        def _(ahead=ahead):
            for cp in page_copies(jnp.int32(ahead) // n_steps, lax.rem(jnp.int32(ahead), n_steps), ahead):
                cp.start()

    nxt = step + (ring - 1)

    @pl.when(nxt < n_total)
    def _():
        for cp in page_copies(nxt // n_steps, lax.rem(nxt, n_steps), lax.rem(nxt, ring)):
            cp.start()

    for cp in page_copies(b, j, slot):
        cp.wait()
    pc_refs = [pc_buf.at[slot, pg] for pg in range(npg)]
    pr_refs = [pr_buf.at[slot, pg] for pg in range(npg)]

    @pl.when(j == 0)
    def _():
        m_scr[...] = jnp.full_like(m_scr, -jnp.inf)
        l_scr[...] = jnp.zeros_like(l_scr)
        acc_scr[...] = jnp.zeros_like(acc_scr)

    def keys(part):
        return slice(part * per * page, (part + 1) * per * page)

    def produce(part):
        for pg in range(part * per, (part + 1) * per):
            kc_scr[pg * page:(pg + 1) * page, :] = pc_refs[pg][...].astype(BF16)
            krt_scr[:, pg * page:(pg + 1) * page] = pr_refs[pg][...].astype(BF16)
        s = (lax.dot_general(qc, kc_scr[keys(part), :], nt, preferred_element_type=F32)
             + jnp.dot(qr, krt_scr[:, keys(part)], preferred_element_type=F32))
        s_scr[part] = s
        mb_scr[part] = jnp.max(s, axis=-1, keepdims=True)

    def consume(part):
        s = s_scr[part]
        m_old = m_scr[...]
        m_new = jnp.maximum(m_old, mb_scr[part])
        alpha = jnp.exp(m_old - m_new)
        p = jnp.exp(s - m_new)
        l_scr[...] = alpha * l_scr[...] + jnp.sum(p, axis=-1, keepdims=True)
        acc_scr[...] = alpha * acc_scr[...] + jnp.dot(p.astype(BF16), kc_scr[keys(part), :],
                                                      preferred_element_type=F32)
        m_scr[...] = m_new

    produce(0)
    for part in range(n_part):
        if part + 1 < n_part:
            produce(part + 1)
        consume(part)

    @pl.when(j == pl.num_programs(1) - 1)
    def _():
        pad = LANES - ds
        kn = jnp.concatenate([cn_ref[0], jnp.zeros((pad, cn_ref.shape[-1]), F32)], axis=0).astype(BF16)
        rn = jnp.concatenate([rn_ref[0], jnp.zeros((pad, rn_ref.shape[-1]), F32)], axis=0).astype(BF16)
        s = _scores(qc, qr, kn, rn)
        t_q = lax.broadcasted_iota(jnp.int32, s.shape, 0) & (ds - 1)
        t_k = lax.broadcasted_iota(jnp.int32, s.shape, 1)
        s = jnp.where(t_k <= t_q, s, -jnp.inf)
        _softmax_step(s, kn, m_scr, l_scr, acc_scr)
        o_ref[0] = _attn_finish(ds, wuv_ref, g_ref, l_scr, acc_scr)


def _attn_sample(page_table, qc, qr, c_new, r_new, cache_c, cache_r, w_uv, g_mla):
    db, rows, d_kv = qc.shape
    ds = c_new.shape[1]
    n_pages = page_table.shape[1]
    page = cache_c.shape[1]
    npg = math.gcd(PAGES_PER_STEP, n_pages)
    n_part = math.gcd(SAMPLE_PARTS, npg)
    assert ds & (ds - 1) == 0 and ds <= LANES
    assert cache_r.shape[1:] == (MLA_ROPE_DIM, page)
    d_out = w_uv.shape[0] * w_uv.shape[2]
    bmap = lambda b, j, pt: (b, 0, 0)
    c2 = lambda b, j, pt: (0, 0)
    c3 = lambda b, j, pt: (0, 0, 0)
    in_specs = [
        pl.BlockSpec((1, rows, d_kv), bmap),
        pl.BlockSpec((1, rows, MLA_ROPE_DIM), bmap),
        pl.BlockSpec((1, ds, d_kv), bmap),
        pl.BlockSpec((1, ds, MLA_ROPE_DIM), bmap),
        pl.BlockSpec(w_uv.shape, c3),
        pl.BlockSpec(g_mla.shape, c2),
        pl.BlockSpec(memory_space=pl.ANY),
        pl.BlockSpec(memory_space=pl.ANY),
    ]
    part_keys = (npg // n_part) * page
    grid_spec = pltpu.PrefetchScalarGridSpec(
        num_scalar_prefetch=1,
        grid=(db, n_pages // npg),
        in_specs=in_specs,
        out_specs=pl.BlockSpec((1, ds, d_out), bmap),
        scratch_shapes=[pltpu.VMEM((SAMPLE_RING, npg, page, d_kv), F32),
                        pltpu.VMEM((SAMPLE_RING, npg, MLA_ROPE_DIM, page), F32),
                        pltpu.SemaphoreType.DMA((2, SAMPLE_RING)),
                        pltpu.VMEM((npg * page, d_kv), BF16), pltpu.VMEM((MLA_ROPE_DIM, npg * page), BF16),
                        pltpu.VMEM((n_part, rows, part_keys), F32), pltpu.VMEM((n_part, rows, 1), F32),
                        pltpu.VMEM((rows, 1), F32), pltpu.VMEM((rows, 1), F32), pltpu.VMEM((rows, d_kv), F32)],
    )
    return pl.pallas_call(
        functools.partial(_attn_sample_body, npg, page, ds, n_part),
        grid_spec=grid_spec,
        out_shape=jax.ShapeDtypeStruct((db, ds, d_out), F32),
        compiler_params=_cparams(("arbitrary", "arbitrary")),
        name="attn_sample",
    )(page_table, qc, qr, c_new, r_new, w_uv, g_mla, cache_c, cache_r)


def _cmul_add(cur, sh, a_r, a_i, half):
    return cur + a_r * sh + a_i * pltpu.roll(sh, half, axis=1)


def _ssm_prompt_body(nb, n_chunks, n_levels, pre, u_ref, um_ref, wy_ref, wd_ref, wc_ref, dv_ref, ar_ref, ai_ref,
                     y_ref, hl_ref, scr, e_scr):
    u = u_ref[0]
    ub = u.astype(BF16)
    half = wd_ref.shape[-1] // 2
    wd = wd_ref[0]
    d = jnp.dot(ub, wd, preferred_element_type=F32)
    h_meta = jnp.dot(um_ref[0].astype(BF16), wd, preferred_element_type=F32)[0:1]
    first = lax.broadcasted_iota(jnp.int32, (n_chunks, 1), 0) == 0
    for b in range(nb):
        scr[b, 0:pre, :] = jnp.zeros((pre, scr.shape[-1]), F32)
        scr[b, pre:pre + n_chunks, :] = d[b * n_chunks:(b + 1) * n_chunks]
    for b in range(nb):
        shifted = scr[b, pre - 1:pre - 1 + n_chunks, :]
        scr[b, pre:pre + n_chunks, :] = shifted + jnp.where(first, h_meta, 0.0)
    for k in range(n_levels):
        s = 1 << k
        for b in range(nb):
            cur = scr[b, pre:pre + n_chunks, :]
            sh = scr[b, pre - s:pre - s + n_chunks, :]
            scr[b, pre:pre + n_chunks, :] = _cmul_add(cur, sh, ar_ref[0, k:k + 1, :], ai_ref[0, k:k + 1, :], half)
    for b in range(nb):
        e_scr[b * n_chunks:(b + 1) * n_chunks, :] = scr[b, pre:pre + n_chunks, :]
    e = e_scr[...]
    y_ref[0] = (jnp.dot(ub, wy_ref[0], preferred_element_type=F32)
                + jnp.dot(e.astype(BF16), wc_ref[0], preferred_element_type=F32)
                + u * dv_ref[0])
    h_after = _cmul_add(d, e, ar_ref[0, 0:1, :], ai_ref[0, 0:1, :], half)
    for b in range(nb):
        last = (b + 1) * n_chunks - 1
        hl_ref[0, b:b + 1, :] = h_after[last:last + 1]


def _ssm_sample_body(u_ref, h_ref, wy_ref, wd_ref, wc_ref, dv_ref, ar_ref, ai_ref, y_ref, hl_ref):
    u = u_ref[0]
    ub = u.astype(BF16)
    e = h_ref[0]
    half = e.shape[-1] // 2
    d = jnp.dot(ub, wd_ref[0], preferred_element_type=F32)
    y_ref[0] = (jnp.dot(ub, wy_ref[0], preferred_element_type=F32)
                + jnp.dot(e.astype(BF16), wc_ref[0], preferred_element_type=F32)
                + u * dv_ref[0])
    hl_ref[0] = _cmul_add(d, e, ar_ref[0, 0:1, :], ai_ref[0, 0:1, :], half)


def _chunk_steps(n_groups, t_chunk, lane_order):
    j = jnp.arange(t_chunk, dtype=jnp.int32)[None, :]
    if not lane_order:
        return jnp.broadcast_to(j, (n_groups, t_chunk))
    per_slab = LANES // SSM_GROUP
    assert t_chunk % per_slab == 0
    gg = (jnp.arange(n_groups, dtype=jnp.int32) % per_slab)[:, None]
    return (j // per_slab) * per_slab + (j % per_slab - gg) % per_slab


def _ssm_weights(a_re, a_im, log_dt, b_re, b_im, c_re, c_im, d_skip, t_chunk, n_levels, lane_order=False):
    hi = lax.Precision.HIGHEST
    a = lax.complex(a_re.astype(F32), a_im.astype(F32))
    dt = jnp.exp(log_dt.astype(F32))[:, None]
    a_dt = a * dt
    a_bar = jnp.exp(a_dt)
    b_bar = ((a_bar - 1.0) / a)[..., None] * lax.complex(b_re.astype(F32), b_im.astype(F32))
    c = lax.complex(c_re.astype(F32), c_im.astype(F32))
    g, p_state, ch = b_bar.shape
    steps = _chunk_steps(g, t_chunk, lane_order)
    k = jnp.arange(t_chunk + 1, dtype=F32)
    a_pow = jnp.exp(a_dt[:, None, :] * k[None, :, None])
    pick = lambda idx: jnp.take_along_axis(a_pow, idx[:, :, None], axis=1)
    kern = jnp.einsum('gcp,gkp,gpd->gkdc', c, a_pow[:, :t_chunk], b_bar, precision=hi).real
    per = LANES // SSM_GROUP
    assert g % per == 0
    lag = steps[:per, None, :] - steps[:per, :, None]
    kern = kern.reshape(g // per, per, t_chunk, ch * ch).transpose(1, 2, 0, 3).reshape(per, t_chunk, -1)
    sel = jnp.clip(lag, 0, t_chunk - 1).reshape(per, t_chunk * t_chunk, 1)
    wy = jnp.take_along_axis(kern, sel, axis=1)
    wy = jnp.where((lag >= 0).reshape(per, t_chunk * t_chunk, 1), wy, 0.0)
    wy = wy.reshape(per, t_chunk, t_chunk, g // per, ch, ch)
    wy = wy.transpose(3, 0, 1, 4, 2, 5).reshape(g, t_chunk * ch, t_chunk * ch)
    wd = pick(t_chunk - 1 - steps)[:, :, None, :] * b_bar.transpose(0, 2, 1)[:, None]
    wd = wd.reshape(g, t_chunk * ch, p_state)
    wd = jnp.concatenate([wd.real, wd.imag], axis=-1)
    gm = c.transpose(0, 2, 1)[:, :, None, :] * pick(steps + 1).transpose(0, 2, 1)[:, :, :, None]
    gm = gm.reshape(g, p_state, t_chunk * ch)
    wc = jnp.concatenate([gm.real, -gm.imag], axis=1)
    dv = jnp.tile(d_skip.astype(F32).reshape(g, 1, ch), (1, 1, t_chunk))
    lev = (t_chunk * (2.0 ** jnp.arange(n_levels, dtype=F32)))
    a_lev = jnp.exp(a_dt[:, None, :] * lev[None, :, None])
    a_r = jnp.concatenate([a_lev.real, a_lev.real], axis=-1)
    a_i = jnp.concatenate([-a_lev.imag, a_lev.imag], axis=-1)
    return wy.astype(BF16), wd.astype(BF16), wc.astype(BF16), dv, a_r, a_i


def _ssm_prompt(u_rows, um_rows, nb, ops):
    wy, wd, wc, dv, a_r, a_i = ops
    g, r, tc = u_rows.shape
    n_chunks = r // nb
    n_levels = a_r.shape[1]
    assert (1 << n_levels) >= n_chunks and n_chunks % SUBLANES == 0
    st = wd.shape[-1]
    pre = -(-(1 << (n_levels - 1)) // SUBLANES) * SUBLANES
    gmap = lambda i: (i, 0, 0)
    return pl.pallas_call(
        functools.partial(_ssm_prompt_body, nb, n_chunks, n_levels, pre),
        grid=(g,),
        in_specs=[pl.BlockSpec((1, r, tc), gmap), pl.BlockSpec((1,) + um_rows.shape[1:], gmap),
                  pl.BlockSpec((1,) + wy.shape[1:], gmap),
                  pl.BlockSpec((1,) + wd.shape[1:], gmap), pl.BlockSpec((1,) + wc.shape[1:], gmap),
                  pl.BlockSpec((1,) + dv.shape[1:], gmap), pl.BlockSpec((1,) + a_r.shape[1:], gmap),
                  pl.BlockSpec((1,) + a_i.shape[1:], gmap)],
        out_specs=[pl.BlockSpec((1, r, tc), gmap), pl.BlockSpec((1, nb, st), gmap)],
        out_shape=[jax.ShapeDtypeStruct((g, r, tc), F32), jax.ShapeDtypeStruct((g, nb, st), F32)],
        scratch_shapes=[pltpu.VMEM((nb, pre + n_chunks, st), F32), pltpu.VMEM((r, st), F32)],
        compiler_params=_cparams(("parallel",)),
        name="ssm_prompt",
    )(u_rows, um_rows, wy, wd, wc, dv, a_r, a_i)


def _ssm_sample(u_rows, h_rows, ops):
    wy, wd, wc, dv, a_r, a_i = ops
    g, r, tc = u_rows.shape
    st = wd.shape[-1]
    gmap = lambda i: (i, 0, 0)
    return pl.pallas_call(
        _ssm_sample_body,
        grid=(g,),
        in_specs=[pl.BlockSpec((1, r, tc), gmap), pl.BlockSpec((1, r, st), gmap),
                  pl.BlockSpec((1,) + wy.shape[1:], gmap), pl.BlockSpec((1,) + wd.shape[1:], gmap),
                  pl.BlockSpec((1,) + wc.shape[1:], gmap), pl.BlockSpec((1,) + dv.shape[1:], gmap),
                  pl.BlockSpec((1,) + a_r.shape[1:], gmap), pl.BlockSpec((1,) + a_i.shape[1:], gmap)],
        out_specs=[pl.BlockSpec((1, r, tc), gmap), pl.BlockSpec((1, r, st), gmap)],
        out_shape=[jax.ShapeDtypeStruct((g, r, tc), F32), jax.ShapeDtypeStruct((g, r, st), F32)],
        compiler_params=_cparams(("parallel",)),
        name="ssm_sample",
    )(u_rows, h_rows, wy, wd, wc, dv, a_r, a_i)


def _mix_out_body(t_chunk, ys_ref, ya_ref, x_ref, wglu_ref, bglu_ref, gs_ref, wo_ref, gpost_ref, o_ref,
                  *maybe_scr):
    if t_chunk:
        (slab_scr,) = maybe_scr
        n_slabs = slab_scr.shape[0]
        n_rows = slab_scr.shape[1] // t_chunk
        per_slab = LANES // SSM_GROUP
        lane_blk = lax.broadcasted_iota(jnp.int32, (n_rows, LANES), 1) // SSM_GROUP
        for k in range(n_slabs):
            for t in range(t_chunk // per_slab):
                tiles = [ys_ref[k * per_slab + gg, :, t * LANES:(t + 1) * LANES] for gg in range(per_slab)]
                for s1 in range(per_slab):
                    w = tiles[(-s1) % per_slab]
                    for q in range(1, per_slab):
                        w = jnp.where(lane_blk == q, tiles[(q - s1) % per_slab], w)
                    if s1:
                        w = pltpu.roll(w, (per_slab - s1) * SSM_GROUP, axis=1)
                    slab_scr[k, pl.ds(t * per_slab + s1, n_rows, stride=t_chunk), :] = w
        ys = jnp.concatenate([slab_scr[k] for k in range(n_slabs)], axis=-1)
    else:
        ys = ys_ref[...]
    z = jax.nn.gelu(ys)
    gate = jax.nn.sigmoid(jnp.dot(z.astype(BF16), wglu_ref[...], preferred_element_type=F32) + bglu_ref[...])
    ns = _rms(z * gate, gs_ref[...]).astype(BF16)
    w = ns.shape[-1]
    y = (jnp.dot(ns, wo_ref[:w, :], preferred_element_type=F32)
         + jnp.dot(ya_ref[...].astype(BF16), wo_ref[w:, :], preferred_element_type=F32))
    o_ref[...] = x_ref[...] + _rms(y, gpost_ref[...])


def _mix_out(ys, ya, x, w_glu, b_glu, g_ssm, w_o, g_post, t_chunk, name):
    n, d = x.shape
    tm = _row_tile(n)
    row = lambda i: (i, 0)
    c2 = lambda i: (0, 0)
    if t_chunk:
        assert tm % (t_chunk * SUBLANES) == 0
        w = ys.shape[0] * SSM_GROUP
        assert w % LANES == 0
        ys_spec = pl.BlockSpec((ys.shape[0], tm // t_chunk, ys.shape[2]), lambda i: (0, i, 0))
        scratch = [pltpu.VMEM((w // LANES, tm, LANES), F32)]
    else:
        ys_spec = pl.BlockSpec((tm, ys.shape[1]), row)
        scratch = []
    return pl.pallas_call(
        functools.partial(_mix_out_body, t_chunk),
        grid=(n // tm,),
        in_specs=[ys_spec, pl.BlockSpec((tm, ya.shape[1]), row), pl.BlockSpec((tm, d), row),
                  pl.BlockSpec(w_glu.shape, c2), pl.BlockSpec(b_glu.shape, c2), pl.BlockSpec(g_ssm.shape, c2),
                  pl.BlockSpec(w_o.shape, c2), pl.BlockSpec(g_post.shape, c2)],
        out_specs=pl.BlockSpec((tm, d), row),
        out_shape=jax.ShapeDtypeStruct((n, d), F32),
        scratch_shapes=scratch,
        compiler_params=_cparams(("parallel",)),
        name=name,
    )(ys, ya, x, w_glu, b_glu, g_ssm, w_o, g_post)


def _rope_tables(pos):
    inv = ROPE_THETA ** (-jnp.arange(ROPE_HALF, dtype=F32) / ROPE_HALF)
    ang = pos.astype(F32)[:, None] * inv[None, :]
    reps = LANES // ROPE_HALF
    return jnp.tile(jnp.cos(ang), (1, reps)), jnp.tile(jnp.sin(ang), (1, reps))


def _regroup_uq_columns(w_uq):
    w = w_uq.reshape(w_uq.shape[0], MLA_HEADS, MLA_NOPE_DIM + MLA_ROPE_DIM)
    parts = (w[:, :, :MLA_NOPE_DIM], w[:, :, MLA_NOPE_DIM:MLA_NOPE_DIM + ROPE_HALF], w[:, :, MLA_NOPE_DIM + ROPE_HALF:])
    return jnp.concatenate([p.reshape(w_uq.shape[0], -1) for p in parts], axis=1)


def kernel(x_prompt, x_sample, cache_kv_latent, cache_k_rope, state_ssm_re, state_ssm_im, page_table, meta_tokens, g_ff1_pre, w_ff1_gate, w_ff1_up, w_ff1_down, g_ff1_post, g_mix_pre, w_in, ssm_a_re, ssm_a_im, ssm_log_dt, ssm_b_re, ssm_b_im, ssm_c_re, ssm_c_im, ssm_d, w_glu, b_glu, g_q_norm, w_uq, g_kv_norm, w_uk, w_uv, g_ssm_out, g_mla_out, w_o, g_mix_post, g_ff2_pre, w_ff2_gate, w_ff2_up, w_ff2_down, g_ff2_post):
    depth = w_in.shape[0]
    assert depth == 1, "single-layer step"
    bp, seq, d_model = x_prompt.shape
    db, ds, _ = x_sample.shape
    n_pages = page_table.shape[1]
    page = cache_kv_latent.shape[2]
    past_len = n_pages * page
    d_kv = cache_kv_latent.shape[3]
    d_q = w_uq.shape[1]
    n_groups, n_state = ssm_a_re.shape[1], ssm_a_re.shape[2]
    d_u = n_groups * SSM_GROUP
    dims = (d_u, d_q, d_kv)
    l = 0
    row = lambda v: v[l].reshape(1, -1).astype(F32)

    ff1_w = _ffn_weights(w_ff1_gate[l], w_ff1_up[l], w_ff1_down[l])
    ff2_w = _ffn_weights(w_ff2_gate[l], w_ff2_up[l], w_ff2_down[l])
    w_in_b = w_in[l].astype(BF16)
    w_uq_b = _regroup_uq_columns(w_uq[l]).astype(BF16)
    w_ukt = jnp.transpose(w_uk[l], (1, 2, 0)).astype(BF16)
    w_uv_b = jnp.transpose(w_uv[l], (1, 0, 2)).astype(BF16)
    w_uvt_b = w_uv[l].reshape(d_kv, -1).T.astype(BF16)
    w_glu_b = w_glu[l].astype(BF16)
    w_o_b = w_o[l].astype(BF16)

    xs = [x_prompt.reshape(bp * seq, d_model), x_sample.reshape(db * ds, d_model), meta_tokens.astype(F32)]
    names = ["prompt", "sample", "meta"]
    pos = [N_META + jnp.tile(jnp.arange(seq), bp), past_len + jnp.tile(jnp.arange(ds), db), jnp.arange(N_META)]

    t_p = SSM_CHUNK
    assert N_META == t_p and seq % t_p == 0
    x1, pr = [], []
    for x, nm, ps, t_rows in zip(xs, names, pos, (t_p, 0, 0)):
        y, hn = _ffn(x, row(g_ff1_pre), ff1_w, row(g_ff1_post), row(g_mix_pre), True, "ffn1_" + nm)
        cos, sin = _rope_tables(ps)
        x1.append(y)
        pr.append(_proj(hn, w_in_b, row(g_q_norm), w_uq_b, w_ukt, row(g_kv_norm), w_uvt_b, cos, sin, dims, t_rows,
                        "proj_" + nm))
    (_, ckv_p, ckvb_p, vt_p, kr_p, krb_p, qc_p, qr_p, u_rows) = pr[0]
    (u_s, ckv_s, _, _, kr_s, _, qc_s, qr_s) = pr[1]
    (u_m, ckv_m, ckvb_m, vt_m, kr_m, krb_m, _, _) = pr[2]

    n_levels = max(1, (seq // t_p - 1).bit_length())
    ssm_w = (ssm_a_re[l], ssm_a_im[l], ssm_log_dt[l], ssm_b_re[l], ssm_b_im[l], ssm_c_re[l], ssm_c_im[l], ssm_d[l])
    ops_p = _ssm_weights(*ssm_w, t_p, n_levels, True)
    ops_s = _ssm_weights(*ssm_w, ds, 1)
    tc = t_p * SSM_GROUP
    um_rows = u_m.reshape(t_p, n_groups, SSM_GROUP).transpose(1, 0, 2)
    um_rows = jnp.take_along_axis(um_rows, _chunk_steps(n_groups, t_p, True)[:, :, None], axis=1)
    um_rows = um_rows.reshape(n_groups, 1, tc)
    um_rows = jnp.pad(um_rows, ((0, 0), (0, SUBLANES - 1), (0, 0)))
    ys_p, hl_p = _ssm_prompt(u_rows, um_rows, bp, ops_p)
    hl_p = hl_p.transpose(1, 0, 2)

    us_rows = u_s.reshape(db, ds, n_groups, SSM_GROUP).transpose(2, 0, 1, 3).reshape(n_groups, db, ds * SSM_GROUP)
    h0_rows = jnp.concatenate([state_ssm_re[l], state_ssm_im[l]], axis=-1).astype(F32).transpose(1, 0, 2)
    ysr, hl_s = _ssm_sample(us_rows, h0_rows, ops_s)
    ys_s = ysr.reshape(n_groups, db, ds, SSM_GROUP).transpose(1, 2, 0, 3).reshape(db * ds, d_u)
    hl_s = hl_s.transpose(1, 0, 2)

    g_mla = row(g_mla_out)
    pad_m = LANES - N_META
    kmc = jnp.pad(ckvb_m, ((0, pad_m), (0, 0)))
    kmr = jnp.pad(krb_m, ((0, pad_m), (0, 0)))
    vmt = jnp.pad(vt_m, ((0, 0), (0, pad_m)))
    ya_p = _attn_prompt(qc_p, qr_p, ckvb_p.reshape(bp, seq, d_kv), vt_p,
                        krb_p.reshape(bp, seq, MLA_ROPE_DIM), kmc, vmt, kmr, g_mla)
    to_seq = lambda q: q.reshape(MLA_HEADS, db, ds, q.shape[-1]).transpose(1, 0, 2, 3).reshape(db, MLA_HEADS * ds, q.shape[-1])
    ya_s = _attn_sample(page_table, to_seq(qc_s), to_seq(qr_s), ckv_s.reshape(db, ds, d_kv),
                        kr_s.reshape(db, ds, MLA_ROPE_DIM), cache_kv_latent[l],
                        jnp.swapaxes(cache_k_rope[l], 1, 2), w_uv_b, g_mla)
    ya_s = ya_s.reshape(db * ds, -1)

    outs = []
    for x, ys, ya, t_rows, nm in ((x1[0], ys_p, ya_p, t_p, "prompt"), (x1[1], ys_s, ya_s, 0, "sample")):
        x2 = _mix_out(ys, ya, x, w_glu_b, row(b_glu), row(g_ssm_out), w_o_b, row(g_mix_post), t_rows,
                      "mix_out_" + nm)
        y, _ = _ffn(x2, row(g_ff2_pre), ff2_w, row(g_ff2_post), row(g_ff2_post), False, "ffn2_" + nm)
        outs.append(y)

    y_prompt = outs[0].reshape(bp, seq, d_model)
    y_sample = outs[1].reshape(db, ds, d_model)
    meta_b = lambda v: jnp.broadcast_to(v[None], (bp,) + v.shape)
    new_ckv_p = jnp.concatenate([meta_b(ckv_m), ckv_p.reshape(bp, seq, d_kv)], axis=1)[None]
    new_kr_p = jnp.concatenate([meta_b(kr_m), kr_p.reshape(bp, seq, MLA_ROPE_DIM)], axis=1)[None]
    return (y_prompt, y_sample, new_ckv_p, new_kr_p,
            hl_p[None, :, :, :n_state], hl_p[None, :, :, n_state:],
            ckv_s.reshape(1, db, ds, d_kv), kr_s.reshape(1, db, ds, MLA_ROPE_DIM),
            hl_s[None, :, :, :n_state], hl_s[None, :, :, n_state:])
```

```python
import functools
import math

import jax
import jax.numpy as jnp
from jax import lax
from jax.experimental import pallas as pl
from jax.experimental.pallas import tpu as pltpu

F32 = jnp.float32
BF16 = jnp.bfloat16

N_META = 16
SSM_GROUP = 16
MLA_HEADS = 8
MLA_NOPE_DIM = 64
MLA_ROPE_DIM = 32
ROPE_HALF = MLA_ROPE_DIM // 2
ROPE_THETA = 10000.0
RMS_EPS = 1e-6
ATTN_SCALE = (MLA_NOPE_DIM + MLA_ROPE_DIM) ** -0.5

LANES = 128
SUBLANES = 8
VMEM_LIMIT = 56 * 1024 * 1024

ROW_TILE = 512
FFN_ROW_TILE = 1024
FF_TILE = 1408
ATTN_BQ = 256
ATTN_BK = 512
ATTN_COLS = 512
PAGES_PER_STEP = 128
SAMPLE_PARTS = 2
SAMPLE_RING = 2
SSM_CHUNK = 16


def _rms(x, g):
    return (x * lax.rsqrt(jnp.mean(x * x, axis=-1, keepdims=True) + RMS_EPS)) * g


def _row_tile(n):
    return ROW_TILE if n % ROW_TILE == 0 else n


def _cparams(sem):
    return pltpu.CompilerParams(dimension_semantics=sem, vmem_limit_bytes=VMEM_LIMIT)


def _ffn_body(emit_norm, x_ref, gpre_ref, wg_ref, wu_ref, wd_ref, gpost_ref, gnext_ref, *refs):
    if emit_norm:
        y_ref, hn_ref, h_scr, acc_scr = refs
    else:
        y_ref, h_scr, acc_scr = refs
        hn_ref = None
    j = pl.program_id(1)

    @pl.when(j == 0)
    def _():
        h_scr[...] = _rms(x_ref[...], gpre_ref[...]).astype(BF16)
        acc_scr[...] = jnp.zeros_like(acc_scr)

    h = h_scr[...]
    g = jnp.dot(h, wg_ref[...], preferred_element_type=F32)
    u = jnp.dot(h, wu_ref[...], preferred_element_type=F32)
    a = (g * jax.nn.sigmoid(g)) * u
    acc_scr[...] += jnp.dot(a.astype(BF16), wd_ref[...], preferred_element_type=F32)

    @pl.when(j == pl.num_programs(1) - 1)
    def _():
        y = x_ref[...] + 0.5 * _rms(acc_scr[...], gpost_ref[...])
        y_ref[...] = y
        if emit_norm:
            hn_ref[...] = _rms(y, gnext_ref[...]).astype(BF16)


def _ffn_weights(wg, wu, wd):
    d, d_ff = wg.shape
    tf = FF_TILE
    assert d_ff % tf == 0
    chunk = lambda w: w.astype(BF16).reshape(d, d_ff // tf, tf).transpose(1, 0, 2)
    return chunk(wg), chunk(wu), wd.astype(BF16).reshape(d_ff // tf, tf, d)


def _ffn(x, g_pre, weights, g_post, g_next, emit_norm, name):
    wg, wu, wd = weights
    n, d = x.shape
    nj, _, tf = wg.shape
    tm = FFN_ROW_TILE if n % FFN_ROW_TILE == 0 else n
    row = lambda i, j: (i, 0)
    vec = lambda i, j: (0, 0)
    chunk = lambda i, j: (j, 0, 0)
    out_shape = [jax.ShapeDtypeStruct((n, d), F32)]
    out_specs = [pl.BlockSpec((tm, d), row)]
    if emit_norm:
        out_shape.append(jax.ShapeDtypeStruct((n, d), BF16))
        out_specs.append(pl.BlockSpec((tm, d), row))
    res = pl.pallas_call(
        functools.partial(_ffn_body, emit_norm),
        grid=(n // tm, nj),
        in_specs=[
            pl.BlockSpec((tm, d), row),
            pl.BlockSpec((1, d), vec),
            pl.BlockSpec((None, d, tf), chunk),
            pl.BlockSpec((None, d, tf), chunk),
            pl.BlockSpec((None, tf, d), chunk),
            pl.BlockSpec((1, d), vec),
            pl.BlockSpec((1, d), vec),
        ],
        out_specs=out_specs,
        out_shape=out_shape,
        scratch_shapes=[pltpu.VMEM((tm, d), BF16), pltpu.VMEM((tm, d), F32)],
        compiler_params=_cparams(("parallel", "arbitrary")),
        name=name,
    )(x, g_pre, wg, wu, wd, g_post, g_next)
    return res if emit_norm else (res[0], None)


def _proj_body(d_u, d_q, d_kv, t_chunk, h_ref, win_ref, gq_ref, wuq_ref, wukt_ref, gkv_ref, wuvt_ref,
               cos_ref, sin_ref,
               u_ref, ckv_ref, ckvb_ref, vt_ref, kr_ref, krb_ref, qc_ref, qr_ref, *maybe_urows):
    proj = jnp.dot(h_ref[...], win_ref[...], preferred_element_type=F32)
    off_q, off_kv, off_kr = d_u, d_u + d_q, d_u + d_q + d_kv
    u_ref[...] = proj[:, :off_q]
    if t_chunk:
        urows_ref, slab_scr = maybe_urows
        n_rows = u_ref.shape[0] // t_chunk
        per_slab = LANES // SSM_GROUP
        lane_blk = lax.broadcasted_iota(jnp.int32, (n_rows, LANES), 1) // SSM_GROUP
        for k in range(d_u // LANES):
            slab_scr[k] = proj[:, k * LANES:(k + 1) * LANES]
        for k in range(d_u // LANES):
            rolled = []
            for s in range(t_chunk):
                x = slab_scr[k, pl.ds(s, n_rows, stride=t_chunk), :]
                shift = (s % per_slab) * SSM_GROUP
                rolled.append(pltpu.roll(x, shift, axis=1) if shift else x)
            for gg in range(per_slab):
                tiles = []
                for t in range(t_chunk // per_slab):
                    tile = rolled[t * per_slab]
                    for s1 in range(1, per_slab):
                        tile = jnp.where(lane_blk == (gg + s1) % per_slab, rolled[t * per_slab + s1], tile)
                    tiles.append(tile)
                urows_ref[k * per_slab + gg] = jnp.concatenate(tiles, axis=-1)
    cq = _rms(proj[:, off_q:off_kv], gq_ref[...]).astype(BF16)
    q = jnp.dot(cq, wuq_ref[...], preferred_element_type=F32) * ATTN_SCALE
    n_nope = MLA_HEADS * MLA_NOPE_DIM
    for hd in range(MLA_HEADS):
        qn = q[:, hd * MLA_NOPE_DIM:(hd + 1) * MLA_NOPE_DIM].astype(BF16)
        qc_ref[hd] = jnp.dot(qn, wukt_ref[hd], preferred_element_type=F32).astype(BF16)
    cos = cos_ref[...]
    sin = sin_ref[...]
    r1 = q[:, n_nope:n_nope + LANES]
    r2 = q[:, n_nope + LANES:n_nope + 2 * LANES]
    o1 = r1 * cos - r2 * sin
    o2 = r2 * cos + r1 * sin
    for hd in range(MLA_HEADS):
        sl = slice(hd * ROPE_HALF, (hd + 1) * ROPE_HALF)
        qr_ref[hd] = jnp.concatenate([o1[:, sl], o2[:, sl]], axis=-1).astype(BF16)
    ckv = _rms(proj[:, off_kv:off_kr], gkv_ref[...])
    ckv_ref[...] = ckv
    ckvb = ckv.astype(BF16)
    ckvb_ref[...] = ckvb
    vt_ref[...] = lax.dot_general(wuvt_ref[...], ckvb, (((1,), (1,)), ((), ())),
                                  preferred_element_type=F32).astype(BF16)
    x1 = proj[:, off_kr:off_kr + ROPE_HALF]
    x2 = proj[:, off_kr + ROPE_HALF:off_kr + MLA_ROPE_DIM]
    c16 = cos[:, :ROPE_HALF]
    s16 = sin[:, :ROPE_HALF]
    kr = jnp.concatenate([x1 * c16 - x2 * s16, x2 * c16 + x1 * s16], axis=-1)
    kr_ref[...] = kr
    krb_ref[...] = kr.astype(BF16)


def _proj(hn, w_in, g_q, w_uq, w_ukt, g_kv, w_uvt, cos, sin, dims, t_chunk, name):
    n, d = hn.shape
    d_u, d_q, d_kv = dims
    tm = _row_tile(n)
    row = lambda i: (i, 0)
    full2 = lambda i: (0, 0)
    full3 = lambda i: (0, 0, 0)
    hrow = lambda i: (0, i, 0)
    extra_specs, extra_shapes, scratch = [], [], []
    if t_chunk:
        assert tm % (t_chunk * SUBLANES) == 0 and d_u % LANES == 0
        n_groups = d_u // SSM_GROUP
        extra_specs = [pl.BlockSpec((n_groups, tm // t_chunk, t_chunk * SSM_GROUP), hrow)]
        extra_shapes = [jax.ShapeDtypeStruct((n_groups, n // t_chunk, t_chunk * SSM_GROUP), F32)]
        scratch = [pltpu.VMEM((d_u // LANES, tm, LANES), F32)]
    return pl.pallas_call(
        functools.partial(_proj_body, d_u, d_q, d_kv, t_chunk),
        grid=(n // tm,),
        in_specs=[
            pl.BlockSpec((tm, d), row),
            pl.BlockSpec(w_in.shape, full2),
            pl.BlockSpec(g_q.shape, full2),
            pl.BlockSpec(w_uq.shape, full2),
            pl.BlockSpec(w_ukt.shape, full3),
            pl.BlockSpec(g_kv.shape, full2),
            pl.BlockSpec(w_uvt.shape, full2),
            pl.BlockSpec((tm, LANES), row),
            pl.BlockSpec((tm, LANES), row),
        ],
        out_specs=[
            pl.BlockSpec((tm, d_u), row),
            pl.BlockSpec((tm, d_kv), row),
            pl.BlockSpec((tm, d_kv), row),
            pl.BlockSpec((w_uvt.shape[0], tm), lambda i: (0, i)),
            pl.BlockSpec((tm, MLA_ROPE_DIM), row),
            pl.BlockSpec((tm, MLA_ROPE_DIM), row),
            pl.BlockSpec((MLA_HEADS, tm, d_kv), hrow),
            pl.BlockSpec((MLA_HEADS, tm, MLA_ROPE_DIM), hrow),
        ] + extra_specs,
        out_shape=[
            jax.ShapeDtypeStruct((n, d_u), F32),
            jax.ShapeDtypeStruct((n, d_kv), F32),
            jax.ShapeDtypeStruct((n, d_kv), BF16),
            jax.ShapeDtypeStruct((w_uvt.shape[0], n), BF16),
            jax.ShapeDtypeStruct((n, MLA_ROPE_DIM), F32),
            jax.ShapeDtypeStruct((n, MLA_ROPE_DIM), BF16),
            jax.ShapeDtypeStruct((MLA_HEADS, n, d_kv), BF16),
            jax.ShapeDtypeStruct((MLA_HEADS, n, MLA_ROPE_DIM), BF16),
        ] + extra_shapes,
        scratch_shapes=scratch,
        compiler_params=_cparams(("parallel",)),
        name=name,
    )(hn, w_in, g_q, w_uq, w_ukt, g_kv, w_uvt, cos, sin)


def _scores(qc, qr, kc, kr):
    nt = (((1,), (1,)), ((), ()))
    return (lax.dot_general(qc, kc, nt, preferred_element_type=F32)
            + lax.dot_general(qr, kr, nt, preferred_element_type=F32))


def _softmax_step(s, kc, m_scr, l_scr, acc_scr):
    m_old = m_scr[...]
    m_new = jnp.maximum(m_old, jnp.max(s, axis=-1, keepdims=True))
    alpha = jnp.exp(m_old - m_new)
    p = jnp.exp(s - m_new)
    l_scr[...] = alpha * l_scr[...] + jnp.sum(p, axis=-1, keepdims=True)
    acc_scr[...] = alpha * acc_scr[...] + jnp.dot(p.astype(BF16), kc, preferred_element_type=F32)
    m_scr[...] = m_new


def _attn_finish(rows_per_head, wuv_ref, g_ref, l_scr, acc_scr):
    o = (acc_scr[...] / l_scr[...]).astype(BF16)
    outs = []
    for hd in range(MLA_HEADS):
        oh = o[hd * rows_per_head:(hd + 1) * rows_per_head]
        outs.append(jnp.dot(oh, wuv_ref[hd], preferred_element_type=F32))
    return _rms(jnp.concatenate(outs, axis=-1), g_ref[...])


def _attn_cols_step(parts, m_blk, cols, heads, bq, m_scr, l_scr, acc_scr):
    m_old = m_scr[:, cols]
    m_new = jnp.maximum(m_old, m_blk)
    alpha = jnp.exp(m_old - m_new)
    ps = [jnp.exp(s - m_new) for s, _ in parts]
    l_new = alpha * l_scr[:, cols]
    for p in ps:
        l_new = l_new + jnp.sum(p, axis=0, keepdims=True)
    l_scr[:, cols] = l_new
    m_scr[:, cols] = m_new
    v_dim = acc_scr.shape[0] // MLA_HEADS
    for n, hd in enumerate(heads):
        hc = slice(n * bq, (n + 1) * bq)
        rows = slice(hd * v_dim, (hd + 1) * v_dim)
        acc = alpha[:, hc] * acc_scr[rows, :]
        for p, (_, vt) in zip(ps, parts):
            acc = acc + jnp.dot(vt[rows, :], p[:, hc].astype(BF16), preferred_element_type=F32)
        acc_scr[rows, :] = acc


def _attn_prompt_body(bq, bk, ncol, qc_ref, qr_ref, qcn_ref, qrn_ref, kc_ref, vt_ref, kr_ref, kmc_ref, vmt_ref,
                      kmr_ref, g_ref, o_ref, m_scr, l_scr, acc_scr, s_scr, mb_scr):
    i = pl.program_id(1)
    nt = (((1,), (1,)), ((), ()))
    heads_per = ncol // bq
    n_groups = MLA_HEADS // heads_per
    groups = [slice(gi * ncol, (gi + 1) * ncol) for gi in range(n_groups)]
    own_q = (qc_ref, qr_ref)
    next_q = (qcn_ref, qrn_ref)

    def scores(gi, kc, kr, q_refs=own_q):
        hs = slice(gi * heads_per, (gi + 1) * heads_per)
        qc = q_refs[0][hs].reshape(ncol, qc_ref.shape[-1])
        qr = q_refs[1][hs].reshape(ncol, qr_ref.shape[-1])
        return (lax.dot_general(kc, qc, nt, preferred_element_type=F32)
                + lax.dot_general(kr, qr, nt, preferred_element_type=F32))

    def produce(j, slot, q_refs=own_q):
        start = pl.multiple_of(j * bk, bk)
        kc = kc_ref[0, pl.ds(start, bk), :]
        kr = kr_ref[0, pl.ds(start, bk), :]
        for gi, cols in enumerate(groups):
            s = scores(gi, kc, kr, q_refs)
            s_scr[slot, :, cols] = s
            mb_scr[slot, :, cols] = jnp.max(s, axis=0, keepdims=True)

    def meta_scores():
        out = []
        for gi in range(n_groups):
            sm = scores(gi, kmc_ref[...], kmr_ref[...])
            out.append(jnp.where(lax.broadcasted_iota(jnp.int32, sm.shape, 0) < N_META, sm, -jnp.inf))
        return out

    def consume(j, slot, diagonal, meta=None):
        start = pl.multiple_of(j * bk, bk)
        vt = vt_ref[:, pl.ds(start, bk)]
        for gi, cols in enumerate(groups):
            s = s_scr[slot, :, cols]
            if diagonal:
                k_pos = start + lax.broadcasted_iota(jnp.int32, s.shape, 0)
                q_pos = i * bq + (lax.broadcasted_iota(jnp.int32, s.shape, 1) & (bq - 1))
                s = jnp.where(k_pos <= q_pos, s, -jnp.inf)
                sm = meta[gi]
                m_blk = jnp.maximum(jnp.max(s, axis=0, keepdims=True), jnp.max(sm, axis=0, keepdims=True))
                parts = [(s, vt), (sm, vmt_ref[...])]
            else:
                m_blk = mb_scr[slot, :, cols]
                parts = [(s, vt)]
            heads = range(gi * heads_per, (gi + 1) * heads_per)
            _attn_cols_step(parts, m_blk, cols, heads, bq, m_scr, l_scr, acc_scr)

    ratio = bk // bq
    n_full = i // ratio
    odd = (n_full & 1) == 1
    m_scr[...] = jnp.full_like(m_scr, -jnp.inf)
    l_scr[...] = jnp.zeros_like(l_scr)
    acc_scr[...] = jnp.zeros_like(acc_scr)

    base = ((i // (2 * ratio)) * ratio + jnp.minimum(lax.rem(i, 2 * ratio), ratio)) & 1

    @pl.when(i == 0)
    def _():
        produce(0, 0)

    def run(b0):
        def pair(k, carry):
            j = 2 * k
            produce(j + 1, 1 - b0)
            consume(j, b0, False)
            produce(j + 2, b0)
            consume(j + 1, 1 - b0, False)
            return carry

        lax.fori_loop(0, n_full // 2, pair, 0)

        @pl.when(odd)
        def _():
            produce(n_full, 1 - b0)
            consume(n_full - 1, b0, False)
            meta = meta_scores()
            produce(0, b0, next_q)
            consume(n_full, 1 - b0, True, meta)

        @pl.when(jnp.logical_not(odd))
        def _():
            meta = meta_scores()
            produce(0, 1 - b0, next_q)
            consume(n_full, b0, True, meta)

    for b0 in (0, 1):
        pl.when(base == b0)(functools.partial(run, b0))

    v_dim = acc_scr.shape[0] // MLA_HEADS
    outs = [acc_scr[hd * v_dim:(hd + 1) * v_dim, :] / l_scr[:, hd * bq:(hd + 1) * bq] for hd in range(MLA_HEADS)]
    y = jnp.transpose(jnp.concatenate(outs, axis=0))
    o_ref[...] = _rms(y, g_ref[...]).astype(o_ref.dtype)


def _attn_prompt(qc, qr, kc, vt, kr, kmc, vmt, kmr, g_mla):
    nb, seq, d_kv = kc.shape
    bq, bk, ncol = ATTN_BQ, ATTN_BK, ATTN_COLS
    assert seq % bk == 0 and bk % bq == 0 and bq & (bq - 1) == 0 and N_META >= 1
    assert ncol % bq == 0 and (MLA_HEADS * bq) % ncol == 0
    nq = seq // bq
    rows = MLA_HEADS * bq
    d_out = vt.shape[0]
    qmap = lambda b, i: (0, b * nq + i, 0)
    qnext = lambda b, i: (0, b * nq + jnp.minimum(i + 1, nq - 1), 0)
    kmap = lambda b, i: (b, 0, 0)
    c2 = lambda b, i: (0, 0)
    return pl.pallas_call(
        functools.partial(_attn_prompt_body, bq, bk, ncol),
        grid=(nb, nq),
        in_specs=[
            pl.BlockSpec((MLA_HEADS, bq, d_kv), qmap),
            pl.BlockSpec((MLA_HEADS, bq, MLA_ROPE_DIM), qmap),
            pl.BlockSpec((MLA_HEADS, bq, d_kv), qnext),
            pl.BlockSpec((MLA_HEADS, bq, MLA_ROPE_DIM), qnext),
            pl.BlockSpec((1, seq, d_kv), kmap),
            pl.BlockSpec((d_out, seq), lambda b, i: (0, b)),
            pl.BlockSpec((1, seq, MLA_ROPE_DIM), kmap),
            pl.BlockSpec(kmc.shape, c2),
            pl.BlockSpec(vmt.shape, c2),
            pl.BlockSpec(kmr.shape, c2),
            pl.BlockSpec(g_mla.shape, c2),
        ],
        out_specs=pl.BlockSpec((bq, d_out), lambda b, i: (b * nq + i, 0)),
        out_shape=jax.ShapeDtypeStruct((nb * seq, d_out), BF16),
        scratch_shapes=[pltpu.VMEM((1, rows), F32), pltpu.VMEM((1, rows), F32),
                        pltpu.VMEM((d_out, bq), F32),
                        pltpu.VMEM((2, bk, rows), F32), pltpu.VMEM((2, 1, rows), F32)],
        compiler_params=_cparams(("parallel", "arbitrary")),
        name="attn_prompt",
    )(qc, qr, qc, qr, kc, vt, kr, kmc, vmt, kmr, g_mla)


def _attn_sample_body(npg, page, ds, n_part, pt_ref, qc_ref, qr_ref, cn_ref, rn_ref, wuv_ref, g_ref,
                      cache_c, cache_r, o_ref, pc_buf, pr_buf, sems, kc_scr, krt_scr, s_scr, mb_scr,
                      m_scr, l_scr, acc_scr):
    b = pl.program_id(0)
    j = pl.program_id(1)
    n_steps = pl.num_programs(1)
    step = b * n_steps + j
    n_total = pl.num_programs(0) * n_steps
    ring = pc_buf.shape[0]
    slot = lax.rem(step, ring)
    qc = qc_ref[0]
    qr = qr_ref[0]
    nt = (((1,), (1,)), ((), ()))
    per = npg // n_part

    def page_copies(bb, jj, sl):
        out = []
        for pg in range(npg):
            idx = pt_ref[bb, jj * npg + pg]
            out.append(pltpu.make_async_copy(cache_c.at[idx], pc_buf.at[sl, pg], sems.at[0, sl]))
            out.append(pltpu.make_async_copy(cache_r.at[idx], pr_buf.at[sl, pg], sems.at[1, sl]))
        return out

    for ahead in range(ring - 1):
        @pl.when(jnp.logical_and(step == 0, ahead < n_total))
        def _(ahead=ahead):
            for cp in page_copies(jnp.int32(ahead) // n_steps, lax.rem(jnp.int32(ahead), n_steps), ahead):
                cp.start()

    nxt = step + (ring - 1)

    @pl.when(nxt < n_total)
    def _():
        for cp in page_copies(nxt // n_steps, lax.rem(nxt, n_steps), lax.rem(nxt, ring)):
            cp.start()

    for cp in page_copies(b, j, slot):
        cp.wait()
    pc_refs = [pc_buf.at[slot, pg] for pg in range(npg)]
    pr_refs = [pr_buf.at[slot, pg] for pg in range(npg)]

    @pl.when(j == 0)
    def _():
        m_scr[...] = jnp.full_like(m_scr, -jnp.inf)
        l_scr[...] = jnp.zeros_like(l_scr)
        acc_scr[...] = jnp.zeros_like(acc_scr)

    def keys(part):
        return slice(part * per * page, (part + 1) * per * page)

    def produce(part):
        for pg in range(part * per, (part + 1) * per):
            kc_scr[pg * page:(pg + 1) * page, :] = pc_refs[pg][...].astype(BF16)
            krt_scr[:, pg * page:(pg + 1) * page] = pr_refs[pg][...].astype(BF16)
        s = (lax.dot_general(qc, kc_scr[keys(part), :], nt, preferred_element_type=F32)
             + jnp.dot(qr, krt_scr[:, keys(part)], preferred_element_type=F32))
        s_scr[part] = s
        mb_scr[part] = jnp.max(s, axis=-1, keepdims=True)

    def consume(part):
        s = s_scr[part]
        m_old = m_scr[...]
        m_new = jnp.maximum(m_old, mb_scr[part])
        alpha = jnp.exp(m_old - m_new)
        p = jnp.exp(s - m_new)
        l_scr[...] = alpha * l_scr[...] + jnp.sum(p, axis=-1, keepdims=True)
        acc_scr[...] = alpha * acc_scr[...] + jnp.dot(p.astype(BF16), kc_scr[keys(part), :],
                                                      preferred_element_type=F32)
        m_scr[...] = m_new

    produce(0)
    for part in range(n_part):
        if part + 1 < n_part:
            produce(part + 1)
        consume(part)

    @pl.when(j == pl.num_programs(1) - 1)
    def _():
        pad = LANES - ds
        kn = jnp.concatenate([cn_ref[0], jnp.zeros((pad, cn_ref.shape[-1]), F32)], axis=0).astype(BF16)
        rn = jnp.concatenate([rn_ref[0], jnp.zeros((pad, rn_ref.shape[-1]), F32)], axis=0).astype(BF16)
        s = _scores(qc, qr, kn, rn)
        t_q = lax.broadcasted_iota(jnp.int32, s.shape, 0) & (ds - 1)
        t_k = lax.broadcasted_iota(jnp.int32, s.shape, 1)
        s = jnp.where(t_k <= t_q, s, -jnp.inf)
        _softmax_step(s, kn, m_scr, l_scr, acc_scr)
        o_ref[0] = _attn_finish(ds, wuv_ref, g_ref, l_scr, acc_scr)


def _attn_sample(page_table, qc, qr, c_new, r_new, cache_c, cache_r, w_uv, g_mla):
    db, rows, d_kv = qc.shape
    ds = c_new.shape[1]
    n_pages = page_table.shape[1]
    page = cache_c.shape[1]
    npg = math.gcd(PAGES_PER_STEP, n_pages)
    n_part = math.gcd(SAMPLE_PARTS, npg)
    assert ds & (ds - 1) == 0 and ds <= LANES
    assert cache_r.shape[1:] == (MLA_ROPE_DIM, page)
    d_out = w_uv.shape[0] * w_uv.shape[2]
    bmap = lambda b, j, pt: (b, 0, 0)
    c2 = lambda b, j, pt: (0, 0)
    c3 = lambda b, j, pt: (0, 0, 0)
    in_specs = [
        pl.BlockSpec((1, rows, d_kv), bmap),
        pl.BlockSpec((1, rows, MLA_ROPE_DIM), bmap),
        pl.BlockSpec((1, ds, d_kv), bmap),
        pl.BlockSpec((1, ds, MLA_ROPE_DIM), bmap),
        pl.BlockSpec(w_uv.shape, c3),
        pl.BlockSpec(g_mla.shape, c2),
        pl.BlockSpec(memory_space=pl.ANY),
        pl.BlockSpec(memory_space=pl.ANY),
    ]
    part_keys = (npg // n_part) * page
    grid_spec = pltpu.PrefetchScalarGridSpec(
        num_scalar_prefetch=1,
        grid=(db, n_pages // npg),
        in_specs=in_specs,
        out_specs=pl.BlockSpec((1, ds, d_out), bmap),
        scratch_shapes=[pltpu.VMEM((SAMPLE_RING, npg, page, d_kv), F32),
                        pltpu.VMEM((SAMPLE_RING, npg, MLA_ROPE_DIM, page), F32),
                        pltpu.SemaphoreType.DMA((2, SAMPLE_RING)),
                        pltpu.VMEM((npg * page, d_kv), BF16), pltpu.VMEM((MLA_ROPE_DIM, npg * page), BF16),
                        pltpu.VMEM((n_part, rows, part_keys), F32), pltpu.VMEM((n_part, rows, 1), F32),
                        pltpu.VMEM((rows, 1), F32), pltpu.VMEM((rows, 1), F32), pltpu.VMEM((rows, d_kv), F32)],
    )
    return pl.pallas_call(
        functools.partial(_attn_sample_body, npg, page, ds, n_part),
        grid_spec=grid_spec,
        out_shape=jax.ShapeDtypeStruct((db, ds, d_out), F32),
        compiler_params=_cparams(("arbitrary", "arbitrary")),
        name="attn_sample",
    )(page_table, qc, qr, c_new, r_new, w_uv, g_mla, cache_c, cache_r)


def _cmul_add(cur, sh, a_r, a_i, half):
    return cur + a_r * sh + a_i * pltpu.roll(sh, half, axis=1)


def _ssm_prompt_body(nb, n_chunks, n_levels, pre, u_ref, um_ref, wy_ref, wd_ref, wc_ref, dv_ref, ar_ref, ai_ref,
                     y_ref, hl_ref, scr, e_scr):
    u = u_ref[0]
    ub = u.astype(BF16)
    half = wd_ref.shape[-1] // 2
    wd = wd_ref[0]
    d = jnp.dot(ub, wd, preferred_element_type=F32)
    h_meta = jnp.dot(um_ref[0].astype(BF16), wd, preferred_element_type=F32)[0:1]
    first = lax.broadcasted_iota(jnp.int32, (n_chunks, 1), 0) == 0
    for b in range(nb):
        scr[b, 0:pre, :] = jnp.zeros((pre, scr.shape[-1]), F32)
        scr[b, pre:pre + n_chunks, :] = d[b * n_chunks:(b + 1) * n_chunks]
    for b in range(nb):
        shifted = scr[b, pre - 1:pre - 1 + n_chunks, :]
        scr[b, pre:pre + n_chunks, :] = shifted + jnp.where(first, h_meta, 0.0)
    for k in range(n_levels):
        s = 1 << k
        for b in range(nb):
            cur = scr[b, pre:pre + n_chunks, :]
            sh = scr[b, pre - s:pre - s + n_chunks, :]
            scr[b, pre:pre + n_chunks, :] = _cmul_add(cur, sh, ar_ref[0, k:k + 1, :], ai_ref[0, k:k + 1, :], half)
    for b in range(nb):
        e_scr[b * n_chunks:(b + 1) * n_chunks, :] = scr[b, pre:pre + n_chunks, :]
    e = e_scr[...]
    y_ref[0] = (jnp.dot(ub, wy_ref[0], preferred_element_type=F32)
                + jnp.dot(e.astype(BF16), wc_ref[0], preferred_element_type=F32)
                + u * dv_ref[0])
    h_after = _cmul_add(d, e, ar_ref[0, 0:1, :], ai_ref[0, 0:1, :], half)
    for b in range(nb):
        last = (b + 1) * n_chunks - 1
        hl_ref[0, b:b + 1, :] = h_after[last:last + 1]


def _ssm_sample_body(u_ref, h_ref, wy_ref, wd_ref, wc_ref, dv_ref, ar_ref, ai_ref, y_ref, hl_ref):
    u = u_ref[0]
    ub = u.astype(BF16)
    e = h_ref[0]
    half = e.shape[-1] // 2
    d = jnp.dot(ub, wd_ref[0], preferred_element_type=F32)
    y_ref[0] = (jnp.dot(ub, wy_ref[0], preferred_element_type=F32)
                + jnp.dot(e.astype(BF16), wc_ref[0], preferred_element_type=F32)
                + u * dv_ref[0])
    hl_ref[0] = _cmul_add(d, e, ar_ref[0, 0:1, :], ai_ref[0, 0:1, :], half)


def _chunk_steps(n_groups, t_chunk, lane_order):
    j = jnp.arange(t_chunk, dtype=jnp.int32)[None, :]
    if not lane_order:
        return jnp.broadcast_to(j, (n_groups, t_chunk))
    per_slab = LANES // SSM_GROUP
    assert t_chunk % per_slab == 0
    gg = (jnp.arange(n_groups, dtype=jnp.int32) % per_slab)[:, None]
    return (j // per_slab) * per_slab + (j % per_slab - gg) % per_slab


def _ssm_weights(a_re, a_im, log_dt, b_re, b_im, c_re, c_im, d_skip, t_chunk, n_levels, lane_order=False):
    hi = lax.Precision.HIGHEST
    a = lax.complex(a_re.astype(F32), a_im.astype(F32))
    dt = jnp.exp(log_dt.astype(F32))[:, None]
    a_dt = a * dt
    a_bar = jnp.exp(a_dt)
    b_bar = ((a_bar - 1.0) / a)[..., None] * lax.complex(b_re.astype(F32), b_im.astype(F32))
    c = lax.complex(c_re.astype(F32), c_im.astype(F32))
    g, p_state, ch = b_bar.shape
    steps = _chunk_steps(g, t_chunk, lane_order)
    k = jnp.arange(t_chunk + 1, dtype=F32)
    a_pow = jnp.exp(a_dt[:, None, :] * k[None, :, None])
    pick = lambda idx: jnp.take_along_axis(a_pow, idx[:, :, None], axis=1)
    kern = jnp.einsum('gcp,gkp,gpd->gkdc', c, a_pow[:, :t_chunk], b_bar, precision=hi).real
    per = LANES // SSM_GROUP
    assert g % per == 0
    lag = steps[:per, None, :] - steps[:per, :, None]
    kern = kern.reshape(g // per, per, t_chunk, ch * ch).transpose(1, 2, 0, 3).reshape(per, t_chunk, -1)
    sel = jnp.clip(lag, 0, t_chunk - 1).reshape(per, t_chunk * t_chunk, 1)
    wy = jnp.take_along_axis(kern, sel, axis=1)
    wy = jnp.where((lag >= 0).reshape(per, t_chunk * t_chunk, 1), wy, 0.0)
    wy = wy.reshape(per, t_chunk, t_chunk, g // per, ch, ch)
    wy = wy.transpose(3, 0, 1, 4, 2, 5).reshape(g, t_chunk * ch, t_chunk * ch)
    wd = pick(t_chunk - 1 - steps)[:, :, None, :] * b_bar.transpose(0, 2, 1)[:, None]
    wd = wd.reshape(g, t_chunk * ch, p_state)
    wd = jnp.concatenate([wd.real, wd.imag], axis=-1)
    gm = c.transpose(0, 2, 1)[:, :, None, :] * pick(steps + 1).transpose(0, 2, 1)[:, :, :, None]
    gm = gm.reshape(g, p_state, t_chunk * ch)
    wc = jnp.concatenate([gm.real, -gm.imag], axis=1)
    dv = jnp.tile(d_skip.astype(F32).reshape(g, 1, ch), (1, 1, t_chunk))
    lev = (t_chunk * (2.0 ** jnp.arange(n_levels, dtype=F32)))
    a_lev = jnp.exp(a_dt[:, None, :] * lev[None, :, None])
    a_r = jnp.concatenate([a_lev.real, a_lev.real], axis=-1)
    a_i = jnp.concatenate([-a_lev.imag, a_lev.imag], axis=-1)
    return wy.astype(BF16), wd.astype(BF16), wc.astype(BF16), dv, a_r, a_i


def _ssm_prompt(u_rows, um_rows, nb, ops):
    wy, wd, wc, dv, a_r, a_i = ops
    g, r, tc = u_rows.shape
    n_chunks = r // nb
    n_levels = a_r.shape[1]
    assert (1 << n_levels) >= n_chunks and n_chunks % SUBLANES == 0
    st = wd.shape[-1]
    pre = -(-(1 << (n_levels - 1)) // SUBLANES) * SUBLANES
    gmap = lambda i: (i, 0, 0)
    return pl.pallas_call(
        functools.partial(_ssm_prompt_body, nb, n_chunks, n_levels, pre),
        grid=(g,),
        in_specs=[pl.BlockSpec((1, r, tc), gmap), pl.BlockSpec((1,) + um_rows.shape[1:], gmap),
                  pl.BlockSpec((1,) + wy.shape[1:], gmap),
                  pl.BlockSpec((1,) + wd.shape[1:], gmap), pl.BlockSpec((1,) + wc.shape[1:], gmap),
                  pl.BlockSpec((1,) + dv.shape[1:], gmap), pl.BlockSpec((1,) + a_r.shape[1:], gmap),
                  pl.BlockSpec((1,) + a_i.shape[1:], gmap)],
        out_specs=[pl.BlockSpec((1, r, tc), gmap), pl.BlockSpec((1, nb, st), gmap)],
        out_shape=[jax.ShapeDtypeStruct((g, r, tc), F32), jax.ShapeDtypeStruct((g, nb, st), F32)],
        scratch_shapes=[pltpu.VMEM((nb, pre + n_chunks, st), F32), pltpu.VMEM((r, st), F32)],
        compiler_params=_cparams(("parallel",)),
        name="ssm_prompt",
    )(u_rows, um_rows, wy, wd, wc, dv, a_r, a_i)


def _ssm_sample(u_rows, h_rows, ops):
    wy, wd, wc, dv, a_r, a_i = ops
    g, r, tc = u_rows.shape
    st = wd.shape[-1]
    gmap = lambda i: (i, 0, 0)
    return pl.pallas_call(
        _ssm_sample_body,
        grid=(g,),
        in_specs=[pl.BlockSpec((1, r, tc), gmap), pl.BlockSpec((1, r, st), gmap),
                  pl.BlockSpec((1,) + wy.shape[1:], gmap), pl.BlockSpec((1,) + wd.shape[1:], gmap),
                  pl.BlockSpec((1,) + wc.shape[1:], gmap), pl.BlockSpec((1,) + dv.shape[1:], gmap),
                  pl.BlockSpec((1,) + a_r.shape[1:], gmap), pl.BlockSpec((1,) + a_i.shape[1:], gmap)],
        out_specs=[pl.BlockSpec((1, r, tc), gmap), pl.BlockSpec((1, r, st), gmap)],
        out_shape=[jax.ShapeDtypeStruct((g, r, tc), F32), jax.ShapeDtypeStruct((g, r, st), F32)],
        compiler_params=_cparams(("parallel",)),
        name="ssm_sample",
    )(u_rows, h_rows, wy, wd, wc, dv, a_r, a_i)


def _mix_out_body(t_chunk, ys_ref, ya_ref, x_ref, wglu_ref, bglu_ref, gs_ref, wo_ref, gpost_ref, o_ref,
                  *maybe_scr):
    if t_chunk:
        (slab_scr,) = maybe_scr
        n_slabs = slab_scr.shape[0]
        n_rows = slab_scr.shape[1] // t_chunk
        per_slab = LANES // SSM_GROUP
        lane_blk = lax.broadcasted_iota(jnp.int32, (n_rows, LANES), 1) // SSM_GROUP
        for k in range(n_slabs):
            for t in range(t_chunk // per_slab):
                tiles = [ys_ref[k * per_slab + gg, :, t * LANES:(t + 1) * LANES] for gg in range(per_slab)]
                for s1 in range(per_slab):
                    w = tiles[(-s1) % per_slab]
                    for q in range(1, per_slab):
                        w = jnp.where(lane_blk == q, tiles[(q - s1) % per_slab], w)
                    if s1:
                        w = pltpu.roll(w, (per_slab - s1) * SSM_GROUP, axis=1)
                    slab_scr[k, pl.ds(t * per_slab + s1, n_rows, stride=t_chunk), :] = w
        ys = jnp.concatenate([slab_scr[k] for k in range(n_slabs)], axis=-1)
    else:
        ys = ys_ref[...]
    z = jax.nn.gelu(ys)
    gate = jax.nn.sigmoid(jnp.dot(z.astype(BF16), wglu_ref[...], preferred_element_type=F32) + bglu_ref[...])
    ns = _rms(z * gate, gs_ref[...]).astype(BF16)
    w = ns.shape[-1]
    y = (jnp.dot(ns, wo_ref[:w, :], preferred_element_type=F32)
         + jnp.dot(ya_ref[...].astype(BF16), wo_ref[w:, :], preferred_element_type=F32))
    o_ref[...] = x_ref[...] + _rms(y, gpost_ref[...])


def _mix_out(ys, ya, x, w_glu, b_glu, g_ssm, w_o, g_post, t_chunk, name):
    n, d = x.shape
    tm = _row_tile(n)
    row = lambda i: (i, 0)
    c2 = lambda i: (0, 0)
    if t_chunk:
        assert tm % (t_chunk * SUBLANES) == 0
        w = ys.shape[0] * SSM_GROUP
        assert w % LANES == 0
        ys_spec = pl.BlockSpec((ys.shape[0], tm // t_chunk, ys.shape[2]), lambda i: (0, i, 0))
        scratch = [pltpu.VMEM((w // LANES, tm, LANES), F32)]
    else:
        ys_spec = pl.BlockSpec((tm, ys.shape[1]), row)
        scratch = []
    return pl.pallas_call(
        functools.partial(_mix_out_body, t_chunk),
        grid=(n // tm,),
        in_specs=[ys_spec, pl.BlockSpec((tm, ya.shape[1]), row), pl.BlockSpec((tm, d), row),
                  pl.BlockSpec(w_glu.shape, c2), pl.BlockSpec(b_glu.shape, c2), pl.BlockSpec(g_ssm.shape, c2),
                  pl.BlockSpec(w_o.shape, c2), pl.BlockSpec(g_post.shape, c2)],
        out_specs=pl.BlockSpec((tm, d), row),
        out_shape=jax.ShapeDtypeStruct((n, d), F32),
        scratch_shapes=scratch,
        compiler_params=_cparams(("parallel",)),
        name=name,
    )(ys, ya, x, w_glu, b_glu, g_ssm, w_o, g_post)


def _rope_tables(pos):
    inv = ROPE_THETA ** (-jnp.arange(ROPE_HALF, dtype=F32) / ROPE_HALF)
    ang = pos.astype(F32)[:, None] * inv[None, :]
    reps = LANES // ROPE_HALF
    return jnp.tile(jnp.cos(ang), (1, reps)), jnp.tile(jnp.sin(ang), (1, reps))


def _regroup_uq_columns(w_uq):
    w = w_uq.reshape(w_uq.shape[0], MLA_HEADS, MLA_NOPE_DIM + MLA_ROPE_DIM)
    parts = (w[:, :, :MLA_NOPE_DIM], w[:, :, MLA_NOPE_DIM:MLA_NOPE_DIM + ROPE_HALF], w[:, :, MLA_NOPE_DIM + ROPE_HALF:])
    return jnp.concatenate([p.reshape(w_uq.shape[0], -1) for p in parts], axis=1)


def kernel(x_prompt, x_sample, cache_kv_latent, cache_k_rope, state_ssm_re, state_ssm_im, page_table, meta_tokens, g_ff1_pre, w_ff1_gate, w_ff1_up, w_ff1_down, g_ff1_post, g_mix_pre, w_in, ssm_a_re, ssm_a_im, ssm_log_dt, ssm_b_re, ssm_b_im, ssm_c_re, ssm_c_im, ssm_d, w_glu, b_glu, g_q_norm, w_uq, g_kv_norm, w_uk, w_uv, g_ssm_out, g_mla_out, w_o, g_mix_post, g_ff2_pre, w_ff2_gate, w_ff2_up, w_ff2_down, g_ff2_post):
    depth = w_in.shape[0]
    assert depth == 1, "single-layer step"
    bp, seq, d_model = x_prompt.shape
    db, ds, _ = x_sample.shape
    n_pages = page_table.shape[1]
    page = cache_kv_latent.shape[2]
    past_len = n_pages * page
    d_kv = cache_kv_latent.shape[3]
    d_q = w_uq.shape[1]
    n_groups, n_state = ssm_a_re.shape[1], ssm_a_re.shape[2]
    d_u = n_groups * SSM_GROUP
    dims = (d_u, d_q, d_kv)
    l = 0
    row = lambda v: v[l].reshape(1, -1).astype(F32)

    ff1_w = _ffn_weights(w_ff1_gate[l], w_ff1_up[l], w_ff1_down[l])
    ff2_w = _ffn_weights(w_ff2_gate[l], w_ff2_up[l], w_ff2_down[l])
    w_in_b = w_in[l].astype(BF16)
    w_uq_b = _regroup_uq_columns(w_uq[l]).astype(BF16)
    w_ukt = jnp.transpose(w_uk[l], (1, 2, 0)).astype(BF16)
    w_uv_b = jnp.transpose(w_uv[l], (1, 0, 2)).astype(BF16)
    w_uvt_b = w_uv[l].reshape(d_kv, -1).T.astype(BF16)
    w_glu_b = w_glu[l].astype(BF16)
    w_o_b = w_o[l].astype(BF16)

    xs = [x_prompt.reshape(bp * seq, d_model), x_sample.reshape(db * ds, d_model), meta_tokens.astype(F32)]
    names = ["prompt", "sample", "meta"]
    pos = [N_META + jnp.tile(jnp.arange(seq), bp), past_len + jnp.tile(jnp.arange(ds), db), jnp.arange(N_META)]

    t_p = SSM_CHUNK
    assert N_META == t_p and seq % t_p == 0
    x1, pr = [], []
    for x, nm, ps, t_rows in zip(xs, names, pos, (t_p, 0, 0)):
        y, hn = _ffn(x, row(g_ff1_pre), ff1_w, row(g_ff1_post), row(g_mix_pre), True, "ffn1_" + nm)
        cos, sin = _rope_tables(ps)
        x1.append(y)
        pr.append(_proj(hn, w_in_b, row(g_q_norm), w_uq_b, w_ukt, row(g_kv_norm), w_uvt_b, cos, sin, dims, t_rows,
                        "proj_" + nm))
    (_, ckv_p, ckvb_p, vt_p, kr_p, krb_p, qc_p, qr_p, u_rows) = pr[0]
    (u_s, ckv_s, _, _, kr_s, _, qc_s, qr_s) = pr[1]
    (u_m, ckv_m, ckvb_m, vt_m, kr_m, krb_m, _, _) = pr[2]

    n_levels = max(1, (seq // t_p - 1).bit_length())
    ssm_w = (ssm_a_re[l], ssm_a_im[l], ssm_log_dt[l], ssm_b_re[l], ssm_b_im[l], ssm_c_re[l], ssm_c_im[l], ssm_d[l])
    ops_p = _ssm_weights(*ssm_w, t_p, n_levels, True)
    ops_s = _ssm_weights(*ssm_w, ds, 1)
    tc = t_p * SSM_GROUP
    um_rows = u_m.reshape(t_p, n_groups, SSM_GROUP).transpose(1, 0, 2)
    um_rows = jnp.take_along_axis(um_rows, _chunk_steps(n_groups, t_p, True)[:, :, None], axis=1)
    um_rows = um_rows.reshape(n_groups, 1, tc)
    um_rows = jnp.pad(um_rows, ((0, 0), (0, SUBLANES - 1), (0, 0)))
    ys_p, hl_p = _ssm_prompt(u_rows, um_rows, bp, ops_p)
    hl_p = hl_p.transpose(1, 0, 2)

    us_rows = u_s.reshape(db, ds, n_groups, SSM_GROUP).transpose(2, 0, 1, 3).reshape(n_groups, db, ds * SSM_GROUP)
    h0_rows = jnp.concatenate([state_ssm_re[l], state_ssm_im[l]], axis=-1).astype(F32).transpose(1, 0, 2)
    ysr, hl_s = _ssm_sample(us_rows, h0_rows, ops_s)
    ys_s = ysr.reshape(n_groups, db, ds, SSM_GROUP).transpose(1, 2, 0, 3).reshape(db * ds, d_u)
    hl_s = hl_s.transpose(1, 0, 2)

    g_mla = row(g_mla_out)
    pad_m = LANES - N_META
    kmc = jnp.pad(ckvb_m, ((0, pad_m), (0, 0)))
    kmr = jnp.pad(krb_m, ((0, pad_m), (0, 0)))
    vmt = jnp.pad(vt_m, ((0, 0), (0, pad_m)))
    ya_p = _attn_prompt(qc_p, qr_p, ckvb_p.reshape(bp, seq, d_kv), vt_p,
                        krb_p.reshape(bp, seq, MLA_ROPE_DIM), kmc, vmt, kmr, g_mla)
    to_seq = lambda q: q.reshape(MLA_HEADS, db, ds, q.shape[-1]).transpose(1, 0, 2, 3).reshape(db, MLA_HEADS * ds, q.shape[-1])
    ya_s = _attn_sample(page_table, to_seq(qc_s), to_seq(qr_s), ckv_s.reshape(db, ds, d_kv),
                        kr_s.reshape(db, ds, MLA_ROPE_DIM), cache_kv_latent[l],
                        jnp.swapaxes(cache_k_rope[l], 1, 2), w_uv_b, g_mla)
    ya_s = ya_s.reshape(db * ds, -1)

    outs = []
    for x, ys, ya, t_rows, nm in ((x1[0], ys_p, ya_p, t_p, "prompt"), (x1[1], ys_s, ya_s, 0, "sample")):
        x2 = _mix_out(ys, ya, x, w_glu_b, row(b_glu), row(g_ssm_out), w_o_b, row(g_mix_post), t_rows,
                      "mix_out_" + nm)
        y, _ = _ffn(x2, row(g_ff2_pre), ff2_w, row(g_ff2_post), row(g_ff2_post), False, "ffn2_" + nm)
        outs.append(y)

    y_prompt = outs[0].reshape(bp, seq, d_model)
    y_sample = outs[1].reshape(db, ds, d_model)
    meta_b = lambda v: jnp.broadcast_to(v[None], (bp,) + v.shape)
    new_ckv_p = jnp.concatenate([meta_b(ckv_m), ckv_p.reshape(bp, seq, d_kv)], axis=1)[None]
    new_kr_p = jnp.concatenate([meta_b(kr_m), kr_p.reshape(bp, seq, MLA_ROPE_DIM)], axis=1)[None]
    return (y_prompt, y_sample, new_ckv_p, new_kr_p,
            hl_p[None, :, :, :n_state], hl_p[None, :, :, n_state:],
            ckv_s.reshape(1, db, ds, d_kv), kr_s.reshape(1, db, ds, MLA_ROPE_DIM),
            hl_s[None, :, :, :n_state], hl_s[None, :, :, n_state:])
```

```python
import functools
import math

import jax
import jax.numpy as jnp
from jax import lax
from jax.experimental import pallas as pl
from jax.experimental.pallas import tpu as pltpu

F32 = jnp.float32
BF16 = jnp.bfloat16

N_META = 16
SSM_GROUP = 16
MLA_HEADS = 8
MLA_NOPE_DIM = 64
MLA_ROPE_DIM = 32
ROPE_HALF = MLA_ROPE_DIM // 2
ROPE_THETA = 10000.0
RMS_EPS = 1e-6
ATTN_SCALE = (MLA_NOPE_DIM + MLA_ROPE_DIM) ** -0.5

LANES = 128
SUBLANES = 8
VMEM_LIMIT = 56 * 1024 * 1024

ROW_TILE = 1024
FFN_ROW_TILE = 1024
FF_TILE = 1408
ATTN_BQ = 256
ATTN_BK = 512
ATTN_COLS = 512
PAGES_PER_STEP = 128
SAMPLE_PARTS = 2
SAMPLE_RING = 2
SSM_CHUNK = 16


def _rms(x, g):
    return (x * lax.rsqrt(jnp.mean(x * x, axis=-1, keepdims=True) + RMS_EPS)) * g


def _row_tile(n):
    return ROW_TILE if n % ROW_TILE == 0 else n


def _cparams(sem):
    return pltpu.CompilerParams(dimension_semantics=sem, vmem_limit_bytes=VMEM_LIMIT)


def _ffn_body(emit_norm, x_ref, gpre_ref, wg_ref, wu_ref, wd_ref, gpost_ref, gnext_ref, *refs):
    if emit_norm:
        y_ref, hn_ref, h_scr, acc_scr = refs
    else:
        y_ref, h_scr, acc_scr = refs
        hn_ref = None
    j = pl.program_id(1)

    @pl.when(j == 0)
    def _():
        h_scr[...] = _rms(x_ref[...], gpre_ref[...]).astype(BF16)
        acc_scr[...] = jnp.zeros_like(acc_scr)

    h = h_scr[...]
    g = jnp.dot(h, wg_ref[...], preferred_element_type=F32)
    u = jnp.dot(h, wu_ref[...], preferred_element_type=F32)
    a = (g * jax.nn.sigmoid(g)) * u
    acc_scr[...] += jnp.dot(a.astype(BF16), wd_ref[...], preferred_element_type=F32)

    @pl.when(j == pl.num_programs(1) - 1)
    def _():
        y = x_ref[...] + 0.5 * _rms(acc_scr[...], gpost_ref[...])
        y_ref[...] = y
        if emit_norm:
            hn_ref[...] = _rms(y, gnext_ref[...]).astype(BF16)


def _ffn_weights(wg, wu, wd):
    d, d_ff = wg.shape
    tf = FF_TILE
    assert d_ff % tf == 0
    chunk = lambda w: w.astype(BF16).reshape(d, d_ff // tf, tf).transpose(1, 0, 2)
    return chunk(wg), chunk(wu), wd.astype(BF16).reshape(d_ff // tf, tf, d)


def _ffn(x, g_pre, weights, g_post, g_next, emit_norm, name):
    wg, wu, wd = weights
    n, d = x.shape
    nj, _, tf = wg.shape
    tm = FFN_ROW_TILE if n % FFN_ROW_TILE == 0 else n
    row = lambda i, j: (i, 0)
    vec = lambda i, j: (0, 0)
    chunk = lambda i, j: (j, 0, 0)
    out_shape = [jax.ShapeDtypeStruct((n, d), F32)]
    out_specs = [pl.BlockSpec((tm, d), row)]
    if emit_norm:
        out_shape.append(jax.ShapeDtypeStruct((n, d), BF16))
        out_specs.append(pl.BlockSpec((tm, d), row))
    res = pl.pallas_call(
        functools.partial(_ffn_body, emit_norm),
        grid=(n // tm, nj),
        in_specs=[
            pl.BlockSpec((tm, d), row),
            pl.BlockSpec((1, d), vec),
            pl.BlockSpec((None, d, tf), chunk),
            pl.BlockSpec((None, d, tf), chunk),
            pl.BlockSpec((None, tf, d), chunk),
            pl.BlockSpec((1, d), vec),
            pl.BlockSpec((1, d), vec),
        ],
        out_specs=out_specs,
        out_shape=out_shape,
        scratch_shapes=[pltpu.VMEM((tm, d), BF16), pltpu.VMEM((tm, d), F32)],
        compiler_params=_cparams(("parallel", "arbitrary")),
        name=name,
    )(x, g_pre, wg, wu, wd, g_post, g_next)
    return res if emit_norm else (res[0], None)


def _proj_body(d_u, d_q, d_kv, t_chunk, h_ref, win_ref, gq_ref, wuq_ref, wukt_ref, gkv_ref, wuvt_ref,
               cos_ref, sin_ref,
               u_ref, ckv_ref, ckvb_ref, vt_ref, kr_ref, krb_ref, qc_ref, qr_ref, *maybe_urows):
    proj = jnp.dot(h_ref[...], win_ref[...], preferred_element_type=F32)
    off_q, off_kv, off_kr = d_u, d_u + d_q, d_u + d_q + d_kv
    u_ref[...] = proj[:, :off_q]
    if t_chunk:
        urows_ref, slab_scr = maybe_urows
        n_rows = u_ref.shape[0] // t_chunk
        per_slab = LANES // SSM_GROUP
        lane_blk = lax.broadcasted_iota(jnp.int32, (n_rows, LANES), 1) // SSM_GROUP
        for k in range(d_u // LANES):
            slab_scr[k] = proj[:, k * LANES:(k + 1) * LANES]
        for k in range(d_u // LANES):
            rolled = []
            for s in range(t_chunk):
                x = slab_scr[k, pl.ds(s, n_rows, stride=t_chunk), :]
                shift = (s % per_slab) * SSM_GROUP
                rolled.append(pltpu.roll(x, shift, axis=1) if shift else x)
            for gg in range(per_slab):
                tiles = []
                for t in range(t_chunk // per_slab):
                    tile = rolled[t * per_slab]
                    for s1 in range(1, per_slab):
                        tile = jnp.where(lane_blk == (gg + s1) % per_slab, rolled[t * per_slab + s1], tile)
                    tiles.append(tile)
                urows_ref[k * per_slab + gg] = jnp.concatenate(tiles, axis=-1)
    cq = _rms(proj[:, off_q:off_kv], gq_ref[...]).astype(BF16)
    q = jnp.dot(cq, wuq_ref[...], preferred_element_type=F32) * ATTN_SCALE
    n_nope = MLA_HEADS * MLA_NOPE_DIM
    for hd in range(MLA_HEADS):
        qn = q[:, hd * MLA_NOPE_DIM:(hd + 1) * MLA_NOPE_DIM].astype(BF16)
        qc_ref[hd] = jnp.dot(qn, wukt_ref[hd], preferred_element_type=F32).astype(BF16)
    cos = cos_ref[...]
    sin = sin_ref[...]
    r1 = q[:, n_nope:n_nope + LANES]
    r2 = q[:, n_nope + LANES:n_nope + 2 * LANES]
    o1 = r1 * cos - r2 * sin
    o2 = r2 * cos + r1 * sin
    for hd in range(MLA_HEADS):
        sl = slice(hd * ROPE_HALF, (hd + 1) * ROPE_HALF)
        qr_ref[hd] = jnp.concatenate([o1[:, sl], o2[:, sl]], axis=-1).astype(BF16)
    ckv = _rms(proj[:, off_kv:off_kr], gkv_ref[...])
    ckv_ref[...] = ckv
    ckvb = ckv.astype(BF16)
    ckvb_ref[...] = ckvb
    vt_ref[...] = lax.dot_general(wuvt_ref[...], ckvb, (((1,), (1,)), ((), ())),
                                  preferred_element_type=F32).astype(BF16)
    x1 = proj[:, off_kr:off_kr + ROPE_HALF]
    x2 = proj[:, off_kr + ROPE_HALF:off_kr + MLA_ROPE_DIM]
    c16 = cos[:, :ROPE_HALF]
    s16 = sin[:, :ROPE_HALF]
    kr = jnp.concatenate([x1 * c16 - x2 * s16, x2 * c16 + x1 * s16], axis=-1)
    kr_ref[...] = kr
    krb_ref[...] = kr.astype(BF16)


def _proj(hn, w_in, g_q, w_uq, w_ukt, g_kv, w_uvt, cos, sin, dims, t_chunk, name):
    n, d = hn.shape
    d_u, d_q, d_kv = dims
    tm = _row_tile(n)
    row = lambda i: (i, 0)
    full2 = lambda i: (0, 0)
    full3 = lambda i: (0, 0, 0)
    hrow = lambda i: (0, i, 0)
    extra_specs, extra_shapes, scratch = [], [], []
    if t_chunk:
        assert tm % (t_chunk * SUBLANES) == 0 and d_u % LANES == 0
        n_groups = d_u // SSM_GROUP
        extra_specs = [pl.BlockSpec((n_groups, tm // t_chunk, t_chunk * SSM_GROUP), hrow)]
        extra_shapes = [jax.ShapeDtypeStruct((n_groups, n // t_chunk, t_chunk * SSM_GROUP), F32)]
        scratch = [pltpu.VMEM((d_u // LANES, tm, LANES), F32)]
    return pl.pallas_call(
        functools.partial(_proj_body, d_u, d_q, d_kv, t_chunk),
        grid=(n // tm,),
        in_specs=[
            pl.BlockSpec((tm, d), row),
            pl.BlockSpec(w_in.shape, full2),
            pl.BlockSpec(g_q.shape, full2),
            pl.BlockSpec(w_uq.shape, full2),
            pl.BlockSpec(w_ukt.shape, full3),
            pl.BlockSpec(g_kv.shape, full2),
            pl.BlockSpec(w_uvt.shape, full2),
            pl.BlockSpec((tm, LANES), row),
            pl.BlockSpec((tm, LANES), row),
        ],
        out_specs=[
            pl.BlockSpec((tm, d_u), row),
            pl.BlockSpec((tm, d_kv), row),
            pl.BlockSpec((tm, d_kv), row),
            pl.BlockSpec((w_uvt.shape[0], tm), lambda i: (0, i)),
            pl.BlockSpec((tm, MLA_ROPE_DIM), row),
            pl.BlockSpec((tm, MLA_ROPE_DIM), row),
            pl.BlockSpec((MLA_HEADS, tm, d_kv), hrow),
            pl.BlockSpec((MLA_HEADS, tm, MLA_ROPE_DIM), hrow),
        ] + extra_specs,
        out_shape=[
            jax.ShapeDtypeStruct((n, d_u), F32),
            jax.ShapeDtypeStruct((n, d_kv), F32),
            jax.ShapeDtypeStruct((n, d_kv), BF16),
            jax.ShapeDtypeStruct((w_uvt.shape[0], n), BF16),
            jax.ShapeDtypeStruct((n, MLA_ROPE_DIM), F32),
            jax.ShapeDtypeStruct((n, MLA_ROPE_DIM), BF16),
            jax.ShapeDtypeStruct((MLA_HEADS, n, d_kv), BF16),
            jax.ShapeDtypeStruct((MLA_HEADS, n, MLA_ROPE_DIM), BF16),
        ] + extra_shapes,
        scratch_shapes=scratch,
        compiler_params=_cparams(("parallel",)),
        name=name,
    )(hn, w_in, g_q, w_uq, w_ukt, g_kv, w_uvt, cos, sin)


def _scores(qc, qr, kc, kr):
    nt = (((1,), (1,)), ((), ()))
    return (lax.dot_general(qc, kc, nt, preferred_element_type=F32)
            + lax.dot_general(qr, kr, nt, preferred_element_type=F32))


def _softmax_step(s, kc, m_scr, l_scr, acc_scr):
    m_old = m_scr[...]
    m_new = jnp.maximum(m_old, jnp.max(s, axis=-1, keepdims=True))
    alpha = jnp.exp(m_old - m_new)
    p = jnp.exp(s - m_new)
    l_scr[...] = alpha * l_scr[...] + jnp.sum(p, axis=-1, keepdims=True)
    acc_scr[...] = alpha * acc_scr[...] + jnp.dot(p.astype(BF16), kc, preferred_element_type=F32)
    m_scr[...] = m_new


def _attn_finish(rows_per_head, wuv_ref, g_ref, l_scr, acc_scr):
    o = (acc_scr[...] / l_scr[...]).astype(BF16)
    outs = []
    for hd in range(MLA_HEADS):
        oh = o[hd * rows_per_head:(hd + 1) * rows_per_head]
        outs.append(jnp.dot(oh, wuv_ref[hd], preferred_element_type=F32))
    return _rms(jnp.concatenate(outs, axis=-1), g_ref[...])


def _attn_cols_step(parts, m_blk, cols, heads, bq, m_scr, l_scr, acc_scr):
    m_old = m_scr[:, cols]
    m_new = jnp.maximum(m_old, m_blk)
    alpha = jnp.exp(m_old - m_new)
    ps = [jnp.exp(s - m_new) for s, _ in parts]
    l_new = alpha * l_scr[:, cols]
    for p in ps:
        l_new = l_new + jnp.sum(p, axis=0, keepdims=True)
    l_scr[:, cols] = l_new
    m_scr[:, cols] = m_new
    v_dim = acc_scr.shape[0] // MLA_HEADS
    for n, hd in enumerate(heads):
        hc = slice(n * bq, (n + 1) * bq)
        rows = slice(hd * v_dim, (hd + 1) * v_dim)
        acc = alpha[:, hc] * acc_scr[rows, :]
        for p, (_, vt) in zip(ps, parts):
            acc = acc + jnp.dot(vt[rows, :], p[:, hc].astype(BF16), preferred_element_type=F32)
        acc_scr[rows, :] = acc


def _attn_prompt_body(bq, bk, ncol, qc_ref, qr_ref, qcn_ref, qrn_ref, kc_ref, vt_ref, kr_ref, kmc_ref, vmt_ref,
                      kmr_ref, g_ref, o_ref, m_scr, l_scr, acc_scr, s_scr, mb_scr):
    i = pl.program_id(1)
    nt = (((1,), (1,)), ((), ()))
    heads_per = ncol // bq
    n_groups = MLA_HEADS // heads_per
    groups = [slice(gi * ncol, (gi + 1) * ncol) for gi in range(n_groups)]
    own_q = (qc_ref, qr_ref)
    next_q = (qcn_ref, qrn_ref)

    def scores(gi, kc, kr, q_refs=own_q):
        hs = slice(gi * heads_per, (gi + 1) * heads_per)
        qc = q_refs[0][hs].reshape(ncol, qc_ref.shape[-1])
        qr = q_refs[1][hs].reshape(ncol, qr_ref.shape[-1])
        return (lax.dot_general(kc, qc, nt, preferred_element_type=F32)
                + lax.dot_general(kr, qr, nt, preferred_element_type=F32))

    def produce(j, slot, q_refs=own_q):
        start = pl.multiple_of(j * bk, bk)
        kc = kc_ref[0, pl.ds(start, bk), :]
        kr = kr_ref[0, pl.ds(start, bk), :]
        for gi, cols in enumerate(groups):
            s = scores(gi, kc, kr, q_refs)
            s_scr[slot, :, cols] = s
            mb_scr[slot, :, cols] = jnp.max(s, axis=0, keepdims=True)

    def meta_scores():
        out = []
        for gi in range(n_groups):
            sm = scores(gi, kmc_ref[...], kmr_ref[...])
            out.append(jnp.where(lax.broadcasted_iota(jnp.int32, sm.shape, 0) < N_META, sm, -jnp.inf))
        return out

    def consume(j, slot, diagonal, meta=None):
        start = pl.multiple_of(j * bk, bk)
        vt = vt_ref[:, pl.ds(start, bk)]
        for gi, cols in enumerate(groups):
            s = s_scr[slot, :, cols]
            if diagonal:
                k_pos = start + lax.broadcasted_iota(jnp.int32, s.shape, 0)
                q_pos = i * bq + (lax.broadcasted_iota(jnp.int32, s.shape, 1) & (bq - 1))
                s = jnp.where(k_pos <= q_pos, s, -jnp.inf)
                sm = meta[gi]
                m_blk = jnp.maximum(jnp.max(s, axis=0, keepdims=True), jnp.max(sm, axis=0, keepdims=True))
                parts = [(s, vt), (sm, vmt_ref[...])]
            else:
                m_blk = mb_scr[slot, :, cols]
                parts = [(s, vt)]
            heads = range(gi * heads_per, (gi + 1) * heads_per)
            _attn_cols_step(parts, m_blk, cols, heads, bq, m_scr, l_scr, acc_scr)

    ratio = bk // bq
    n_full = i // ratio
    odd = (n_full & 1) == 1
    m_scr[...] = jnp.full_like(m_scr, -jnp.inf)
    l_scr[...] = jnp.zeros_like(l_scr)
    acc_scr[...] = jnp.zeros_like(acc_scr)

    base = ((i // (2 * ratio)) * ratio + jnp.minimum(lax.rem(i, 2 * ratio), ratio)) & 1

    @pl.when(i == 0)
    def _():
        produce(0, 0)

    def run(b0):
        def pair(k, carry):
            j = 2 * k
            produce(j + 1, 1 - b0)
            consume(j, b0, False)
            produce(j + 2, b0)
            consume(j + 1, 1 - b0, False)
            return carry

        lax.fori_loop(0, n_full // 2, pair, 0)

        @pl.when(odd)
        def _():
            produce(n_full, 1 - b0)
            consume(n_full - 1, b0, False)
            meta = meta_scores()
            produce(0, b0, next_q)
            consume(n_full, 1 - b0, True, meta)

        @pl.when(jnp.logical_not(odd))
        def _():
            meta = meta_scores()
            produce(0, 1 - b0, next_q)
            consume(n_full, b0, True, meta)

    for b0 in (0, 1):
        pl.when(base == b0)(functools.partial(run, b0))

    v_dim = acc_scr.shape[0] // MLA_HEADS
    outs = [acc_scr[hd * v_dim:(hd + 1) * v_dim, :] / l_scr[:, hd * bq:(hd + 1) * bq] for hd in range(MLA_HEADS)]
    y = jnp.transpose(jnp.concatenate(outs, axis=0))
    o_ref[...] = _rms(y, g_ref[...]).astype(o_ref.dtype)


def _attn_prompt(qc, qr, kc, vt, kr, kmc, vmt, kmr, g_mla):
    nb, seq, d_kv = kc.shape
    bq, bk, ncol = ATTN_BQ, ATTN_BK, ATTN_COLS
    assert seq % bk == 0 and bk % bq == 0 and bq & (bq - 1) == 0 and N_META >= 1
    assert ncol % bq == 0 and (MLA_HEADS * bq) % ncol == 0
    nq = seq // bq
    rows = MLA_HEADS * bq
    d_out = vt.shape[0]
    qmap = lambda b, i: (0, b * nq + i, 0)
    qnext = lambda b, i: (0, b * nq + jnp.minimum(i + 1, nq - 1), 0)
    kmap = lambda b, i: (b, 0, 0)
    c2 = lambda b, i: (0, 0)
    return pl.pallas_call(
        functools.partial(_attn_prompt_body, bq, bk, ncol),
        grid=(nb, nq),
        in_specs=[
            pl.BlockSpec((MLA_HEADS, bq, d_kv), qmap),
            pl.BlockSpec((MLA_HEADS, bq, MLA_ROPE_DIM), qmap),
            pl.BlockSpec((MLA_HEADS, bq, d_kv), qnext),
            pl.BlockSpec((MLA_HEADS, bq, MLA_ROPE_DIM), qnext),
            pl.BlockSpec((1, seq, d_kv), kmap),
            pl.BlockSpec((d_out, seq), lambda b, i: (0, b)),
            pl.BlockSpec((1, seq, MLA_ROPE_DIM), kmap),
            pl.BlockSpec(kmc.shape, c2),
            pl.BlockSpec(vmt.shape, c2),
            pl.BlockSpec(kmr.shape, c2),
            pl.BlockSpec(g_mla.shape, c2),
        ],
        out_specs=pl.BlockSpec((bq, d_out), lambda b, i: (b * nq + i, 0)),
        out_shape=jax.ShapeDtypeStruct((nb * seq, d_out), BF16),
        scratch_shapes=[pltpu.VMEM((1, rows), F32), pltpu.VMEM((1, rows), F32),
                        pltpu.VMEM((d_out, bq), F32),
                        pltpu.VMEM((2, bk, rows), F32), pltpu.VMEM((2, 1, rows), F32)],
        compiler_params=_cparams(("parallel", "arbitrary")),
        name="attn_prompt",
    )(qc, qr, qc, qr, kc, vt, kr, kmc, vmt, kmr, g_mla)


def _attn_sample_body(npg, page, ds, n_part, pt_ref, qc_ref, qr_ref, cn_ref, rn_ref, wuv_ref, g_ref,
                      cache_c, cache_r, o_ref, pc_buf, pr_buf, sems, kc_scr, krt_scr, s_scr, mb_scr,
                      m_scr, l_scr, acc_scr):
    b = pl.program_id(0)
    j = pl.program_id(1)
    n_steps = pl.num_programs(1)
    step = b * n_steps + j
    n_total = pl.num_programs(0) * n_steps
    ring = pc_buf.shape[0]
    slot = lax.rem(step, ring)
    qc = qc_ref[0]
    qr = qr_ref[0]
    nt = (((1,), (1,)), ((), ()))
    per = npg // n_part

    def page_copies(bb, jj, sl):
        out = []
        for pg in range(npg):
            idx = pt_ref[bb, jj * npg + pg]
            out.append(pltpu.make_async_copy(cache_c.at[idx], pc_buf.at[sl, pg], sems.at[0, sl]))
            out.append(pltpu.make_async_copy(cache_r.at[idx], pr_buf.at[sl, pg], sems.at[1, sl]))
        return out

    for ahead in range(ring - 1):
        @pl.when(jnp.logical_and(step == 0, ahead < n_total))
        def _(ahead=ahead):
            for cp in page_copies(jnp.int32(ahead) // n_steps, lax.rem(jnp.int32(ahead), n_steps), ahead):
                cp.start()

    nxt = step + (ring - 1)

    @pl.when(nxt < n_total)
    def _():
        for cp in page_copies(nxt // n_steps, lax.rem(nxt, n_steps), lax.rem(nxt, ring)):
            cp.start()

    for cp in page_copies(b, j, slot):
        cp.wait()
    pc_refs = [pc_buf.at[slot, pg] for pg in range(npg)]
    pr_refs = [pr_buf.at[slot, pg] for pg in range(npg)]

    @pl.when(j == 0)
    def _():
        m_scr[...] = jnp.full_like(m_scr, -jnp.inf)
        l_scr[...] = jnp.zeros_like(l_scr)
        acc_scr[...] = jnp.zeros_like(acc_scr)

    def keys(part):
        return slice(part * per * page, (part + 1) * per * page)

    def produce(part):
        for pg in range(part * per, (part + 1) * per):
            kc_scr[pg * page:(pg + 1) * page, :] = pc_refs[pg][...].astype(BF16)
            krt_scr[:, pg * page:(pg + 1) * page] = pr_refs[pg][...].astype(BF16)
        s = (lax.dot_general(qc, kc_scr[keys(part), :], nt, preferred_element_type=F32)
             + jnp.dot(qr, krt_scr[:, keys(part)], preferred_element_type=F32))
        s_scr[part] = s
        mb_scr[part] = jnp.max(s, axis=-1, keepdims=True)

    def consume(part):
        s = s_scr[part]
        m_old = m_scr[...]
        m_new = jnp.maximum(m_old, mb_scr[part])
        alpha = jnp.exp(m_old - m_new)
        p = jnp.exp(s - m_new)
        l_scr[...] = alpha * l_scr[...] + jnp.sum(p, axis=-1, keepdims=True)
        acc_scr[...] = alpha * acc_scr[...] + jnp.dot(p.astype(BF16), kc_scr[keys(part), :],
                                                      preferred_element_type=F32)
        m_scr[...] = m_new

    produce(0)
    for part in range(n_part):
        if part + 1 < n_part:
            produce(part + 1)
        consume(part)

    @pl.when(j == pl.num_programs(1) - 1)
    def _():
        pad = LANES - ds
        kn = jnp.concatenate([cn_ref[0], jnp.zeros((pad, cn_ref.shape[-1]), F32)], axis=0).astype(BF16)
        rn = jnp.concatenate([rn_ref[0], jnp.zeros((pad, rn_ref.shape[-1]), F32)], axis=0).astype(BF16)
        s = _scores(qc, qr, kn, rn)
        t_q = lax.broadcasted_iota(jnp.int32, s.shape, 0) & (ds - 1)
        t_k = lax.broadcasted_iota(jnp.int32, s.shape, 1)
        s = jnp.where(t_k <= t_q, s, -jnp.inf)
        _softmax_step(s, kn, m_scr, l_scr, acc_scr)
        o_ref[0] = _attn_finish(ds, wuv_ref, g_ref, l_scr, acc_scr)


def _attn_sample(page_table, qc, qr, c_new, r_new, cache_c, cache_r, w_uv, g_mla):
    db, rows, d_kv = qc.shape
    ds = c_new.shape[1]
    n_pages = page_table.shape[1]
    page = cache_c.shape[1]
    npg = math.gcd(PAGES_PER_STEP, n_pages)
    n_part = math.gcd(SAMPLE_PARTS, npg)
    assert ds & (ds - 1) == 0 and ds <= LANES
    assert cache_r.shape[1:] == (MLA_ROPE_DIM, page)
    d_out = w_uv.shape[0] * w_uv.shape[2]
    bmap = lambda b, j, pt: (b, 0, 0)
    c2 = lambda b, j, pt: (0, 0)
    c3 = lambda b, j, pt: (0, 0, 0)
    in_specs = [
        pl.BlockSpec((1, rows, d_kv), bmap),
        pl.BlockSpec((1, rows, MLA_ROPE_DIM), bmap),
        pl.BlockSpec((1, ds, d_kv), bmap),
        pl.BlockSpec((1, ds, MLA_ROPE_DIM), bmap),
        pl.BlockSpec(w_uv.shape, c3),
        pl.BlockSpec(g_mla.shape, c2),
        pl.BlockSpec(memory_space=pl.ANY),
        pl.BlockSpec(memory_space=pl.ANY),
    ]
    part_keys = (npg // n_part) * page
    grid_spec = pltpu.PrefetchScalarGridSpec(
        num_scalar_prefetch=1,
        grid=(db, n_pages // npg),
        in_specs=in_specs,
        out_specs=pl.BlockSpec((1, ds, d_out), bmap),
        scratch_shapes=[pltpu.VMEM((SAMPLE_RING, npg, page, d_kv), F32),
                        pltpu.VMEM((SAMPLE_RING, npg, MLA_ROPE_DIM, page), F32),
                        pltpu.SemaphoreType.DMA((2, SAMPLE_RING)),
                        pltpu.VMEM((npg * page, d_kv), BF16), pltpu.VMEM((MLA_ROPE_DIM, npg * page), BF16),
                        pltpu.VMEM((n_part, rows, part_keys), F32), pltpu.VMEM((n_part, rows, 1), F32),
                        pltpu.VMEM((rows, 1), F32), pltpu.VMEM((rows, 1), F32), pltpu.VMEM((rows, d_kv), F32)],
    )
    return pl.pallas_call(
        functools.partial(_attn_sample_body, npg, page, ds, n_part),
        grid_spec=grid_spec,
        out_shape=jax.ShapeDtypeStruct((db, ds, d_out), F32),
        compiler_params=_cparams(("arbitrary", "arbitrary")),
        name="attn_sample",
    )(page_table, qc, qr, c_new, r_new, w_uv, g_mla, cache_c, cache_r)


def _cmul_add(cur, sh, a_r, a_i, half):
    return cur + a_r * sh + a_i * pltpu.roll(sh, half, axis=1)


def _ssm_prompt_body(nb, n_chunks, n_levels, pre, u_ref, um_ref, wy_ref, wd_ref, wc_ref, dv_ref, ar_ref, ai_ref,
                     y_ref, hl_ref, scr, e_scr):
    u = u_ref[0]
    ub = u.astype(BF16)
    half = wd_ref.shape[-1] // 2
    wd = wd_ref[0]
    d = jnp.dot(ub, wd, preferred_element_type=F32)
    h_meta = jnp.dot(um_ref[0].astype(BF16), wd, preferred_element_type=F32)[0:1]
    first = lax.broadcasted_iota(jnp.int32, (n_chunks, 1), 0) == 0
    for b in range(nb):
        scr[b, 0:pre, :] = jnp.zeros((pre, scr.shape[-1]), F32)
        scr[b, pre:pre + n_chunks, :] = d[b * n_chunks:(b + 1) * n_chunks]
    for b in range(nb):
        shifted = scr[b, pre - 1:pre - 1 + n_chunks, :]
        scr[b, pre:pre + n_chunks, :] = shifted + jnp.where(first, h_meta, 0.0)
    for k in range(n_levels):
        s = 1 << k
        for b in range(nb):
            cur = scr[b, pre:pre + n_chunks, :]
            sh = scr[b, pre - s:pre - s + n_chunks, :]
            scr[b, pre:pre + n_chunks, :] = _cmul_add(cur, sh, ar_ref[0, k:k + 1, :], ai_ref[0, k:k + 1, :], half)
    for b in range(nb):
        e_scr[b * n_chunks:(b + 1) * n_chunks, :] = scr[b, pre:pre + n_chunks, :]
    e = e_scr[...]
    y_ref[0] = (jnp.dot(ub, wy_ref[0], preferred_element_type=F32)
                + jnp.dot(e.astype(BF16), wc_ref[0], preferred_element_type=F32)
                + u * dv_ref[0])
    h_after = _cmul_add(d, e, ar_ref[0, 0:1, :], ai_ref[0, 0:1, :], half)
    for b in range(nb):
        last = (b + 1) * n_chunks - 1
        hl_ref[0, b:b + 1, :] = h_after[last:last + 1]


def _ssm_sample_body(u_ref, h_ref, wy_ref, wd_ref, wc_ref, dv_ref, ar_ref, ai_ref, y_ref, hl_ref):
    u = u_ref[0]
    ub = u.astype(BF16)
    e = h_ref[0]
    half = e.shape[-1] // 2
    d = jnp.dot(ub, wd_ref[0], preferred_element_type=F32)
    y_ref[0] = (jnp.dot(ub, wy_ref[0], preferred_element_type=F32)
                + jnp.dot(e.astype(BF16), wc_ref[0], preferred_element_type=F32)
                + u * dv_ref[0])
    hl_ref[0] = _cmul_add(d, e, ar_ref[0, 0:1, :], ai_ref[0, 0:1, :], half)


def _chunk_steps(n_groups, t_chunk, lane_order):
    j = jnp.arange(t_chunk, dtype=jnp.int32)[None, :]
    if not lane_order:
        return jnp.broadcast_to(j, (n_groups, t_chunk))
    per_slab = LANES // SSM_GROUP
    assert t_chunk % per_slab == 0
    gg = (jnp.arange(n_groups, dtype=jnp.int32) % per_slab)[:, None]
    return (j // per_slab) * per_slab + (j % per_slab - gg) % per_slab


def _ssm_weights(a_re, a_im, log_dt, b_re, b_im, c_re, c_im, d_skip, t_chunk, n_levels, lane_order=False):
    hi = lax.Precision.HIGHEST
    a = lax.complex(a_re.astype(F32), a_im.astype(F32))
    dt = jnp.exp(log_dt.astype(F32))[:, None]
    a_dt = a * dt
    a_bar = jnp.exp(a_dt)
    b_bar = ((a_bar - 1.0) / a)[..., None] * lax.complex(b_re.astype(F32), b_im.astype(F32))
    c = lax.complex(c_re.astype(F32), c_im.astype(F32))
    g, p_state, ch = b_bar.shape
    steps = _chunk_steps(g, t_chunk, lane_order)
    k = jnp.arange(t_chunk + 1, dtype=F32)
    a_pow = jnp.exp(a_dt[:, None, :] * k[None, :, None])
    pick = lambda idx: jnp.take_along_axis(a_pow, idx[:, :, None], axis=1)
    kern = jnp.einsum('gcp,gkp,gpd->gkdc', c, a_pow[:, :t_chunk], b_bar, precision=hi).real
    per = LANES // SSM_GROUP
    assert g % per == 0
    lag = steps[:per, None, :] - steps[:per, :, None]
    kern = kern.reshape(g // per, per, t_chunk, ch * ch).transpose(1, 2, 0, 3).reshape(per, t_chunk, -1)
    sel = jnp.clip(lag, 0, t_chunk - 1).reshape(per, t_chunk * t_chunk, 1)
    wy = jnp.take_along_axis(kern, sel, axis=1)
    wy = jnp.where((lag >= 0).reshape(per, t_chunk * t_chunk, 1), wy, 0.0)
    wy = wy.reshape(per, t_chunk, t_chunk, g // per, ch, ch)
    wy = wy.transpose(3, 0, 1, 4, 2, 5).reshape(g, t_chunk * ch, t_chunk * ch)
    wd = pick(t_chunk - 1 - steps)[:, :, None, :] * b_bar.transpose(0, 2, 1)[:, None]
    wd = wd.reshape(g, t_chunk * ch, p_state)
    wd = jnp.concatenate([wd.real, wd.imag], axis=-1)
    gm = c.transpose(0, 2, 1)[:, :, None, :] * pick(steps + 1).transpose(0, 2, 1)[:, :, :, None]
    gm = gm.reshape(g, p_state, t_chunk * ch)
    wc = jnp.concatenate([gm.real, -gm.imag], axis=1)
    dv = jnp.tile(d_skip.astype(F32).reshape(g, 1, ch), (1, 1, t_chunk))
    lev = (t_chunk * (2.0 ** jnp.arange(n_levels, dtype=F32)))
    a_lev = jnp.exp(a_dt[:, None, :] * lev[None, :, None])
    a_r = jnp.concatenate([a_lev.real, a_lev.real], axis=-1)
    a_i = jnp.concatenate([-a_lev.imag, a_lev.imag], axis=-1)
    return wy.astype(BF16), wd.astype(BF16), wc.astype(BF16), dv, a_r, a_i


def _ssm_prompt(u_rows, um_rows, nb, ops):
    wy, wd, wc, dv, a_r, a_i = ops
    g, r, tc = u_rows.shape
    n_chunks = r // nb
    n_levels = a_r.shape[1]
    assert (1 << n_levels) >= n_chunks and n_chunks % SUBLANES == 0
    st = wd.shape[-1]
    pre = -(-(1 << (n_levels - 1)) // SUBLANES) * SUBLANES
    gmap = lambda i: (i, 0, 0)
    return pl.pallas_call(
        functools.partial(_ssm_prompt_body, nb, n_chunks, n_levels, pre),
        grid=(g,),
        in_specs=[pl.BlockSpec((1, r, tc), gmap), pl.BlockSpec((1,) + um_rows.shape[1:], gmap),
                  pl.BlockSpec((1,) + wy.shape[1:], gmap),
                  pl.BlockSpec((1,) + wd.shape[1:], gmap), pl.BlockSpec((1,) + wc.shape[1:], gmap),
                  pl.BlockSpec((1,) + dv.shape[1:], gmap), pl.BlockSpec((1,) + a_r.shape[1:], gmap),
                  pl.BlockSpec((1,) + a_i.shape[1:], gmap)],
        out_specs=[pl.BlockSpec((1, r, tc), gmap), pl.BlockSpec((1, nb, st), gmap)],
        out_shape=[jax.ShapeDtypeStruct((g, r, tc), F32), jax.ShapeDtypeStruct((g, nb, st), F32)],
        scratch_shapes=[pltpu.VMEM((nb, pre + n_chunks, st), F32), pltpu.VMEM((r, st), F32)],
        compiler_params=_cparams(("parallel",)),
        name="ssm_prompt",
    )(u_rows, um_rows, wy, wd, wc, dv, a_r, a_i)


def _ssm_sample(u_rows, h_rows, ops):
    wy, wd, wc, dv, a_r, a_i = ops
    g, r, tc = u_rows.shape
    st = wd.shape[-1]
    gmap = lambda i: (i, 0, 0)
    return pl.pallas_call(
        _ssm_sample_body,
        grid=(g,),
        in_specs=[pl.BlockSpec((1, r, tc), gmap), pl.BlockSpec((1, r, st), gmap),
                  pl.BlockSpec((1,) + wy.shape[1:], gmap), pl.BlockSpec((1,) + wd.shape[1:], gmap),
                  pl.BlockSpec((1,) + wc.shape[1:], gmap), pl.BlockSpec((1,) + dv.shape[1:], gmap),
                  pl.BlockSpec((1,) + a_r.shape[1:], gmap), pl.BlockSpec((1,) + a_i.shape[1:], gmap)],
        out_specs=[pl.BlockSpec((1, r, tc), gmap), pl.BlockSpec((1, r, st), gmap)],
        out_shape=[jax.ShapeDtypeStruct((g, r, tc), F32), jax.ShapeDtypeStruct((g, r, st), F32)],
        compiler_params=_cparams(("parallel",)),
        name="ssm_sample",
    )(u_rows, h_rows, wy, wd, wc, dv, a_r, a_i)


def _mix_out_body(t_chunk, ys_ref, ya_ref, x_ref, wglu_ref, bglu_ref, gs_ref, wo_ref, gpost_ref, o_ref,
                  *maybe_scr):
    if t_chunk:
        (slab_scr,) = maybe_scr
        n_slabs = slab_scr.shape[0]
        n_rows = slab_scr.shape[1] // t_chunk
        per_slab = LANES // SSM_GROUP
        lane_blk = lax.broadcasted_iota(jnp.int32, (n_rows, LANES), 1) // SSM_GROUP
        for k in range(n_slabs):
            for t in range(t_chunk // per_slab):
                tiles = [ys_ref[k * per_slab + gg, :, t * LANES:(t + 1) * LANES] for gg in range(per_slab)]
                for s1 in range(per_slab):
                    w = tiles[(-s1) % per_slab]
                    for q in range(1, per_slab):
                        w = jnp.where(lane_blk == q, tiles[(q - s1) % per_slab], w)
                    if s1:
                        w = pltpu.roll(w, (per_slab - s1) * SSM_GROUP, axis=1)
                    slab_scr[k, pl.ds(t * per_slab + s1, n_rows, stride=t_chunk), :] = w
        ys = jnp.concatenate([slab_scr[k] for k in range(n_slabs)], axis=-1)
    else:
        ys = ys_ref[...]
    z = jax.nn.gelu(ys)
    gate = jax.nn.sigmoid(jnp.dot(z.astype(BF16), wglu_ref[...], preferred_element_type=F32) + bglu_ref[...])
    ns = _rms(z * gate, gs_ref[...]).astype(BF16)
    w = ns.shape[-1]
    y = (jnp.dot(ns, wo_ref[:w, :], preferred_element_type=F32)
         + jnp.dot(ya_ref[...].astype(BF16), wo_ref[w:, :], preferred_element_type=F32))
    o_ref[...] = x_ref[...] + _rms(y, gpost_ref[...])


def _mix_out(ys, ya, x, w_glu, b_glu, g_ssm, w_o, g_post, t_chunk, name):
    n, d = x.shape
    tm = _row_tile(n)
    row = lambda i: (i, 0)
    c2 = lambda i: (0, 0)
    if t_chunk:
        assert tm % (t_chunk * SUBLANES) == 0
        w = ys.shape[0] * SSM_GROUP
        assert w % LANES == 0
        ys_spec = pl.BlockSpec((ys.shape[0], tm // t_chunk, ys.shape[2]), lambda i: (0, i, 0))
        scratch = [pltpu.VMEM((w // LANES, tm, LANES), F32)]
    else:
        ys_spec = pl.BlockSpec((tm, ys.shape[1]), row)
        scratch = []
    return pl.pallas_call(
        functools.partial(_mix_out_body, t_chunk),
        grid=(n // tm,),
        in_specs=[ys_spec, pl.BlockSpec((tm, ya.shape[1]), row), pl.BlockSpec((tm, d), row),
                  pl.BlockSpec(w_glu.shape, c2), pl.BlockSpec(b_glu.shape, c2), pl.BlockSpec(g_ssm.shape, c2),
                  pl.BlockSpec(w_o.shape, c2), pl.BlockSpec(g_post.shape, c2)],
        out_specs=pl.BlockSpec((tm, d), row),
        out_shape=jax.ShapeDtypeStruct((n, d), F32),
        scratch_shapes=scratch,
        compiler_params=_cparams(("parallel",)),
        name=name,
    )(ys, ya, x, w_glu, b_glu, g_ssm, w_o, g_post)


def _rope_tables(pos):
    inv = ROPE_THETA ** (-jnp.arange(ROPE_HALF, dtype=F32) / ROPE_HALF)
    ang = pos.astype(F32)[:, None] * inv[None, :]
    reps = LANES // ROPE_HALF
    return jnp.tile(jnp.cos(ang), (1, reps)), jnp.tile(jnp.sin(ang), (1, reps))


def _regroup_uq_columns(w_uq):
    w = w_uq.reshape(w_uq.shape[0], MLA_HEADS, MLA_NOPE_DIM + MLA_ROPE_DIM)
    parts = (w[:, :, :MLA_NOPE_DIM], w[:, :, MLA_NOPE_DIM:MLA_NOPE_DIM + ROPE_HALF], w[:, :, MLA_NOPE_DIM + ROPE_HALF:])
    return jnp.concatenate([p.reshape(w_uq.shape[0], -1) for p in parts], axis=1)


def kernel(x_prompt, x_sample, cache_kv_latent, cache_k_rope, state_ssm_re, state_ssm_im, page_table, meta_tokens, g_ff1_pre, w_ff1_gate, w_ff1_up, w_ff1_down, g_ff1_post, g_mix_pre, w_in, ssm_a_re, ssm_a_im, ssm_log_dt, ssm_b_re, ssm_b_im, ssm_c_re, ssm_c_im, ssm_d, w_glu, b_glu, g_q_norm, w_uq, g_kv_norm, w_uk, w_uv, g_ssm_out, g_mla_out, w_o, g_mix_post, g_ff2_pre, w_ff2_gate, w_ff2_up, w_ff2_down, g_ff2_post):
    depth = w_in.shape[0]
    assert depth == 1, "single-layer step"
    bp, seq, d_model = x_prompt.shape
    db, ds, _ = x_sample.shape
    n_pages = page_table.shape[1]
    page = cache_kv_latent.shape[2]
    past_len = n_pages * page
    d_kv = cache_kv_latent.shape[3]
    d_q = w_uq.shape[1]
    n_groups, n_state = ssm_a_re.shape[1], ssm_a_re.shape[2]
    d_u = n_groups * SSM_GROUP
    dims = (d_u, d_q, d_kv)
    l = 0
    row = lambda v: v[l].reshape(1, -1).astype(F32)

    ff1_w = _ffn_weights(w_ff1_gate[l], w_ff1_up[l], w_ff1_down[l])
    ff2_w = _ffn_weights(w_ff2_gate[l], w_ff2_up[l], w_ff2_down[l])
    w_in_b = w_in[l].astype(BF16)
    w_uq_b = _regroup_uq_columns(w_uq[l]).astype(BF16)
    w_ukt = jnp.transpose(w_uk[l], (1, 2, 0)).astype(BF16)
    w_uv_b = jnp.transpose(w_uv[l], (1, 0, 2)).astype(BF16)
    w_uvt_b = w_uv[l].reshape(d_kv, -1).T.astype(BF16)
    w_glu_b = w_glu[l].astype(BF16)
    w_o_b = w_o[l].astype(BF16)

    xs = [x_prompt.reshape(bp * seq, d_model), x_sample.reshape(db * ds, d_model), meta_tokens.astype(F32)]
    names = ["prompt", "sample", "meta"]
    pos = [N_META + jnp.tile(jnp.arange(seq), bp), past_len + jnp.tile(jnp.arange(ds), db), jnp.arange(N_META)]

    t_p = SSM_CHUNK
    assert N_META == t_p and seq % t_p == 0
    x1, pr = [], []
    for x, nm, ps, t_rows in zip(xs, names, pos, (t_p, 0, 0)):
        y, hn = _ffn(x, row(g_ff1_pre), ff1_w, row(g_ff1_post), row(g_mix_pre), True, "ffn1_" + nm)
        cos, sin = _rope_tables(ps)
        x1.append(y)
        pr.append(_proj(hn, w_in_b, row(g_q_norm), w_uq_b, w_ukt, row(g_kv_norm), w_uvt_b, cos, sin, dims, t_rows,
                        "proj_" + nm))
    (_, ckv_p, ckvb_p, vt_p, kr_p, krb_p, qc_p, qr_p, u_rows) = pr[0]
    (u_s, ckv_s, _, _, kr_s, _, qc_s, qr_s) = pr[1]
    (u_m, ckv_m, ckvb_m, vt_m, kr_m, krb_m, _, _) = pr[2]

    n_levels = max(1, (seq // t_p - 1).bit_length())
    ssm_w = (ssm_a_re[l], ssm_a_im[l], ssm_log_dt[l], ssm_b_re[l], ssm_b_im[l], ssm_c_re[l], ssm_c_im[l], ssm_d[l])
    ops_p = _ssm_weights(*ssm_w, t_p, n_levels, True)
    ops_s = _ssm_weights(*ssm_w, ds, 1)
    tc = t_p * SSM_GROUP
    um_rows = u_m.reshape(t_p, n_groups, SSM_GROUP).transpose(1, 0, 2)
    um_rows = jnp.take_along_axis(um_rows, _chunk_steps(n_groups, t_p, True)[:, :, None], axis=1)
    um_rows = um_rows.reshape(n_groups, 1, tc)
    um_rows = jnp.pad(um_rows, ((0, 0), (0, SUBLANES - 1), (0, 0)))
    ys_p, hl_p = _ssm_prompt(u_rows, um_rows, bp, ops_p)
    hl_p = hl_p.transpose(1, 0, 2)

    us_rows = u_s.reshape(db, ds, n_groups, SSM_GROUP).transpose(2, 0, 1, 3).reshape(n_groups, db, ds * SSM_GROUP)
    h0_rows = jnp.concatenate([state_ssm_re[l], state_ssm_im[l]], axis=-1).astype(F32).transpose(1, 0, 2)
    ysr, hl_s = _ssm_sample(us_rows, h0_rows, ops_s)
    ys_s = ysr.reshape(n_groups, db, ds, SSM_GROUP).transpose(1, 2, 0, 3).reshape(db * ds, d_u)
    hl_s = hl_s.transpose(1, 0, 2)

    g_mla = row(g_mla_out)
    pad_m = LANES - N_META
    kmc = jnp.pad(ckvb_m, ((0, pad_m), (0, 0)))
    kmr = jnp.pad(krb_m, ((0, pad_m), (0, 0)))
    vmt = jnp.pad(vt_m, ((0, 0), (0, pad_m)))
    ya_p = _attn_prompt(qc_p, qr_p, ckvb_p.reshape(bp, seq, d_kv), vt_p,
                        krb_p.reshape(bp, seq, MLA_ROPE_DIM), kmc, vmt, kmr, g_mla)
    to_seq = lambda q: q.reshape(MLA_HEADS, db, ds, q.shape[-1]).transpose(1, 0, 2, 3).reshape(db, MLA_HEADS * ds, q.shape[-1])
    ya_s = _attn_sample(page_table, to_seq(qc_s), to_seq(qr_s), ckv_s.reshape(db, ds, d_kv),
                        kr_s.reshape(db, ds, MLA_ROPE_DIM), cache_kv_latent[l],
                        jnp.swapaxes(cache_k_rope[l], 1, 2), w_uv_b, g_mla)
    ya_s = ya_s.reshape(db * ds, -1)

    outs = []
    for x, ys, ya, t_rows, nm in ((x1[0], ys_p, ya_p, t_p, "prompt"), (x1[1], ys_s, ya_s, 0, "sample")):
        x2 = _mix_out(ys, ya, x, w_glu_b, row(b_glu), row(g_ssm_out), w_o_b, row(g_mix_post), t_rows,
                      "mix_out_" + nm)
        y, _ = _ffn(x2, row(g_ff2_pre), ff2_w, row(g_ff2_post), row(g_ff2_post), False, "ffn2_" + nm)
        outs.append(y)

    y_prompt = outs[0].reshape(bp, seq, d_model)
    y_sample = outs[1].reshape(db, ds, d_model)
    meta_b = lambda v: jnp.broadcast_to(v[None], (bp,) + v.shape)
    new_ckv_p = jnp.concatenate([meta_b(ckv_m), ckv_p.reshape(bp, seq, d_kv)], axis=1)[None]
    new_kr_p = jnp.concatenate([meta_b(kr_m), kr_p.reshape(bp, seq, MLA_ROPE_DIM)], axis=1)[None]
    return (y_prompt, y_sample, new_ckv_p, new_kr_p,
            hl_p[None, :, :, :n_state], hl_p[None, :, :, n_state:],
            ckv_s.reshape(1, db, ds, d_kv), kr_s.reshape(1, db, ds, MLA_ROPE_DIM),
            hl_s[None, :, :, :n_state], hl_s[None, :, :, n_state:])
```
